```python
import math
import jax
import jax.numpy as jnp
from jax import lax
import numpy as np

D_MODEL = 2048
BATCH = 2
SEQ = 4096
DEPTH = 1
DEC_BATCH = 128
DEC_SEQ = 8
PAST_LEN = 2048
PAGE_SIZE = 128

N_HEADS = 16
N_KV_HEADS = 4
HEAD_DIM = 128
GROUP = N_HEADS // N_KV_HEADS
MOBA_BLOCK = 256
MOBA_TOPK = 3
MOBA_Q_CHUNK = 32
DN_HEADS = 16
DN_DK = 128
DN_DV = 128
DN_CHUNK = 64
CONV_W = 4
Q_A = N_HEADS * HEAD_DIM
KV_A = N_KV_HEADS * HEAD_DIM
DN_QK = DN_HEADS * DN_DK
DN_V = DN_HEADS * DN_DV
CONV_DIM = 2 * DN_QK + DN_V
PROJ_SIZES = (Q_A, KV_A, KV_A, CONV_DIM, DN_V, DN_HEADS, DN_HEADS, D_MODEL, D_MODEL)
PROJ_DIM = Q_A + 2 * KV_A + CONV_DIM + DN_V + 2 * DN_HEADS + 2 * D_MODEL
D_FF = 4 * D_MODEL
EPS = 1e-6

kernel_name = 'moba_gated_deltanet_hybrid_step'


def rms_norm(x, g):
    xf = x.astype(jnp.float32)
    y = xf * lax.rsqrt(jnp.mean(xf * xf, axis=-1, keepdims=True) + EPS)
    return y.astype(x.dtype) * g


def l2_normalize(x):
    return x * lax.rsqrt(jnp.sum(x * x, axis=-1, keepdims=True) + EPS)


def alibi_slopes():
    return jnp.asarray(2.0 ** (-8.0 * np.arange(1, N_HEADS + 1) / N_HEADS), dtype=jnp.float32)


def split_points():
    return [int(s) for s in np.cumsum(PROJ_SIZES)[:-1]]


def moba_sequence(q, k, v, p0):
    t_len, length = q.shape[0], k.shape[0]
    n_blk = -(-length // MOBA_BLOCK)
    qc = math.gcd(t_len, MOBA_Q_CHUNK)
    n_sel = min(MOBA_TOPK, n_blk)
    span = MOBA_BLOCK + qc
    pad = n_blk * MOBA_BLOCK + qc - length
    k = jnp.pad(k, ((0, pad), (0, 0), (0, 0)))
    v = jnp.pad(v, ((0, pad), (0, 0), (0, 0)))
    kb = k[:n_blk * MOBA_BLOCK].reshape(n_blk, MOBA_BLOCK, N_KV_HEADS, HEAD_DIM)
    vb = v[:n_blk * MOBA_BLOCK].reshape(n_blk, MOBA_BLOCK, N_KV_HEADS, HEAD_DIM)
    k_mean = jnp.mean(kb.astype(jnp.float32), axis=1)
    kb_h = jnp.transpose(kb, (2, 0, 1, 3))
    vb_h = jnp.transpose(vb, (2, 0, 1, 3))
    slopes = alibi_slopes().reshape(N_KV_HEADS, GROUP)
    head_ix = jnp.arange(N_KV_HEADS)[None, :, None, None]
    blk_off = jnp.arange(MOBA_BLOCK)
    scale = HEAD_DIM ** -0.5

    def query_block(args):
        q_blk, ci = args
        pos = p0 + ci * qc + jnp.arange(qc)
        own = pos // MOBA_BLOCK
        qg = q_blk.reshape(qc, N_KV_HEADS, GROUP, HEAD_DIM)
        gate = jnp.einsum('qcgd,ncd->qcgn', qg.astype(jnp.float32), k_mean)
        past_blk = jnp.arange(n_blk)[None, :] < own[:, None]
        gate = jnp.where(past_blk[:, None, None, :], gate, -jnp.inf)
        top_val, top_idx = lax.top_k(gate, n_sel)
        sel_ok = jnp.isfinite(top_val)
        k_sel = kb_h[head_ix, top_idx]
        v_sel = vb_h[head_ix, top_idx]
        kpos_sel = top_idx[..., None] * MOBA_BLOCK + blk_off
        s_sel = jnp.einsum('qcgd,qcgnjd->qcgnj', qg, k_sel).astype(jnp.float32) * scale
        s_sel = s_sel - slopes[None, :, :, None, None] * (pos[:, None, None, None, None] - kpos_sel).astype(jnp.float32)
        s_sel = jnp.where(sel_ok[..., None], s_sel, -jnp.inf).reshape(qc, N_KV_HEADS, GROUP, n_sel * MOBA_BLOCK)
        start = (pos[0] // MOBA_BLOCK) * MOBA_BLOCK
        k_own = lax.dynamic_slice_in_dim(k, start, span, 0)
        v_own = lax.dynamic_slice_in_dim(v, start, span, 0)
        kpos_own = start + jnp.arange(span)
        own_ok = (kpos_own[None, :] // MOBA_BLOCK == own[:, None]) & (kpos_own[None, :] <= pos[:, None])
        s_own = jnp.einsum('qcgd,scd->qcgs', qg, k_own).astype(jnp.float32) * scale
        s_own = s_own - slopes[None, :, :, None] * (pos[:, None] - kpos_own[None, :]).astype(jnp.float32)[:, None, None, :]
        s_own = jnp.where(own_ok[:, None, None, :], s_own, -jnp.inf)
        p = jax.nn.softmax(jnp.concatenate([s_sel, s_own], axis=-1), axis=-1).astype(v.dtype)
        p_sel = p[..., :n_sel * MOBA_BLOCK].reshape(qc, N_KV_HEADS, GROUP, n_sel, MOBA_BLOCK)
        p_own = p[..., n_sel * MOBA_BLOCK:]
        o = jnp.einsum('qcgnj,qcgnjd->qcgd', p_sel, v_sel) + jnp.einsum('qcgs,scd->qcgd', p_own, v_own)
        return o.reshape(qc, N_HEADS, HEAD_DIM)

    n_q = t_len // qc
    out = lax.map(query_block, (q.reshape(n_q, qc, N_HEADS, HEAD_DIM), jnp.arange(n_q)))
    return out.reshape(t_len, N_HEADS, HEAD_DIM)


def moba_prompt(q, k, v):
    return lax.map(lambda a: moba_sequence(a[0], a[1], a[2], 0), (q, k, v))


def moba_sample(q, k, v, cache_k, cache_v, page_table):
    past = page_table.shape[1] * cache_k.shape[1]

    def one(a):
        q_s, k_s, v_s, pages = a
        k_past = cache_k[pages].reshape(past, N_KV_HEADS, HEAD_DIM)
        v_past = cache_v[pages].reshape(past, N_KV_HEADS, HEAD_DIM)
        k_full = jnp.concatenate([k_past.astype(k_s.dtype), k_s], axis=0)
        v_full = jnp.concatenate([v_past.astype(v_s.dtype), v_s], axis=0)
        return moba_sequence(q_s, k_full, v_full, past)

    return lax.map(one, (q, k, v, page_table))


def causal_conv_silu(x, buf, w):
    t = x.shape[1]
    xp = jnp.concatenate([buf.astype(x.dtype), x], axis=1)
    y = xp[:, 0:t] * w[0]
    for i in range(1, CONV_W):
        y = y + xp[:, i:i + t] * w[i]
    return jax.nn.silu(y), xp[:, t:]


def gated_delta_rule(q, k, v, g, beta, s0):
    b, t, h, _ = q.shape
    dv = v.shape[-1]
    c = math.gcd(t, DN_CHUNK)
    n = t // c

    def blocks(z):
        z = jnp.moveaxis(z, 2, 1)
        return z.reshape(b, h, n, c, *z.shape[3:])

    q_c, k_c, v_c, g_c, b_c = blocks(q), blocks(k), blocks(v), blocks(g), blocks(beta)
    cum = jnp.cumsum(g_c, axis=-1)
    incl = jnp.tril(jnp.ones((c, c), dtype=bool))
    strict = jnp.tril(jnp.ones((c, c), dtype=bool), -1)
    decay = jnp.exp(jnp.where(incl, cum[..., :, None] - cum[..., None, :], -jnp.inf))
    a_mat = jnp.where(strict, b_c[..., :, None] * decay * jnp.einsum('bhnid,bhnjd->bhnij', k_c, k_c), 0.0)
    lhs = jnp.eye(c, dtype=q.dtype) + a_mat
    gam = jnp.exp(cum)
    w_blk = lax.linalg.triangular_solve(lhs, (b_c * gam)[..., None] * k_c, left_side=True, lower=True, unit_diagonal=True)
    u_blk = lax.linalg.triangular_solve(lhs, b_c[..., None] * v_c, left_side=True, lower=True, unit_diagonal=True)
    qk = jnp.where(incl, jnp.einsum('bhnid,bhnjd->bhnij', q_c, k_c) * decay, 0.0)
    q_dec = q_c * gam[..., None]
    k_dec = k_c * jnp.exp(cum[..., -1:] - cum)[..., None]
    g_end = gam[..., -1]

    def step(s, xs):
        w_i, u_i, qk_i, q_i, k_i, ge = xs
        u = u_i - jnp.einsum('bhck,bhkv->bhcv', w_i, s)
        o = jnp.einsum('bhck,bhkv->bhcv', q_i, s) + jnp.einsum('bhcj,bhjv->bhcv', qk_i, u)
        s = ge[..., None, None] * s + jnp.einsum('bhck,bhcv->bhkv', k_i, u)
        return s, o

    xs = tuple(jnp.moveaxis(z, 2, 0) for z in (w_blk, u_blk, qk, q_dec, k_dec, g_end))
    s_final, o = lax.scan(step, s0, xs)
    o = jnp.moveaxis(o, 0, 2).reshape(b, h, t, dv)
    return jnp.moveaxis(o, 1, 2), s_final


def decoder_layer(x, conv_buf, s_init, attend, w_in, conv_w, a_log, dt_bias, dn_norm_w,
                  w_branch_attn, w_branch_delta, w_out, g_mix_pre, g_mix_post, g_mlp_pre, g_mlp_post,
                  w_up, w_down):
    b, t, _ = x.shape
    f32 = jnp.float32
    h = rms_norm(x, g_mix_pre)
    proj = h @ w_in
    q_a, k_a, v_a, qkv_d, z_d, beta_d, alpha_d, gate_a, gate_d = jnp.split(proj, split_points(), axis=-1)
    q_a = q_a.reshape(b, t, N_HEADS, HEAD_DIM)
    k_a = k_a.reshape(b, t, N_KV_HEADS, HEAD_DIM)
    v_a = v_a.reshape(b, t, N_KV_HEADS, HEAD_DIM)
    o_a = attend(q_a, k_a, v_a)
    y_a = o_a.reshape(b, t, Q_A) @ w_branch_attn
    qkv_d, conv_state = causal_conv_silu(qkv_d, conv_buf, conv_w)
    q_d, k_d, v_d = jnp.split(qkv_d, [DN_QK, 2 * DN_QK], axis=-1)
    q_d = l2_normalize(q_d.reshape(b, t, DN_HEADS, DN_DK).astype(f32)) * (DN_DK ** -0.5)
    k_d = l2_normalize(k_d.reshape(b, t, DN_HEADS, DN_DK).astype(f32))
    v_d = v_d.reshape(b, t, DN_HEADS, DN_DV).astype(f32)
    beta = jax.nn.sigmoid(beta_d.astype(f32))
    g = -jnp.exp(a_log.astype(f32)) * jax.nn.softplus(alpha_d.astype(f32) + dt_bias.astype(f32))
    o_d, s_final = gated_delta_rule(q_d, k_d, v_d, g, beta, s_init.astype(f32))
    o_d = rms_norm(o_d, dn_norm_w.astype(f32)) * jax.nn.silu(z_d.reshape(b, t, DN_HEADS, DN_DV).astype(f32))
    y_d = o_d.reshape(b, t, DN_V).astype(x.dtype) @ w_branch_delta
    mixed = jax.nn.sigmoid(gate_a) * y_a + jax.nn.sigmoid(gate_d) * y_d
    x = x + rms_norm(mixed @ w_out, g_mix_post)
    h = rms_norm(x, g_mlp_pre)
    x = x + rms_norm(jnp.square(jax.nn.relu(h @ w_up)) @ w_down, g_mlp_post)
    return x, k_a, v_a, s_final, conv_state


def setup_inputs(seed: int = 0) -> dict:
    key = jax.random.key(seed)
    ks = jax.random.split(key, 24)
    f32 = jnp.float32
    n_pages = PAST_LEN // PAGE_SIZE
    n_pool = (5 * DEC_BATCH * n_pages) // 4

    def nrm(k, shape, s):
        return jax.random.normal(k, shape, f32) * s

    page_table = jax.random.permutation(ks[6], n_pool)[:DEC_BATCH * n_pages].reshape(DEC_BATCH, n_pages).astype(jnp.int32)
    dt = jnp.exp(jax.random.uniform(ks[10], (DEPTH, DN_HEADS), f32, math.log(1e-3), math.log(1e-1)))
    return {
        'x_prompt': nrm(ks[0], (BATCH, SEQ, D_MODEL), 1.0),
        'x_sample': nrm(ks[1], (DEC_BATCH, DEC_SEQ, D_MODEL), 1.0),
        'cache_k': nrm(ks[2], (DEPTH, n_pool, PAGE_SIZE, N_KV_HEADS, HEAD_DIM), 1.0),
        'cache_v': nrm(ks[3], (DEPTH, n_pool, PAGE_SIZE, N_KV_HEADS, HEAD_DIM), 1.0),
        'state_delta': nrm(ks[4], (DEPTH, DEC_BATCH, DN_HEADS, DN_DK, DN_DV), 0.05),
        'state_conv': nrm(ks[5], (DEPTH, DEC_BATCH, CONV_W - 1, CONV_DIM), 1.0),
        'page_table': page_table,
        'w_in': nrm(ks[7], (DEPTH, D_MODEL, PROJ_DIM), D_MODEL ** -0.5),
        'conv_w': nrm(ks[8], (DEPTH, CONV_W, CONV_DIM), CONV_W ** -0.5),
        'a_log': jnp.log(jax.random.uniform(ks[9], (DEPTH, DN_HEADS), f32, 1.0, 16.0)),
        'dt_bias': dt + jnp.log(-jnp.expm1(-dt)),
        'dn_norm_w': 1.0 + nrm(ks[11], (DEPTH, DN_DV), 0.02),
        'w_branch_attn': nrm(ks[12], (DEPTH, Q_A, D_MODEL), Q_A ** -0.5),
        'w_branch_delta': nrm(ks[13], (DEPTH, DN_V, D_MODEL), DN_V ** -0.5),
        'w_out': nrm(ks[14], (DEPTH, D_MODEL, D_MODEL), D_MODEL ** -0.5),
        'g_mix_pre': 1.0 + nrm(ks[15], (DEPTH, D_MODEL), 0.02),
        'g_mix_post': 1.0 + nrm(ks[16], (DEPTH, D_MODEL), 0.02),
        'g_mlp_pre': 1.0 + nrm(ks[17], (DEPTH, D_MODEL), 0.02),
        'g_mlp_post': 1.0 + nrm(ks[18], (DEPTH, D_MODEL), 0.02),
        'w_up': nrm(ks[19], (DEPTH, D_MODEL, D_FF), D_MODEL ** -0.5),
        'w_down': nrm(ks[20], (DEPTH, D_FF, D_MODEL), D_FF ** -0.5),
    }


def reference(x_prompt, x_sample, cache_k, cache_v, state_delta, state_conv, page_table,
              w_in, conv_w, a_log, dt_bias, dn_norm_w, w_branch_attn, w_branch_delta, w_out,
              g_mix_pre, g_mix_post, g_mlp_pre, g_mlp_post, w_up, w_down):
    b_p = x_prompt.shape[0]
    y_p, y_s = x_prompt, x_sample
    k_p, v_p, d_p, c_p = [], [], [], []
    k_s, v_s, d_s, c_s = [], [], [], []
    for l in range(DEPTH):
        wl = (w_in[l], conv_w[l], a_log[l], dt_bias[l], dn_norm_w[l], w_branch_attn[l], w_branch_delta[l],
              w_out[l], g_mix_pre[l], g_mix_post[l], g_mlp_pre[l], g_mlp_post[l], w_up[l], w_down[l])
        buf0 = jnp.zeros((b_p, CONV_W - 1, CONV_DIM), x_prompt.dtype)
        s0 = jnp.zeros((b_p, DN_HEADS, DN_DK, DN_DV), jnp.float32)
        y_p, kn, vn, sn, cn = decoder_layer(y_p, buf0, s0, moba_prompt, *wl)
        k_p.append(kn)
        v_p.append(vn)
        d_p.append(sn)
        c_p.append(cn)
        ck, cv = cache_k[l], cache_v[l]
        attend_s = lambda q, k, v, ck=ck, cv=cv: moba_sample(q, k, v, ck, cv, page_table)
        y_s, kn, vn, sn, cn = decoder_layer(y_s, state_conv[l], state_delta[l], attend_s, *wl)
        k_s.append(kn)
        v_s.append(vn)
        d_s.append(sn)
        c_s.append(cn)
    return (y_p, y_s, jnp.stack(k_p), jnp.stack(v_p), jnp.stack(d_p), jnp.stack(c_p),
            jnp.stack(k_s), jnp.stack(v_s), jnp.stack(d_s), jnp.stack(c_s))
```

```python
import functools
import math

import numpy as np
import jax
import jax.numpy as jnp
from jax import lax
from jax.experimental import pallas as pl
from jax.experimental.pallas import tpu as pltpu

F32 = jnp.float32
BF16 = jnp.bfloat16

D_MODEL = 2048
N_HEADS = 16
N_KV = 4
HD = 128
GROUP = N_HEADS // N_KV
MOBA_BLOCK = 256
MOBA_TOPK = 3
PAGE = 128
DN_HEADS = 16
DK = 128
DV = 128
CONV_W = 4
Q_A = N_HEADS * HD
KV_A = N_KV * HD
DN_QK = DN_HEADS * DK
DN_V = DN_HEADS * DV
CONV_DIM = 2 * DN_QK + DN_V
D_FF = 4 * D_MODEL
EPS = 1e-6

LANES = 128
SUBLANES = 8

C_Q = 0
C_K = C_Q + Q_A
C_V = C_K + KV_A
C_QKV = C_V + KV_A
C_Z = C_QKV + CONV_DIM
C_GA = C_Z + DN_V
C_GD = C_GA + D_MODEL
C_BA = C_GD + D_MODEL
P_DIM = C_BA + LANES
HEAD_COLS = 3 * DK

NEG = -1e30
SCALE = HD ** -0.5
EXP2_C = SCALE * math.log2(math.e)


def _params(sem, vmem_mib):
    return pltpu.CompilerParams(dimension_semantics=sem, vmem_limit_bytes=vmem_mib * 2**20)


def _rms(x, g):
    return x * lax.rsqrt(jnp.mean(x * x, axis=-1, keepdims=True) + EPS) * g


def _sigmoid(x):
    return 1.0 / (1.0 + jnp.exp(-x))


def _in_proj_kernel(x_ref, g_ref, w_ref, o_ref, h_ref):
    @pl.when(pl.program_id(1) == 0)
    def _():
        h_ref[...] = _rms(x_ref[...], g_ref[...]).astype(BF16)

    o_ref[...] = jnp.dot(h_ref[...], w_ref[...], preferred_element_type=F32)


def _in_proj(x2d, g_row, wp):
    n = x2d.shape[0]
    tm = 512
    tn = 1408
    assert n % tm == 0 and P_DIM % tn == 0
    return pl.pallas_call(
        _in_proj_kernel,
        grid=(n // tm, P_DIM // tn),
        in_specs=[
            pl.BlockSpec((tm, D_MODEL), lambda i, j: (i, 0)),
            pl.BlockSpec((1, D_MODEL), lambda i, j: (0, 0)),
            pl.BlockSpec((D_MODEL, tn), lambda i, j: (0, j)),
        ],
        out_specs=pl.BlockSpec((tm, tn), lambda i, j: (i, j)),
        out_shape=jax.ShapeDtypeStruct((n, P_DIM), F32),
        scratch_shapes=[pltpu.VMEM((tm, D_MODEL), BF16)],
        compiler_params=_params(("parallel", "arbitrary"), 40),
        name="in_proj",
    )(x2d, g_row, wp)


N_SLOPE_FEATS = 4
F_SEL = 0
F_HI = 16
F_LO = F_HI + N_SLOPE_FEATS
MAX_BLOCKS = 16


def _kprep_kernel(k_ref, v_ref, kaug_ref, vb_ref, kmean_ref, *, blocks_per_step):
    step = pl.program_id(1)
    lane = lax.broadcasted_iota(jnp.int32, (MOBA_BLOCK, LANES), 1)
    row = lax.broadcasted_iota(jnp.int32, (MOBA_BLOCK, LANES), 0).astype(F32)
    is_hi = jnp.where(lane >= F_HI, jnp.where(lane < F_LO, 1.0, 0.0), 0.0)
    is_lo = jnp.where(lane >= F_LO, jnp.where(lane < F_LO + N_SLOPE_FEATS, 1.0, 0.0), 0.0)
    for j in range(blocks_per_step):
        n = step * blocks_per_step + j
        rows = slice(j * MOBA_BLOCK, (j + 1) * MOBA_BLOCK)
        k = k_ref[rows, :]
        v = v_ref[rows, :]
        mean = jnp.mean(k, axis=0, keepdims=True)
        feat = (jnp.where(lane == n, 1.0, 0.0)
                + is_hi * (n * MOBA_BLOCK).astype(F32)
                + is_lo * row).astype(BF16)
        for c in range(N_KV):
            cols = slice(c * HD, (c + 1) * HD)
            kaug_ref[c, rows, 0:HD] = k[:, cols].astype(BF16)
            kaug_ref[c, rows, HD:2 * HD] = feat
            vb_ref[c, rows, :] = v[:, cols].astype(BF16)
            kmean_ref[c, j:j + 1, :] = mean[:, cols]


def _kprep(proj, batch, t_len):
    nblk = t_len // MOBA_BLOCK
    bps = min(nblk, 8)
    assert nblk % bps == 0 and nblk <= MAX_BLOCKS
    steps = nblk // bps
    rows = bps * MOBA_BLOCK
    return pl.pallas_call(
        functools.partial(_kprep_kernel, blocks_per_step=bps),
        grid=(batch, steps),
        in_specs=[
            pl.BlockSpec((rows, KV_A), lambda b, s: (b * steps + s, C_K // KV_A)),
            pl.BlockSpec((rows, KV_A), lambda b, s: (b * steps + s, C_V // KV_A)),
        ],
        out_specs=[
            pl.BlockSpec((None, N_KV, rows, 2 * HD), lambda b, s: (b, 0, s, 0)),
            pl.BlockSpec((None, N_KV, rows, HD), lambda b, s: (b, 0, s, 0)),
            pl.BlockSpec((None, N_KV, bps, HD), lambda b, s: (b, 0, s, 0)),
        ],
        out_shape=[
            jax.ShapeDtypeStruct((batch, N_KV, t_len, 2 * HD), BF16),
            jax.ShapeDtypeStruct((batch, N_KV, t_len, HD), BF16),
            jax.ShapeDtypeStruct((batch, N_KV, nblk, HD), F32),
        ],
        compiler_params=_params(("parallel", "arbitrary"), 40),
        name="kprep",
    )(proj, proj)


def _slope_pieces():
    slopes = np.asarray(2.0 ** (-8.0 * np.arange(1, N_HEADS + 1) / N_HEADS), np.float32).astype(np.float64)
    x = slopes / SCALE
    pieces = []
    for _ in range(N_SLOPE_FEATS):
        p = x.astype(np.float32).astype(jnp.bfloat16).astype(np.float64)
        pieces.append(p)
        x = x - p
    return np.stack(pieces, axis=1)


def _slope_feature_table():
    pieces = _slope_pieces()
    tab = np.zeros((N_KV, 2 * N_SLOPE_FEATS, GROUP * MOBA_BLOCK), np.float32)
    for c in range(N_KV):
        for g in range(GROUP):
            cols = slice(g * MOBA_BLOCK, (g + 1) * MOBA_BLOCK)
            for f in range(N_SLOPE_FEATS):
                tab[c, f, cols] = pieces[c * GROUP + g, f]
                tab[c, N_SLOPE_FEATS + f, cols] = pieces[c * GROUP + g, f]
    return jnp.asarray(tab)


def _select_bias(gate_t, own):
    nblk = gate_t.shape[0]
    blk = lax.broadcasted_iota(jnp.int32, gate_t.shape, 0)
    past = blk < own
    gm = jnp.where(past, gate_t, -jnp.inf)
    rank = jnp.zeros(gate_t.shape, F32)
    for m in range(nblk):
        row = gm[m:m + 1, :]
        tie = jnp.where(blk > m, 1.0, 0.0)
        rank = rank + jnp.where(row > gm, 1.0, jnp.where(row == gm, tie, 0.0))
    keep_past = jnp.where(past, jnp.where(rank < MOBA_TOPK - 0.5, 0.0, NEG), NEG)
    return jnp.where(blk == own, 0.0, keep_past)


def _moba_prompt_kernel(q_ref, kmean_ref, sf_ref, kaug_ref, vb_ref, o_ref,
                        feat_t, qaug, m_s, l_s, acc_s, *, nblk):
    i = pl.program_id(2)
    rows = GROUP * MOBA_BLOCK
    q = q_ref[...]
    qs = jnp.concatenate([q[:, g * HD:(g + 1) * HD] for g in range(GROUP)], axis=0).astype(BF16)
    qaug[:, 0:HD] = qs
    gate_t = lax.dot_general(kmean_ref[...].astype(BF16), qs, (((1,), (1,)), ((), ())),
                             preferred_element_type=F32)
    feat_t[...] = jnp.zeros(feat_t.shape, F32)
    feat_t[F_SEL:F_SEL + nblk, :] = _select_bias(gate_t, i)
    feat_t[F_HI:F_HI + 2 * N_SLOPE_FEATS, :] = sf_ref[...]
    qaug[:, HD:2 * HD] = feat_t[...].T.astype(BF16)

    def scores(n):
        start = pl.multiple_of(n * MOBA_BLOCK, MOBA_BLOCK)
        kn = kaug_ref[pl.ds(start, MOBA_BLOCK), :]
        vn = vb_ref[pl.ds(start, MOBA_BLOCK), :]
        s = lax.dot_general(qaug[...], kn, (((1,), (1,)), ((), ())), preferred_element_type=F32)
        return s, vn

    s, vn = scores(i)
    qi = lax.broadcasted_iota(jnp.int32, s.shape, 0) & (MOBA_BLOCK - 1)
    kj = lax.broadcasted_iota(jnp.int32, s.shape, 1)
    s = jnp.where(kj <= qi, s, NEG)
    m = jnp.max(s, axis=-1, keepdims=True)
    p = jnp.exp2((s - m) * EXP2_C)
    m_s[...] = m
    l_s[...] = jnp.sum(p, axis=-1, keepdims=True)
    acc_s[...] = jnp.dot(p.astype(BF16), vn, preferred_element_type=F32)

    def body(n, carry):
        s, vn = scores(n)
        m_old = m_s[...]
        m_new = jnp.maximum(m_old, jnp.max(s, axis=-1, keepdims=True))
        alpha = jnp.exp2((m_old - m_new) * EXP2_C)
        p = jnp.exp2((s - m_new) * EXP2_C)
        l_s[...] = alpha * l_s[...] + jnp.sum(p, axis=-1, keepdims=True)
        acc_s[...] = alpha * acc_s[...] + jnp.dot(p.astype(BF16), vn, preferred_element_type=F32)
        m_s[...] = m_new
        return carry

    lax.fori_loop(0, i, body, 0)
    out = acc_s[...] / l_s[...]
    for g in range(GROUP):
        o_ref[:, g * HD:(g + 1) * HD] = out[g * MOBA_BLOCK:(g + 1) * MOBA_BLOCK, :].astype(o_ref.dtype)


def _moba_prompt(proj, kaug, vb, kmean, batch, t_len):
    nq = t_len // MOBA_BLOCK
    nblk = nq
    rows = GROUP * MOBA_BLOCK
    qcols = GROUP * HD
    return pl.pallas_call(
        functools.partial(_moba_prompt_kernel, nblk=nblk),
        grid=(batch, N_KV, nq),
        in_specs=[
            pl.BlockSpec((MOBA_BLOCK, qcols), lambda b, c, i: (b * nq + i, C_Q // qcols + c)),
            pl.BlockSpec((None, None, nblk, HD), lambda b, c, i: (b, c, 0, 0)),
            pl.BlockSpec((None, 2 * N_SLOPE_FEATS, rows), lambda b, c, i: (c, 0, 0)),
            pl.BlockSpec((None, None, t_len, 2 * HD), lambda b, c, i: (b, c, 0, 0)),
            pl.BlockSpec((None, None, t_len, HD), lambda b, c, i: (b, c, 0, 0)),
        ],
        out_specs=pl.BlockSpec((MOBA_BLOCK, qcols), lambda b, c, i: (b * nq + i, c)),
        out_shape=jax.ShapeDtypeStruct((batch * t_len, Q_A), BF16),
        scratch_shapes=[
            pltpu.VMEM((LANES, rows), F32),
            pltpu.VMEM((rows, 2 * HD), BF16),
            pltpu.VMEM((rows, 1), F32),
            pltpu.VMEM((rows, 1), F32),
            pltpu.VMEM((rows, HD), F32),
        ],
        compiler_params=_params(("parallel", "parallel", "arbitrary"), 40),
        name="moba_prompt",
    )(proj, kmean, _slope_feature_table(), kaug, vb)


def _moba_sample_kernel(pt_ref, q_ref, kn_ref, vn_ref, slope_ref, *rest, n_pages, t_dec):
    del pt_ref
    k_pages = rest[:n_pages]
    v_pages = rest[n_pages:2 * n_pages]
    o_ref = rest[2 * n_pages]
    s_scr, kb_scr = rest[2 * n_pages + 1:]
    past = n_pages * PAGE
    nblk = past // MOBA_BLOCK
    pages_per_blk = MOBA_BLOCK // PAGE
    rows = N_HEADS * t_dec
    assert rows == LANES

    q = q_ref[...]
    q_rows = jnp.concatenate([q[:, h * HD:(h + 1) * HD] for h in range(N_HEADS)], axis=0)
    q_t = q_rows.T.astype(BF16)
    lane = lax.broadcasted_iota(jnp.int32, (HD, LANES), 1)
    rows_per_kv = GROUP * t_dec
    zero = jnp.zeros((HD, LANES), BF16)
    q_bd = [jnp.where(lane // rows_per_kv == c, q_t, zero) for c in range(N_KV)]

    def scores_t(k2d_rows):
        acc = None
        for c in range(N_KV):
            part = jnp.dot(k2d_rows(c), q_bd[c], preferred_element_type=F32)
            acc = part if acc is None else acc + part
        return acc

    for n in range(nblk):
        sums = [jnp.zeros((1, HD), F32) for _ in range(N_KV)]
        for pp in range(pages_per_blk):
            p = n * pages_per_blk + pp
            kc = []
            for c in range(N_KV):
                kf = k_pages[p][pl.ds(c, PAGE, stride=N_KV), :]
                sums[c] = sums[c] + jnp.sum(kf, axis=0, keepdims=True)
                kc.append(kf.astype(BF16))
            s_scr[p * PAGE:(p + 1) * PAGE, :] = scores_t(lambda c: kc[c])
        for c in range(N_KV):
            kb_scr[c, n:n + 1, :] = sums[c] * (1.0 / MOBA_BLOCK)
    gate_t = scores_t(lambda c: kb_scr[c].astype(BF16))
    sel = _select_bias(gate_t, nblk)

    slope = slope_ref[...]
    t_q = lax.broadcasted_iota(jnp.int32, (1, LANES), 1) % t_dec
    q_pos = (past + t_q).astype(F32)

    def logits(raw, k_pos):
        return raw * SCALE - slope * (q_pos - k_pos)

    kn = kn_ref[...]
    vn = vn_ref[...]
    knc = [kn[:, c * HD:(c + 1) * HD].astype(BF16) for c in range(N_KV)]
    t_k = lax.broadcasted_iota(jnp.int32, (t_dec, LANES), 0)
    s_own = logits(scores_t(lambda c: knc[c]), (past + t_k).astype(F32))
    s_own = jnp.where(t_k <= t_q, s_own, NEG)
    m = jnp.max(s_own, axis=0, keepdims=True)

    sub = lax.broadcasted_iota(jnp.int32, (PAGE, LANES), 0)
    for p in range(n_pages):
        n = p // pages_per_blk
        k_pos = (sub + p * PAGE).astype(F32)
        s = logits(s_scr[p * PAGE:(p + 1) * PAGE, :], k_pos) + sel[n:n + 1, :]
        s_scr[p * PAGE:(p + 1) * PAGE, :] = s
        m = jnp.maximum(m, jnp.max(s, axis=0, keepdims=True))

    p_own = jnp.exp(s_own - m)
    l = jnp.sum(p_own, axis=0, keepdims=True)
    row_kv = lax.broadcasted_iota(jnp.int32, (LANES, HD), 0) // rows_per_kv
    contract0 = (((0,), (0,)), ((), ()))
    p_own_b = p_own.astype(BF16)
    acc = jnp.zeros((LANES, HD), F32)
    for c in range(N_KV):
        part = lax.dot_general(p_own_b, vn[:, c * HD:(c + 1) * HD].astype(BF16), contract0,
                               preferred_element_type=F32)
        acc = acc + jnp.where(row_kv == c, part, 0.0)
    for p in range(n_pages):
        pr = jnp.exp(s_scr[p * PAGE:(p + 1) * PAGE, :] - m)
        l = l + jnp.sum(pr, axis=0, keepdims=True)
        pb = pr.astype(BF16)
        for c in range(N_KV):
            vf = v_pages[p][pl.ds(c, PAGE, stride=N_KV), :].astype(BF16)
            part = lax.dot_general(pb, vf, contract0, preferred_element_type=F32)
            acc = acc + jnp.where(row_kv == c, part, 0.0)
    l_col = jnp.broadcast_to(l, (LANES, LANES)).T
    out = acc / l_col
    for h in range(N_HEADS):
        o_ref[:, h * HD:(h + 1) * HD] = out[h * t_dec:(h + 1) * t_dec, :].astype(o_ref.dtype)


def _moba_sample(proj, cache_k2d, cache_v2d, page_table, n_seq, t_dec):
    n_pages = page_table.shape[1]
    past = n_pages * PAGE
    nblk = past // MOBA_BLOCK
    page_rows = PAGE * N_KV
    slopes = np.asarray(2.0 ** (-8.0 * np.arange(1, N_HEADS + 1) / N_HEADS), np.float32)
    slope_row = jnp.asarray(np.repeat(slopes, t_dec)[None, :])

    def page_spec(p):
        return pl.BlockSpec((page_rows, HD), lambda b, pt, p=p: (pt[b, p], 0))

    grid_spec = pltpu.PrefetchScalarGridSpec(
        num_scalar_prefetch=1,
        grid=(n_seq,),
        in_specs=[
            pl.BlockSpec((t_dec, Q_A), lambda b, pt: (b, C_Q // Q_A)),
            pl.BlockSpec((t_dec, KV_A), lambda b, pt: (b, C_K // KV_A)),
            pl.BlockSpec((t_dec, KV_A), lambda b, pt: (b, C_V // KV_A)),
            pl.BlockSpec((1, LANES), lambda b, pt: (0, 0)),
        ] + [page_spec(p) for p in range(n_pages)] * 2,
        out_specs=pl.BlockSpec((t_dec, Q_A), lambda b, pt: (b, 0)),
        scratch_shapes=[
            pltpu.VMEM((past, LANES), F32),
            pltpu.VMEM((N_KV, nblk, HD), F32),
        ],
    )
    return pl.pallas_call(
        functools.partial(_moba_sample_kernel, n_pages=n_pages, t_dec=t_dec),
        grid_spec=grid_spec,
        out_shape=jax.ShapeDtypeStruct((n_seq * t_dec, Q_A), BF16),
        compiler_params=_params(("arbitrary",), 40),
        name="moba_sample",
    )(page_table, proj, proj, proj, slope_row, *([cache_k2d] * n_pages), *([cache_v2d] * n_pages))


DN_TILE = 128
PK_W, PK_QD, PK_KD, PK_QK = 0, 1, 2, 3
INV_BASE_LEVELS = 3


def _conv_silu(x, w, shifted):
    y = x * w[CONV_W - 1:CONV_W, :]
    for d in range(1, CONV_W):
        y = y + shifted(d) * w[CONV_W - 1 - d:CONV_W - d, :]
    return y * _sigmoid(y)


def _delta_prep_kernel(x_ref, halo_ref, ba_ref, cw_ref, al_ref, dt_ref, pk_ref, u_ref, ge_ref,
                       *, seg, tiles_per_step, tiles_per_seq):
    step = pl.program_id(0)
    h = pl.program_id(1)
    levels = int(math.log2(seg))
    ri = lax.broadcasted_iota(jnp.int32, (DN_TILE, DN_TILE), 0)
    ci = lax.broadcasted_iota(jnp.int32, (DN_TILE, DN_TILE), 1)
    same = jnp.where((ri >> levels) == (ci >> levels), 1.0, 0.0)
    incl = jnp.where(ci <= ri, same, 0.0)
    strict = jnp.where(ci < ri, same, 0.0)
    eye = jnp.where(ci == ri, 1.0, 0.0)
    base_levels = min(INV_BASE_LEVELS, levels)
    merge_masks = [jnp.where((ri >> lv) == (ci >> lv), 1.0, 0.0) for lv in range(base_levels, levels + 1)]
    idx_b = jnp.full((DN_TILE, LANES), h, jnp.int32)
    idx_g = idx_b + DN_HEADS
    cw = cw_ref[...]
    nt = (((1,), (1,)), ((), ()))
    hi = lax.Precision.HIGHEST

    for t in range(tiles_per_step):
        rows = slice(t * DN_TILE, (t + 1) * DN_TILE)
        x = x_ref[rows, :]
        if seg == DN_TILE:
            if t == 0:
                tile = step * tiles_per_step
                first = ((tile % tiles_per_seq) == 0).astype(F32)
                halo = halo_ref[...] * (1.0 - first)
            else:
                halo = x_ref[t * DN_TILE - SUBLANES:t * DN_TILE, :]
            xe = jnp.concatenate([halo, x], axis=0)

            def shifted(d):
                return pltpu.roll(xe, d, 0)[SUBLANES:, :]
        else:
            bx = halo_ref[rows, :]
            t_in = lax.broadcasted_iota(jnp.int32, x.shape, 0) & (seg - 1)

            def shifted(d):
                xs = pltpu.roll(x, d, 0)
                bs = pltpu.roll(bx, (d - seg) % DN_TILE, 0)
                return jnp.where(t_in >= d, xs, bs)
        y = _conv_silu(x, cw, shifted)
        qc, kc, v = y[:, 0:DK], y[:, DK:2 * DK], y[:, 2 * DK:3 * DK]
        q = qc * lax.rsqrt(jnp.sum(qc * qc, axis=-1, keepdims=True) + EPS) * (DK ** -0.5)
        k = kc * lax.rsqrt(jnp.sum(kc * kc, axis=-1, keepdims=True) + EPS)

        ba = ba_ref[rows, :]
        beta_all = _sigmoid(ba)
        xg = ba + dt_ref[...]
        softplus = jnp.maximum(xg, 0.0) + jnp.log(1.0 + jnp.exp(-jnp.abs(xg)))
        g_all = -jnp.exp(al_ref[...]) * softplus
        beta = jnp.take_along_axis(beta_all, idx_b, axis=1, mode="promise_in_bounds")
        g = jnp.take_along_axis(g_all, idx_g, axis=1, mode="promise_in_bounds")

        cum = jnp.dot(incl, g, precision=hi, preferred_element_type=F32)
        tot = jnp.dot(same, g, precision=hi, preferred_element_type=F32)
        diff = cum - cum.T
        decay = jnp.exp(jnp.where(incl > 0.0, diff, NEG))
        kb = k.astype(BF16)
        kk = lax.dot_general(kb, kb, nt, preferred_element_type=F32)
        qk = lax.dot_general(q.astype(BF16), kb, nt, preferred_element_type=F32)
        a = strict * (beta * decay * kk)
        a0 = a * merge_masks[0]
        t_inv = eye - a0
        pw = a0
        for _ in range(base_levels - 1):
            pwb = pw.astype(BF16)
            pw = jnp.dot(pwb, pwb, preferred_element_type=F32)
            t_inv = t_inv + jnp.dot(t_inv.astype(BF16), pw.astype(BF16), preferred_element_type=F32)
        for lv in range(base_levels, levels):
            below = (a * (merge_masks[lv - base_levels + 1] - merge_masks[lv - base_levels])).astype(BF16)
            tb = t_inv.astype(BF16)
            tl = jnp.dot(tb, below, preferred_element_type=F32)
            t_inv = t_inv - jnp.dot(tl.astype(BF16), tb, preferred_element_type=F32)
        gam = jnp.exp(cum)
        tb = t_inv.astype(BF16)
        w = jnp.dot(tb, (beta * gam * k).astype(BF16), preferred_element_type=F32)
        u = jnp.dot(tb, (beta * v).astype(BF16), preferred_element_type=F32)
        pk_ref[rows, PK_W * DK:(PK_W + 1) * DK] = w.astype(BF16)
        pk_ref[rows, PK_QD * DK:(PK_QD + 1) * DK] = (q * gam).astype(BF16)
        pk_ref[rows, PK_KD * DK:(PK_KD + 1) * DK] = (k * jnp.exp(tot - cum)).astype(BF16)
        pk_ref[rows, PK_QK * DK:(PK_QK + 1) * DK] = (qk * decay).astype(BF16)
        u_ref[rows, :] = u
        g_end = jnp.exp(tot)
        segs = DN_TILE // seg
        for j in range(segs):
            ge_ref[t * segs + j] = g_end[j * seg:j * seg + 1, :]


def _delta_prep(proj, halo_src, conv_w_perm, al_row, dt_row, *, seg, t_len):
    n = proj.shape[0]
    tiles_per_step = 2
    tr = tiles_per_step * DN_TILE
    assert n % tr == 0
    steps = n // tr
    qkv_blk = C_QKV // HEAD_COLS
    segs = tr // seg
    if seg == DN_TILE:
        assert t_len % tr == 0
        sub_per_step = tr // SUBLANES
        halo_spec = pl.BlockSpec(
            (SUBLANES, HEAD_COLS), lambda s, h: (jnp.maximum(s * sub_per_step - 1, 0), qkv_blk + h))
    else:
        halo_spec = pl.BlockSpec((tr, HEAD_COLS), lambda s, h: (s, h))
    return pl.pallas_call(
        functools.partial(_delta_prep_kernel, seg=seg, tiles_per_step=tiles_per_step,
                          tiles_per_seq=max(t_len // DN_TILE, 1)),
        grid=(steps, DN_HEADS),
        in_specs=[
            pl.BlockSpec((tr, HEAD_COLS), lambda s, h: (s, qkv_blk + h)),
            halo_spec,
            pl.BlockSpec((tr, LANES), lambda s, h: (s, C_BA // LANES)),
            pl.BlockSpec((CONV_W, HEAD_COLS), lambda s, h: (0, h)),
            pl.BlockSpec((1, LANES), lambda s, h: (0, 0)),
            pl.BlockSpec((1, LANES), lambda s, h: (0, 0)),
        ],
        out_specs=[
            pl.BlockSpec((None, tr, 4 * DK), lambda s, h: (h, s, 0)),
            pl.BlockSpec((None, tr, DV), lambda s, h: (h, s, 0)),
            pl.BlockSpec((None, segs, 1, LANES), lambda s, h: (h, s, 0, 0)),
        ],
        out_shape=[
            jax.ShapeDtypeStruct((DN_HEADS, n, 4 * DK), BF16),
            jax.ShapeDtypeStruct((DN_HEADS, n, DV), F32),
            jax.ShapeDtypeStruct((DN_HEADS, n // seg, 1, LANES), F32),
        ],
        compiler_params=_params(("parallel", "arbitrary"), 32),
        name="delta_prep",
    )(proj, halo_src, proj, conv_w_perm, al_row, dt_row)


def _out_gate(o, z, dnw):
    return _rms(o, dnw) * (z * _sigmoid(z))


def _delta_scan_prompt_kernel(pk_ref, u_ref, ge_ref, z_ref, dnw_ref, o_ref, sout_ref, s_scr, *, hg):
    c = pl.program_id(2)

    @pl.when(c == 0)
    def _():
        s_scr[...] = jnp.zeros(s_scr.shape, F32)

    contract0 = (((0,), (0,)), ((), ()))
    for j in range(hg):
        s = s_scr[j]
        pk = pk_ref[j]
        wq = jnp.concatenate([pk[:, PK_W * DK:(PK_W + 1) * DK], pk[:, PK_QD * DK:(PK_QD + 1) * DK]], axis=0)
        res = jnp.dot(wq, s.astype(BF16), preferred_element_type=F32)
        ub = (u_ref[j] - res[:DN_TILE]).astype(BF16)
        o = res[DN_TILE:] + jnp.dot(pk[:, PK_QK * DK:(PK_QK + 1) * DK], ub, preferred_element_type=F32)
        s_new = ge_ref[j, 0] * s + lax.dot_general(pk[:, PK_KD * DK:(PK_KD + 1) * DK], ub, contract0,
                                                   preferred_element_type=F32)
        s_scr[j] = s_new
        cols = slice(j * DV, (j + 1) * DV)
        o_ref[:, cols] = _out_gate(o, z_ref[:, cols], dnw_ref[...]).astype(o_ref.dtype)

    @pl.when(c == pl.num_programs(2) - 1)
    def _():
        sout_ref[...] = s_scr[...]


def _delta_scan_prompt(pk, u, ge, proj, dnw_row, batch, t_len):
    hg = 4
    nc = t_len // DN_TILE
    zblk = C_Z // (hg * DV)
    return pl.pallas_call(
        functools.partial(_delta_scan_prompt_kernel, hg=hg),
        grid=(batch, DN_HEADS // hg, nc),
        in_specs=[
            pl.BlockSpec((hg, DN_TILE, 4 * DK), lambda b, g, c: (g, b * nc + c, 0)),
            pl.BlockSpec((hg, DN_TILE, DV), lambda b, g, c: (g, b * nc + c, 0)),
            pl.BlockSpec((hg, 1, 1, LANES), lambda b, g, c: (g, b * nc + c, 0, 0)),
            pl.BlockSpec((DN_TILE, hg * DV), lambda b, g, c: (b * nc + c, zblk + g)),
            pl.BlockSpec((1, DV), lambda b, g, c: (0, 0)),
        ],
        out_specs=[
            pl.BlockSpec((DN_TILE, hg * DV), lambda b, g, c: (b * nc + c, g)),
            pl.BlockSpec((None, hg, DK, DV), lambda b, g, c: (b, g, 0, 0)),
        ],
        out_shape=[
            jax.ShapeDtypeStruct((batch * t_len, DN_V), BF16),
            jax.ShapeDtypeStruct((batch, DN_HEADS, DK, DV), F32),
        ],
        scratch_shapes=[pltpu.VMEM((hg, DK, DV), F32)],
        compiler_params=_params(("parallel", "parallel", "arbitrary"), 32),
        name="delta_scan_prompt",
    )(pk, u, ge, proj, dnw_row)


def _delta_scan_sample_kernel(pk_ref, u_ref, ge_ref, z_ref, dnw_ref, s0_ref, o_ref, sout_ref,
                              ub_scr, oq_scr, pk32_scr, *, hg, t_dec):
    n_seq = DN_TILE // t_dec
    contract0 = (((0,), (0,)), ((), ()))
    for j in range(hg):
        pk32_scr[...] = pk_ref[j].astype(F32)

        def per_seq(i, carry):
            r0 = pl.multiple_of(i * t_dec, t_dec)
            s0 = s0_ref[i, j]
            wq = jnp.concatenate([pk32_scr[pl.ds(r0, t_dec), PK_W * DK:(PK_W + 1) * DK],
                                  pk32_scr[pl.ds(r0, t_dec), PK_QD * DK:(PK_QD + 1) * DK]], axis=0)
            res = jnp.dot(wq.astype(BF16), s0.astype(BF16), preferred_element_type=F32)
            uu = u_ref[j, pl.ds(r0, t_dec), :] - res[:t_dec]
            ub_scr[pl.ds(r0, t_dec), :] = uu
            oq_scr[pl.ds(r0, t_dec), :] = res[t_dec:]
            kd = pk32_scr[pl.ds(r0, t_dec), PK_KD * DK:(PK_KD + 1) * DK]
            upd = lax.dot_general(kd.astype(BF16), uu.astype(BF16), contract0, preferred_element_type=F32)
            sout_ref[i, j] = ge_ref[j, i] * s0 + upd
            return carry

        lax.fori_loop(0, n_seq, per_seq, 0)
        qk = pk_ref[j, :, PK_QK * DK:(PK_QK + 1) * DK]
        o = oq_scr[...] + jnp.dot(qk, ub_scr[...].astype(BF16), preferred_element_type=F32)
        cols = slice(j * DV, (j + 1) * DV)
        o_ref[:, cols] = _out_gate(o, z_ref[:, cols], dnw_ref[...]).astype(o_ref.dtype)


def _delta_scan_sample(pk, u, ge, proj, dnw_row, s0, n_seq, t_dec):
    hg = 4
    seq_per_tile = DN_TILE // t_dec
    tiles = n_seq // seq_per_tile
    zblk = C_Z // (hg * DV)
    return pl.pallas_call(
        functools.partial(_delta_scan_sample_kernel, hg=hg, t_dec=t_dec),
        grid=(tiles, DN_HEADS // hg),
        in_specs=[
            pl.BlockSpec((hg, DN_TILE, 4 * DK), lambda r, g: (g, r, 0)),
            pl.BlockSpec((hg, DN_TILE, DV), lambda r, g: (g, r, 0)),
            pl.BlockSpec((hg, seq_per_tile, 1, LANES), lambda r, g: (g, r, 0, 0)),
            pl.BlockSpec((DN_TILE, hg * DV), lambda r, g: (r, zblk + g)),
            pl.BlockSpec((1, DV), lambda r, g: (0, 0)),
            pl.BlockSpec((seq_per_tile, hg, DK, DV), lambda r, g: (r, g, 0, 0)),
        ],
        out_specs=[
            pl.BlockSpec((DN_TILE, hg * DV), lambda r, g: (r, g)),
            pl.BlockSpec((seq_per_tile, hg, DK, DV), lambda r, g: (r, g, 0, 0)),
        ],
        out_shape=[
            jax.ShapeDtypeStruct((n_seq * t_dec, DN_V), BF16),
            jax.ShapeDtypeStruct((n_seq, DN_HEADS, DK, DV), F32),
        ],
        scratch_shapes=[pltpu.VMEM((DN_TILE, DV), F32), pltpu.VMEM((DN_TILE, DV), F32),
                        pltpu.VMEM((DN_TILE, 4 * DK), F32)],
        compiler_params=_params(("parallel", "parallel"), 48),
        name="delta_scan_sample",
    )(pk, u, ge, proj, dnw_row, s0)


def _merge_kernel(oa_ref, od_ref, wa_ref, wd_ref, ga_ref, gd_ref, o_ref):
    ya = jnp.dot(oa_ref[...], wa_ref[...], preferred_element_type=F32)
    yd = jnp.dot(od_ref[...], wd_ref[...], preferred_element_type=F32)
    o_ref[...] = (_sigmoid(ga_ref[...]) * ya + _sigmoid(gd_ref[...]) * yd).astype(o_ref.dtype)


def _merge(oa, od, wa, wd, proj):
    n = oa.shape[0]
    tm, tn = 512, 512
    return pl.pallas_call(
        _merge_kernel,
        grid=(n // tm, D_MODEL // tn),
        in_specs=[
            pl.BlockSpec((tm, Q_A), lambda i, j: (i, 0)),
            pl.BlockSpec((tm, DN_V), lambda i, j: (i, 0)),
            pl.BlockSpec((Q_A, tn), lambda i, j: (0, j)),
            pl.BlockSpec((DN_V, tn), lambda i, j: (0, j)),
            pl.BlockSpec((tm, tn), lambda i, j: (i, C_GA // tn + j)),
            pl.BlockSpec((tm, tn), lambda i, j: (i, C_GD // tn + j)),
        ],
        out_specs=pl.BlockSpec((tm, tn), lambda i, j: (i, j)),
        out_shape=jax.ShapeDtypeStruct((n, D_MODEL), BF16),
        compiler_params=_params(("parallel", "arbitrary"), 40),
        name="merge",
    )(oa, od, wa, wd, proj, proj)


def _out_norm_kernel(m_ref, w_ref, x_ref, g_ref, o_ref):
    t = jnp.dot(m_ref[...], w_ref[...], preferred_element_type=F32)
    o_ref[...] = x_ref[...] + _rms(t, g_ref[...])


def _out_norm(mixed, w_out, x2d, g_row):
    n = mixed.shape[0]
    tm = 512
    return pl.pallas_call(
        _out_norm_kernel,
        grid=(n // tm,),
        in_specs=[
            pl.BlockSpec((tm, D_MODEL), lambda i: (i, 0)),
            pl.BlockSpec((D_MODEL, D_MODEL), lambda i: (0, 0)),
            pl.BlockSpec((tm, D_MODEL), lambda i: (i, 0)),
            pl.BlockSpec((1, D_MODEL), lambda i: (0, 0)),
        ],
        out_specs=pl.BlockSpec((tm, D_MODEL), lambda i: (i, 0)),
        out_shape=jax.ShapeDtypeStruct((n, D_MODEL), F32),
        compiler_params=_params(("parallel",), 48),
        name="out_norm",
    )(mixed, w_out, x2d, g_row)


def _mlp_kernel(x_ref, gpre_ref, wu_ref, wd_ref, gpost_ref, o_ref, h_ref, acc_ref):
    f = pl.program_id(1)

    @pl.when(f == 0)
    def _():
        h_ref[...] = _rms(x_ref[...], gpre_ref[...]).astype(BF16)
        acc_ref[...] = jnp.zeros(acc_ref.shape, F32)

    up = jnp.dot(h_ref[...], wu_ref[...], preferred_element_type=F32)
    act = jnp.square(jnp.maximum(up, 0.0)).astype(BF16)
    acc_ref[...] += jnp.dot(act, wd_ref[...], preferred_element_type=F32)

    @pl.when(f == pl.num_programs(1) - 1)
    def _():
        o_ref[...] = x_ref[...] + _rms(acc_ref[...], gpost_ref[...])


def _mlp(x2d, gpre, w_up, w_down, gpost):
    n = x2d.shape[0]
    tm, tf = 512, 1024
    return pl.pallas_call(
        _mlp_kernel,
        grid=(n // tm, D_FF // tf),
        in_specs=[
            pl.BlockSpec((tm, D_MODEL), lambda i, f: (i, 0)),
            pl.BlockSpec((1, D_MODEL), lambda i, f: (0, 0)),
            pl.BlockSpec((D_MODEL, tf), lambda i, f: (0, f)),
            pl.BlockSpec((tf, D_MODEL), lambda i, f: (f, 0)),
            pl.BlockSpec((1, D_MODEL), lambda i, f: (0, 0)),
        ],
        out_specs=pl.BlockSpec((tm, D_MODEL), lambda i, f: (i, 0)),
        out_shape=jax.ShapeDtypeStruct((n, D_MODEL), F32),
        scratch_shapes=[pltpu.VMEM((tm, D_MODEL), BF16), pltpu.VMEM((tm, D_MODEL), F32)],
        compiler_params=_params(("parallel", "arbitrary"), 48),
        name="mlp",
    )(x2d, gpre, w_up, w_down, gpost)


def _qkv_perm():
    idx = np.empty((CONV_DIM,), np.int32)
    for h in range(DN_HEADS):
        for p in range(3):
            idx[h * HEAD_COLS + p * DK:h * HEAD_COLS + (p + 1) * DK] = p * DN_QK + h * DK + np.arange(DK)
    return idx


def _pack_w_in(w_in):
    old_qkv = Q_A + 2 * KV_A
    old_z = old_qkv + CONV_DIM
    old_b = old_z + DN_V
    old_ga = old_b + 2 * DN_HEADS
    perm = _qkv_perm()
    parts = [
        w_in[:, :old_qkv],
        jnp.take(w_in[:, old_qkv:old_z], perm, axis=1),
        w_in[:, old_z:old_b],
        w_in[:, old_ga:],
        w_in[:, old_b:old_ga],
        jnp.zeros((D_MODEL, LANES - 2 * DN_HEADS), w_in.dtype),
    ]
    return jnp.concatenate(parts, axis=1).astype(BF16)


def _lane_row(vals, offset):
    return jnp.zeros((1, LANES), F32).at[0, offset:offset + vals.shape[0]].set(vals.astype(F32))


def _layer(x2d, attend, delta, w):
    proj = _in_proj(x2d, w["g_mix_pre"], w["w_in"])
    o_a = attend(proj)
    o_d, s_final = delta(proj)
    mixed = _merge(o_a, o_d, w["w_ba"], w["w_bd"], proj)
    x1 = _out_norm(mixed, w["w_out"], x2d, w["g_mix_post"])
    y = _mlp(x1, w["g_mlp_pre"], w["w_up"], w["w_down"], w["g_mlp_post"])
    return y, proj, s_final


def kernel(x_prompt, x_sample, cache_k, cache_v, state_delta, state_conv, page_table, w_in, conv_w, a_log,
           dt_bias, dn_norm_w, w_branch_attn, w_branch_delta, w_out, g_mix_pre, g_mix_post, g_mlp_pre,
           g_mlp_post, w_up, w_down):
    depth = w_in.shape[0]
    assert depth == 1
    b_p, t_p, _ = x_prompt.shape
    b_s, t_s, _ = x_sample.shape
    perm = _qkv_perm()
    inv_perm = np.argsort(perm)
    l = 0
    w = {
        "w_in": _pack_w_in(w_in[l]),
        "g_mix_pre": g_mix_pre[l][None, :],
        "g_mix_post": g_mix_post[l][None, :],
        "g_mlp_pre": g_mlp_pre[l][None, :],
        "g_mlp_post": g_mlp_post[l][None, :],
        "w_ba": w_branch_attn[l].astype(BF16),
        "w_bd": w_branch_delta[l].astype(BF16),
        "w_out": w_out[l].astype(BF16),
        "w_up": w_up[l].astype(BF16),
        "w_down": w_down[l].astype(BF16),
    }
    cw = jnp.take(conv_w[l], perm, axis=1)
    al_row = _lane_row(a_log[l], DN_HEADS)
    dt_row = _lane_row(dt_bias[l], DN_HEADS)
    dnw_row = dn_norm_w[l][None, :]

    def attend_p(proj):
        kaug, vb, kmean = _kprep(proj, b_p, t_p)
        return _moba_prompt(proj, kaug, vb, kmean, b_p, t_p)

    def delta_p(proj):
        pk, u, ge = _delta_prep(proj, proj, cw, al_row, dt_row, seg=DN_TILE, t_len=t_p)
        return _delta_scan_prompt(pk, u, ge, proj, dnw_row, b_p, t_p)

    y_p, proj_p, d_p = _layer(x_prompt.reshape(b_p * t_p, D_MODEL), attend_p, delta_p, w)

    n_pool = cache_k.shape[1]
    ck2d = cache_k[l].reshape(n_pool * PAGE * N_KV, HD)
    cv2d = cache_v[l].reshape(n_pool * PAGE * N_KV, HD)
    conv_rows = jnp.take(state_conv[l], perm, axis=2)
    halo_s = jnp.pad(conv_rows, ((0, 0), (t_s - (CONV_W - 1), 0), (0, 0))).reshape(b_s * t_s, CONV_DIM)

    def attend_s(proj):
        return _moba_sample(proj, ck2d, cv2d, page_table, b_s, t_s)

    def delta_s(proj):
        pk, u, ge = _delta_prep(proj, halo_s, cw, al_row, dt_row, seg=t_s, t_len=t_s)
        return _delta_scan_sample(pk, u, ge, proj, dnw_row, state_delta[l], b_s, t_s)

    y_s, proj_s, d_s = _layer(x_sample.reshape(b_s * t_s, D_MODEL), attend_s, delta_s, w)

    def kv_out(proj, b, t):
        k = proj[:, C_K:C_K + KV_A].reshape(1, b, t, N_KV, HD)
        v = proj[:, C_V:C_V + KV_A].reshape(1, b, t, N_KV, HD)
        return k, v

    def conv_out(proj, b, t):
        raw = proj[:, C_QKV:C_QKV + CONV_DIM].reshape(b, t, CONV_DIM)[:, t - (CONV_W - 1):, :]
        return jnp.take(raw, inv_perm, axis=2)[None]

    k_p, v_p = kv_out(proj_p, b_p, t_p)
    k_s, v_s = kv_out(proj_s, b_s, t_s)
    return (y_p.reshape(b_p, t_p, D_MODEL), y_s.reshape(b_s, t_s, D_MODEL), k_p, v_p, d_p[None],
            conv_out(proj_p, b_p, t_p), k_s, v_s, d_s[None], conv_out(proj_s, b_s, t_s))
```

```python
import functools
import math

import numpy as np
import jax
import jax.numpy as jnp
from jax import lax
from jax.experimental import pallas as pl
from jax.experimental.pallas import tpu as pltpu

F32 = jnp.float32
BF16 = jnp.bfloat16

D_MODEL = 2048
N_HEADS = 16
N_KV = 4
HD = 128
GROUP = N_HEADS // N_KV
MOBA_BLOCK = 256
MOBA_TOPK = 3
PAGE = 128
DN_HEADS = 16
DK = 128
DV = 128
CONV_W = 4
Q_A = N_HEADS * HD
KV_A = N_KV * HD
DN_QK = DN_HEADS * DK
DN_V = DN_HEADS * DV
CONV_DIM = 2 * DN_QK + DN_V
D_FF = 4 * D_MODEL
EPS = 1e-6

LANES = 128
SUBLANES = 8

C_Q = 0
C_K = C_Q + Q_A
C_V = C_K + KV_A
C_QKV = C_V + KV_A
C_Z = C_QKV + CONV_DIM
C_GA = C_Z + DN_V
C_GD = C_GA + D_MODEL
C_BA = C_GD + D_MODEL
P_DIM = C_BA + LANES
HEAD_COLS = 3 * DK

NEG = -1e30
SCALE = HD ** -0.5
EXP2_C = SCALE * math.log2(math.e)


def _params(sem, vmem_mib):
    return pltpu.CompilerParams(dimension_semantics=sem, vmem_limit_bytes=vmem_mib * 2**20)


def _rms(x, g):
    return x * lax.rsqrt(jnp.mean(x * x, axis=-1, keepdims=True) + EPS) * g


def _sigmoid(x):
    return 1.0 / (1.0 + jnp.exp(-x))


def _in_proj_kernel(x_ref, g_ref, w_ref, o_ref, h_ref):
    @pl.when(pl.program_id(1) == 0)
    def _():
        h_ref[...] = _rms(x_ref[...], g_ref[...]).astype(BF16)

    o_ref[...] = jnp.dot(h_ref[...], w_ref[...], preferred_element_type=F32)


def _in_proj(x2d, g_row, wp):
    n = x2d.shape[0]
    tm = 512
    tn = 1408
    assert n % tm == 0 and P_DIM % tn == 0
    return pl.pallas_call(
        _in_proj_kernel,
        grid=(n // tm, P_DIM // tn),
        in_specs=[
            pl.BlockSpec((tm, D_MODEL), lambda i, j: (i, 0)),
            pl.BlockSpec((1, D_MODEL), lambda i, j: (0, 0)),
            pl.BlockSpec((D_MODEL, tn), lambda i, j: (0, j)),
        ],
        out_specs=pl.BlockSpec((tm, tn), lambda i, j: (i, j)),
        out_shape=jax.ShapeDtypeStruct((n, P_DIM), F32),
        scratch_shapes=[pltpu.VMEM((tm, D_MODEL), BF16)],
        compiler_params=_params(("parallel", "arbitrary"), 40),
        name="in_proj",
    )(x2d, g_row, wp)


N_SLOPE_FEATS = 4
F_SEL = 0
F_HI = 16
F_LO = F_HI + N_SLOPE_FEATS
MAX_BLOCKS = 16


def _kprep_kernel(k_ref, v_ref, kaug_ref, vb_ref, kmean_ref, *, blocks_per_step):
    step = pl.program_id(1)
    lane = lax.broadcasted_iota(jnp.int32, (MOBA_BLOCK, LANES), 1)
    row = lax.broadcasted_iota(jnp.int32, (MOBA_BLOCK, LANES), 0).astype(F32)
    is_hi = jnp.where(lane >= F_HI, jnp.where(lane < F_LO, 1.0, 0.0), 0.0)
    is_lo = jnp.where(lane >= F_LO, jnp.where(lane < F_LO + N_SLOPE_FEATS, 1.0, 0.0), 0.0)
    ones_col = jnp.where(lane == 0, 1.0, 0.0).astype(BF16)
    for j in range(blocks_per_step):
        n = step * blocks_per_step + j
        rows = slice(j * MOBA_BLOCK, (j + 1) * MOBA_BLOCK)
        k = k_ref[rows, :]
        v = v_ref[rows, :]
        mean = jnp.mean(k, axis=0, keepdims=True)
        feat = (jnp.where(lane == n, 1.0, 0.0)
                + is_hi * (n * MOBA_BLOCK).astype(F32)
                + is_lo * row).astype(BF16)
        for c in range(N_KV):
            cols = slice(c * HD, (c + 1) * HD)
            kaug_ref[c, rows, 0:HD] = k[:, cols].astype(BF16)
            kaug_ref[c, rows, HD:2 * HD] = feat
            vb_ref[c, rows, 0:HD] = v[:, cols].astype(BF16)
            vb_ref[c, rows, HD:2 * HD] = ones_col
            kmean_ref[c, j:j + 1, :] = mean[:, cols]


def _kprep(proj, batch, t_len):
    nblk = t_len // MOBA_BLOCK
    bps = min(nblk, 8)
    assert nblk % bps == 0 and nblk <= MAX_BLOCKS
    steps = nblk // bps
    rows = bps * MOBA_BLOCK
    return pl.pallas_call(
        functools.partial(_kprep_kernel, blocks_per_step=bps),
        grid=(batch, steps),
        in_specs=[
            pl.BlockSpec((rows, KV_A), lambda b, s: (b * steps + s, C_K // KV_A)),
            pl.BlockSpec((rows, KV_A), lambda b, s: (b * steps + s, C_V // KV_A)),
        ],
        out_specs=[
            pl.BlockSpec((None, N_KV, rows, 2 * HD), lambda b, s: (b, 0, s, 0)),
            pl.BlockSpec((None, N_KV, rows, 2 * HD), lambda b, s: (b, 0, s, 0)),
            pl.BlockSpec((None, N_KV, bps, HD), lambda b, s: (b, 0, s, 0)),
        ],
        out_shape=[
            jax.ShapeDtypeStruct((batch, N_KV, t_len, 2 * HD), BF16),
            jax.ShapeDtypeStruct((batch, N_KV, t_len, 2 * HD), BF16),
            jax.ShapeDtypeStruct((batch, N_KV, nblk, HD), F32),
        ],
        compiler_params=_params(("parallel", "arbitrary"), 40),
        name="kprep",
    )(proj, proj)


def _slope_pieces():
    slopes = np.asarray(2.0 ** (-8.0 * np.arange(1, N_HEADS + 1) / N_HEADS), np.float32).astype(np.float64)
    x = slopes / SCALE
    pieces = []
    for _ in range(N_SLOPE_FEATS):
        p = x.astype(np.float32).astype(jnp.bfloat16).astype(np.float64)
        pieces.append(p)
        x = x - p
    return np.stack(pieces, axis=1)


def _slope_feature_table():
    pieces = _slope_pieces()
    tab = np.zeros((N_KV, 2 * N_SLOPE_FEATS, GROUP * MOBA_BLOCK), np.float32)
    for c in range(N_KV):
        for g in range(GROUP):
            cols = slice(g * MOBA_BLOCK, (g + 1) * MOBA_BLOCK)
            for f in range(N_SLOPE_FEATS):
                tab[c, f, cols] = pieces[c * GROUP + g, f]
                tab[c, N_SLOPE_FEATS + f, cols] = pieces[c * GROUP + g, f]
    return jnp.asarray(tab)


def _select_bias(gate_t, own):
    nblk = gate_t.shape[0]
    blk = lax.broadcasted_iota(jnp.int32, gate_t.shape, 0)
    past = blk < own
    gm = jnp.where(past, gate_t, -jnp.inf)
    rank = jnp.zeros(gate_t.shape, F32)
    for m in range(nblk):
        row = gm[m:m + 1, :]
        tie = jnp.where(blk > m, 1.0, 0.0)
        rank = rank + jnp.where(row > gm, 1.0, jnp.where(row == gm, tie, 0.0))
    keep_past = jnp.where(past, jnp.where(rank < MOBA_TOPK - 0.5, 0.0, NEG), NEG)
    return jnp.where(blk == own, 0.0, keep_past)


def _moba_prompt_kernel(q_ref, kmean_ref, sf_ref, kaug_ref, vb_ref, o_ref,
                        feat_t, qaug, m_s, acc_s, *, nblk):
    i = pl.program_id(2)
    rows = GROUP * MOBA_BLOCK
    q = q_ref[...]
    qs = jnp.concatenate([q[:, g * HD:(g + 1) * HD] for g in range(GROUP)], axis=0).astype(BF16)
    qaug[:, 0:HD] = qs
    gate_t = lax.dot_general(kmean_ref[...].astype(BF16), qs, (((1,), (1,)), ((), ())),
                             preferred_element_type=F32)
    feat_t[...] = jnp.zeros(feat_t.shape, F32)
    feat_t[F_SEL:F_SEL + nblk, :] = _select_bias(gate_t, i)
    feat_t[F_HI:F_HI + 2 * N_SLOPE_FEATS, :] = sf_ref[...]
    qaug[:, HD:2 * HD] = feat_t[...].T.astype(BF16)

    def scores(n):
        start = pl.multiple_of(n * MOBA_BLOCK, MOBA_BLOCK)
        kn = kaug_ref[pl.ds(start, MOBA_BLOCK), :]
        vn = vb_ref[pl.ds(start, MOBA_BLOCK), :]
        s = lax.dot_general(qaug[...], kn, (((1,), (1,)), ((), ())), preferred_element_type=F32)
        return s, vn

    def probs(s, m):
        return jnp.concatenate([jnp.exp2((s[:, :LANES] - m) * EXP2_C),
                                jnp.exp2((s[:, LANES:] - m) * EXP2_C)], axis=1).astype(BF16)

    def row_max(s):
        cur = jnp.max(jnp.maximum(s[:, :LANES], s[:, LANES:]), axis=-1, keepdims=True)
        return jnp.broadcast_to(cur, (rows, LANES))

    s, vn = scores(i)
    qi = lax.broadcasted_iota(jnp.int32, s.shape, 0) & (MOBA_BLOCK - 1)
    kj = lax.broadcasted_iota(jnp.int32, s.shape, 1)
    s = jnp.where(kj <= qi, s, NEG)
    m = row_max(s)
    m_s[...] = m
    acc_s[...] = jnp.dot(probs(s, m), vn, preferred_element_type=F32)

    def body(n, carry):
        s, vn = scores(n)
        m_old = m_s[...]
        m_new = jnp.maximum(m_old, row_max(s))
        alpha = jnp.exp2((m_old - m_new) * EXP2_C)
        pv = jnp.dot(probs(s, m_new), vn, preferred_element_type=F32)
        acc_s[...] = jnp.concatenate([alpha, alpha], axis=1) * acc_s[...] + pv
        m_s[...] = m_new
        return carry

    lax.fori_loop(0, i, body, 0)
    acc = acc_s[...]
    out = acc[:, 0:HD] / acc[:, HD:HD + 1]
    for g in range(GROUP):
        o_ref[:, g * HD:(g + 1) * HD] = out[g * MOBA_BLOCK:(g + 1) * MOBA_BLOCK, :].astype(o_ref.dtype)


def _moba_prompt(proj, kaug, vb, kmean, batch, t_len):
    nq = t_len // MOBA_BLOCK
    nblk = nq
    rows = GROUP * MOBA_BLOCK
    qcols = GROUP * HD
    return pl.pallas_call(
        functools.partial(_moba_prompt_kernel, nblk=nblk),
        grid=(batch, N_KV, nq),
        in_specs=[
            pl.BlockSpec((MOBA_BLOCK, qcols), lambda b, c, i: (b * nq + i, C_Q // qcols + c)),
            pl.BlockSpec((None, None, nblk, HD), lambda b, c, i: (b, c, 0, 0)),
            pl.BlockSpec((None, 2 * N_SLOPE_FEATS, rows), lambda b, c, i: (c, 0, 0)),
            pl.BlockSpec((None, None, t_len, 2 * HD), lambda b, c, i: (b, c, 0, 0)),
            pl.BlockSpec((None, None, t_len, 2 * HD), lambda b, c, i: (b, c, 0, 0)),
        ],
        out_specs=pl.BlockSpec((MOBA_BLOCK, qcols), lambda b, c, i: (b * nq + i, c)),
        out_shape=jax.ShapeDtypeStruct((batch * t_len, Q_A), BF16),
        scratch_shapes=[
            pltpu.VMEM((LANES, rows), F32),
            pltpu.VMEM((rows, 2 * HD), BF16),
            pltpu.VMEM((rows, LANES), F32),
            pltpu.VMEM((rows, 2 * HD), F32),
        ],
        compiler_params=_params(("parallel", "parallel", "arbitrary"), 40),
        name="moba_prompt",
    )(proj, kmean, _slope_feature_table(), kaug, vb)


def _moba_sample_kernel(pt_ref, q_ref, kn_ref, vn_ref, slope_ref, *rest, n_pages, t_dec):
    del pt_ref
    k_pages = rest[:n_pages]
    v_pages = rest[n_pages:2 * n_pages]
    o_ref = rest[2 * n_pages]
    s_scr, kb_scr = rest[2 * n_pages + 1:]
    past = n_pages * PAGE
    nblk = past // MOBA_BLOCK
    pages_per_blk = MOBA_BLOCK // PAGE
    rows = N_HEADS * t_dec
    assert rows == LANES

    q = q_ref[...]
    q_rows = jnp.concatenate([q[:, h * HD:(h + 1) * HD] for h in range(N_HEADS)], axis=0)
    q_t = q_rows.T.astype(BF16)
    lane = lax.broadcasted_iota(jnp.int32, (HD, LANES), 1)
    rows_per_kv = GROUP * t_dec
    zero = jnp.zeros((HD, LANES), BF16)
    q_bd = [jnp.where(lane // rows_per_kv == c, q_t, zero) for c in range(N_KV)]

    def scores_t(k2d_rows):
        acc = None
        for c in range(N_KV):
            part = jnp.dot(k2d_rows(c), q_bd[c], preferred_element_type=F32)
            acc = part if acc is None else acc + part
        return acc

    for n in range(nblk):
        sums = [jnp.zeros((1, HD), F32) for _ in range(N_KV)]
        for pp in range(pages_per_blk):
            p = n * pages_per_blk + pp
            kc = []
            for c in range(N_KV):
                kf = k_pages[p][pl.ds(c, PAGE, stride=N_KV), :]
                sums[c] = sums[c] + jnp.sum(kf, axis=0, keepdims=True)
                kc.append(kf.astype(BF16))
            s_scr[p * PAGE:(p + 1) * PAGE, :] = scores_t(lambda c: kc[c])
        for c in range(N_KV):
            kb_scr[c, n:n + 1, :] = sums[c] * (1.0 / MOBA_BLOCK)
    gate_t = scores_t(lambda c: kb_scr[c].astype(BF16))
    sel = _select_bias(gate_t, nblk)

    slope = slope_ref[...]
    t_q = lax.broadcasted_iota(jnp.int32, (1, LANES), 1) % t_dec
    q_pos = (past + t_q).astype(F32)

    def logits(raw, k_pos):
        return raw * SCALE - slope * (q_pos - k_pos)

    kn = kn_ref[...]
    vn = vn_ref[...]
    knc = [kn[:, c * HD:(c + 1) * HD].astype(BF16) for c in range(N_KV)]
    t_k = lax.broadcasted_iota(jnp.int32, (t_dec, LANES), 0)
    s_own = logits(scores_t(lambda c: knc[c]), (past + t_k).astype(F32))
    s_own = jnp.where(t_k <= t_q, s_own, NEG)
    m = jnp.max(s_own, axis=0, keepdims=True)

    sub = lax.broadcasted_iota(jnp.int32, (PAGE, LANES), 0)
    for p in range(n_pages):
        n = p // pages_per_blk
        k_pos = (sub + p * PAGE).astype(F32)
        s = logits(s_scr[p * PAGE:(p + 1) * PAGE, :], k_pos) + sel[n:n + 1, :]
        s_scr[p * PAGE:(p + 1) * PAGE, :] = s
        m = jnp.maximum(m, jnp.max(s, axis=0, keepdims=True))

    p_own = jnp.exp(s_own - m)
    l = jnp.sum(p_own, axis=0, keepdims=True)
    row_kv = lax.broadcasted_iota(jnp.int32, (LANES, HD), 0) // rows_per_kv
    contract0 = (((0,), (0,)), ((), ()))
    p_own_b = p_own.astype(BF16)
    acc = jnp.zeros((LANES, HD), F32)
    for c in range(N_KV):
        part = lax.dot_general(p_own_b, vn[:, c * HD:(c + 1) * HD].astype(BF16), contract0,
                               preferred_element_type=F32)
        acc = acc + jnp.where(row_kv == c, part, 0.0)
    for p in range(n_pages):
        pr = jnp.exp(s_scr[p * PAGE:(p + 1) * PAGE, :] - m)
        l = l + jnp.sum(pr, axis=0, keepdims=True)
        pb = pr.astype(BF16)
        for c in range(N_KV):
            vf = v_pages[p][pl.ds(c, PAGE, stride=N_KV), :].astype(BF16)
            part = lax.dot_general(pb, vf, contract0, preferred_element_type=F32)
            acc = acc + jnp.where(row_kv == c, part, 0.0)
    l_col = jnp.broadcast_to(l, (LANES, LANES)).T
    out = acc / l_col
    for h in range(N_HEADS):
        o_ref[:, h * HD:(h + 1) * HD] = out[h * t_dec:(h + 1) * t_dec, :].astype(o_ref.dtype)


def _moba_sample(proj, cache_k2d, cache_v2d, page_table, n_seq, t_dec):
    n_pages = page_table.shape[1]
    past = n_pages * PAGE
    nblk = past // MOBA_BLOCK
    page_rows = PAGE * N_KV
    slopes = np.asarray(2.0 ** (-8.0 * np.arange(1, N_HEADS + 1) / N_HEADS), np.float32)
    slope_row = jnp.asarray(np.repeat(slopes, t_dec)[None, :])

    def page_spec(p):
        return pl.BlockSpec((page_rows, HD), lambda b, pt, p=p: (pt[b, p], 0))

    grid_spec = pltpu.PrefetchScalarGridSpec(
        num_scalar_prefetch=1,
        grid=(n_seq,),
        in_specs=[
            pl.BlockSpec((t_dec, Q_A), lambda b, pt: (b, C_Q // Q_A)),
            pl.BlockSpec((t_dec, KV_A), lambda b, pt: (b, C_K // KV_A)),
            pl.BlockSpec((t_dec, KV_A), lambda b, pt: (b, C_V // KV_A)),
            pl.BlockSpec((1, LANES), lambda b, pt: (0, 0)),
        ] + [page_spec(p) for p in range(n_pages)] * 2,
        out_specs=pl.BlockSpec((t_dec, Q_A), lambda b, pt: (b, 0)),
        scratch_shapes=[
            pltpu.VMEM((past, LANES), F32),
            pltpu.VMEM((N_KV, nblk, HD), F32),
        ],
    )
    return pl.pallas_call(
        functools.partial(_moba_sample_kernel, n_pages=n_pages, t_dec=t_dec),
        grid_spec=grid_spec,
        out_shape=jax.ShapeDtypeStruct((n_seq * t_dec, Q_A), BF16),
        compiler_params=_params(("arbitrary",), 40),
        name="moba_sample",
    )(page_table, proj, proj, proj, slope_row, *([cache_k2d] * n_pages), *([cache_v2d] * n_pages))


DN_TILE = 128
PK_W, PK_QD, PK_KD, PK_QK = 0, 1, 2, 3
INV_BASE_LEVELS = 3


def _conv_silu(x, w, shifted):
    y = x * w[CONV_W - 1:CONV_W, :]
    for d in range(1, CONV_W):
        y = y + shifted(d) * w[CONV_W - 1 - d:CONV_W - d, :]
    return y * _sigmoid(y)


def _delta_prep_kernel(x_ref, halo_ref, ba_ref, cw_ref, al_ref, dt_ref, pk_ref, u_ref, ge_ref,
                       *, seg, tiles_per_step, tiles_per_seq):
    step = pl.program_id(0)
    h = pl.program_id(1)
    levels = int(math.log2(seg))
    ri = lax.broadcasted_iota(jnp.int32, (DN_TILE, DN_TILE), 0)
    ci = lax.broadcasted_iota(jnp.int32, (DN_TILE, DN_TILE), 1)
    same = jnp.where((ri >> levels) == (ci >> levels), 1.0, 0.0)
    incl = jnp.where(ci <= ri, same, 0.0)
    strict = jnp.where(ci < ri, same, 0.0)
    eye = jnp.where(ci == ri, 1.0, 0.0)
    base_levels = min(INV_BASE_LEVELS, levels)
    merge_masks = [jnp.where((ri >> lv) == (ci >> lv), 1.0, 0.0) for lv in range(base_levels, levels + 1)]
    idx_b = jnp.full((DN_TILE, LANES), h, jnp.int32)
    idx_g = idx_b + DN_HEADS
    cw = cw_ref[...]
    nt = (((1,), (1,)), ((), ()))
    after = jnp.where(ci > ri, same, 0.0)
    prefix_suffix = jnp.concatenate([incl, after], axis=0).astype(BF16)

    def front(t):
        rows = slice(t * DN_TILE, (t + 1) * DN_TILE)
        x = x_ref[rows, :]
        if seg == DN_TILE:
            if t == 0:
                tile = step * tiles_per_step
                first = ((tile % tiles_per_seq) == 0).astype(F32)
                halo = halo_ref[...] * (1.0 - first)
            else:
                halo = x_ref[t * DN_TILE - SUBLANES:t * DN_TILE, :]
            xe = jnp.concatenate([halo, x], axis=0)

            def shifted(d):
                return pltpu.roll(xe, d, 0)[SUBLANES:, :]
        else:
            bx = halo_ref[rows, :]
            t_in = lax.broadcasted_iota(jnp.int32, x.shape, 0) & (seg - 1)

            def shifted(d):
                xs = pltpu.roll(x, d, 0)
                bs = pltpu.roll(bx, (d - seg) % DN_TILE, 0)
                return jnp.where(t_in >= d, xs, bs)
        y = _conv_silu(x, cw, shifted)
        qc, kc, v = y[:, 0:DK], y[:, DK:2 * DK], y[:, 2 * DK:3 * DK]
        q = qc * lax.rsqrt(jnp.sum(qc * qc, axis=-1, keepdims=True) + EPS) * (DK ** -0.5)
        k = kc * lax.rsqrt(jnp.sum(kc * kc, axis=-1, keepdims=True) + EPS)

        ba = ba_ref[rows, :]
        beta_all = _sigmoid(ba)
        xg = ba + dt_ref[...]
        softplus = jnp.maximum(xg, 0.0) + jnp.log(1.0 + jnp.exp(-jnp.abs(xg)))
        g_all = -jnp.exp(al_ref[...]) * softplus
        beta = jnp.take_along_axis(beta_all, idx_b, axis=1, mode="promise_in_bounds")
        g = jnp.take_along_axis(g_all, idx_g, axis=1, mode="promise_in_bounds")
        g_hi = g.astype(BF16)
        g_r = g - g_hi.astype(F32)
        g_mid = g_r.astype(BF16)
        g_lo = (g_r - g_mid.astype(F32)).astype(BF16)
        return dict(rows=rows, q=q, k=k, v=v, beta=beta, g3=(g_hi, g_mid, g_lo), kb=k.astype(BF16),
                    qb=q.astype(BF16))

    tiles = [front(t) for t in range(tiles_per_step)]

    def mm(x, y):
        return jnp.dot(x, y, preferred_element_type=F32)

    for piece in range(3):
        for ts in tiles:
            part = mm(prefix_suffix, ts["g3"][piece])
            ts["sums"] = part if piece == 0 else ts["sums"] + part
    for ts in tiles:
        ts["kk"] = lax.dot_general(ts["kb"], ts["kb"], nt, preferred_element_type=F32)
    for ts in tiles:
        ts["qk"] = lax.dot_general(ts["qb"], ts["kb"], nt, preferred_element_type=F32)
    for ts in tiles:
        cum = ts["sums"][:DN_TILE]
        ts["cum"] = cum
        ts["rem"] = ts["sums"][DN_TILE:]
        diff = cum - cum.T
        ts["decay"] = jnp.exp(jnp.where(incl > 0.0, diff, NEG))
        ts["a"] = strict * (ts["beta"] * ts["decay"] * ts["kk"])
        a0 = ts["a"] * merge_masks[0]
        ts["t_inv"] = eye - a0
        ts["pw"] = a0
    for _ in range(base_levels - 1):
        for ts in tiles:
            pwb = ts["pw"].astype(BF16)
            ts["pw"] = mm(pwb, pwb)
        for ts in tiles:
            ts["t_inv"] = ts["t_inv"] + mm(ts["t_inv"].astype(BF16), ts["pw"].astype(BF16))
    for lv in range(base_levels, levels):
        lmask = merge_masks[lv - base_levels + 1] - merge_masks[lv - base_levels]
        for ts in tiles:
            ts["tb"] = ts["t_inv"].astype(BF16)
            ts["tl"] = mm(ts["tb"], (ts["a"] * lmask).astype(BF16))
        for ts in tiles:
            ts["t_inv"] = ts["t_inv"] - mm(ts["tl"].astype(BF16), ts["tb"])
    for ts in tiles:
        ts["gam"] = jnp.exp(ts["cum"])
        ts["tb"] = ts["t_inv"].astype(BF16)
        ts["w"] = mm(ts["tb"], (ts["beta"] * ts["gam"] * ts["k"]).astype(BF16))
    for ts in tiles:
        ts["u"] = mm(ts["tb"], (ts["beta"] * ts["v"]).astype(BF16))
    segs = DN_TILE // seg
    for t, ts in enumerate(tiles):
        rows = ts["rows"]
        pk_ref[rows, PK_W * DK:(PK_W + 1) * DK] = ts["w"].astype(BF16)
        pk_ref[rows, PK_QD * DK:(PK_QD + 1) * DK] = (ts["q"] * ts["gam"]).astype(BF16)
        pk_ref[rows, PK_KD * DK:(PK_KD + 1) * DK] = (ts["k"] * jnp.exp(ts["rem"])).astype(BF16)
        pk_ref[rows, PK_QK * DK:(PK_QK + 1) * DK] = (ts["qk"] * ts["decay"]).astype(BF16)
        u_ref[rows, :] = ts["u"]
        g_end = jnp.exp(ts["cum"] + ts["rem"])
        for j in range(segs):
            ge_ref[t * segs + j] = g_end[j * seg:j * seg + 1, :]


def _delta_prep(proj, halo_src, conv_w_perm, al_row, dt_row, *, seg, t_len):
    n = proj.shape[0]
    tiles_per_step = 4
    tr = tiles_per_step * DN_TILE
    assert n % tr == 0
    steps = n // tr
    qkv_blk = C_QKV // HEAD_COLS
    segs = tr // seg
    if seg == DN_TILE:
        assert t_len % tr == 0
        sub_per_step = tr // SUBLANES
        halo_spec = pl.BlockSpec(
            (SUBLANES, HEAD_COLS), lambda s, h: (jnp.maximum(s * sub_per_step - 1, 0), qkv_blk + h))
    else:
        halo_spec = pl.BlockSpec((tr, HEAD_COLS), lambda s, h: (s, h))
    return pl.pallas_call(
        functools.partial(_delta_prep_kernel, seg=seg, tiles_per_step=tiles_per_step,
                          tiles_per_seq=max(t_len // DN_TILE, 1)),
        grid=(steps, DN_HEADS),
        in_specs=[
            pl.BlockSpec((tr, HEAD_COLS), lambda s, h: (s, qkv_blk + h)),
            halo_spec,
            pl.BlockSpec((tr, LANES), lambda s, h: (s, C_BA // LANES)),
            pl.BlockSpec((CONV_W, HEAD_COLS), lambda s, h: (0, h)),
            pl.BlockSpec((1, LANES), lambda s, h: (0, 0)),
            pl.BlockSpec((1, LANES), lambda s, h: (0, 0)),
        ],
        out_specs=[
            pl.BlockSpec((None, tr, 4 * DK), lambda s, h: (h, s, 0)),
            pl.BlockSpec((None, tr, DV), lambda s, h: (h, s, 0)),
            pl.BlockSpec((None, segs, 1, LANES), lambda s, h: (h, s, 0, 0)),
        ],
        out_shape=[
            jax.ShapeDtypeStruct((DN_HEADS, n, 4 * DK), BF16),
            jax.ShapeDtypeStruct((DN_HEADS, n, DV), F32),
            jax.ShapeDtypeStruct((DN_HEADS, n // seg, 1, LANES), F32),
        ],
        compiler_params=_params(("parallel", "arbitrary"), 32),
        name="delta_prep",
    )(proj, halo_src, proj, conv_w_perm, al_row, dt_row)


def _out_gate(o, z, dnw):
    return _rms(o, dnw) * (z * _sigmoid(z))


def _delta_scan_prompt_kernel(pk_ref, u_ref, ge_ref, z_ref, dnw_ref, o_ref, sout_ref, s_scr, *, hg):
    c = pl.program_id(2)

    @pl.when(c == 0)
    def _():
        s_scr[...] = jnp.zeros(s_scr.shape, F32)

    contract0 = (((0,), (0,)), ((), ()))
    heads = range(hg)
    res = []
    for j in heads:
        wq = jnp.concatenate([pk_ref[j, :, PK_W * DK:(PK_W + 1) * DK],
                              pk_ref[j, :, PK_QD * DK:(PK_QD + 1) * DK]], axis=0)
        res.append(jnp.dot(wq, s_scr[j].astype(BF16), preferred_element_type=F32))
    ub = [(u_ref[j] - res[j][:DN_TILE]).astype(BF16) for j in heads]
    o = [res[j][DN_TILE:] + jnp.dot(pk_ref[j, :, PK_QK * DK:(PK_QK + 1) * DK], ub[j],
                                    preferred_element_type=F32) for j in heads]
    upd = [lax.dot_general(pk_ref[j, :, PK_KD * DK:(PK_KD + 1) * DK], ub[j], contract0,
                           preferred_element_type=F32) for j in heads]
    for j in heads:
        s_scr[j] = ge_ref[j, 0] * s_scr[j] + upd[j]
        cols = slice(j * DV, (j + 1) * DV)
        o_ref[:, cols] = _out_gate(o[j], z_ref[:, cols], dnw_ref[...]).astype(o_ref.dtype)

    @pl.when(c == pl.num_programs(2) - 1)
    def _():
        sout_ref[...] = s_scr[...]


def _delta_scan_prompt(pk, u, ge, proj, dnw_row, batch, t_len):
    hg = 4
    nc = t_len // DN_TILE
    zblk = C_Z // (hg * DV)
    return pl.pallas_call(
        functools.partial(_delta_scan_prompt_kernel, hg=hg),
        grid=(batch, DN_HEADS // hg, nc),
        in_specs=[
            pl.BlockSpec((hg, DN_TILE, 4 * DK), lambda b, g, c: (g, b * nc + c, 0)),
            pl.BlockSpec((hg, DN_TILE, DV), lambda b, g, c: (g, b * nc + c, 0)),
            pl.BlockSpec((hg, 1, 1, LANES), lambda b, g, c: (g, b * nc + c, 0, 0)),
            pl.BlockSpec((DN_TILE, hg * DV), lambda b, g, c: (b * nc + c, zblk + g)),
            pl.BlockSpec((1, DV), lambda b, g, c: (0, 0)),
        ],
        out_specs=[
            pl.BlockSpec((DN_TILE, hg * DV), lambda b, g, c: (b * nc + c, g)),
            pl.BlockSpec((None, hg, DK, DV), lambda b, g, c: (b, g, 0, 0)),
        ],
        out_shape=[
            jax.ShapeDtypeStruct((batch * t_len, DN_V), BF16),
            jax.ShapeDtypeStruct((batch, DN_HEADS, DK, DV), F32),
        ],
        scratch_shapes=[pltpu.VMEM((hg, DK, DV), F32)],
        compiler_params=_params(("parallel", "parallel", "arbitrary"), 32),
        name="delta_scan_prompt",
    )(pk, u, ge, proj, dnw_row)


def _delta_scan_sample_kernel(pk_ref, u_ref, ge_ref, z_ref, dnw_ref, s0_ref, o_ref, sout_ref,
                              ub_scr, oq_scr, pk32_scr, *, hg, t_dec, seq_unroll):
    n_seq = DN_TILE // t_dec
    contract0 = (((0,), (0,)), ((), ()))
    for j in range(hg):
        pk32_scr[j] = pk_ref[j, :, 0:3 * DK].astype(F32)

    def group(gi, carry):
        pairs = []
        for a in range(seq_unroll):
            i = gi * seq_unroll + a
            r0 = pl.multiple_of(i * t_dec, t_dec)
            pairs += [(i, r0, j) for j in range(hg)]
        res = []
        for i, r0, j in pairs:
            wq = jnp.concatenate([pk32_scr[j, pl.ds(r0, t_dec), PK_W * DK:(PK_W + 1) * DK],
                                  pk32_scr[j, pl.ds(r0, t_dec), PK_QD * DK:(PK_QD + 1) * DK]], axis=0)
            res.append(jnp.dot(wq.astype(BF16), s0_ref[i, j].astype(BF16), preferred_element_type=F32))
        upd = []
        for (i, r0, j), r in zip(pairs, res):
            uu = u_ref[j, pl.ds(r0, t_dec), :] - r[:t_dec]
            ub_scr[j, pl.ds(r0, t_dec), :] = uu
            oq_scr[j, pl.ds(r0, t_dec), :] = r[t_dec:]
            kd = pk32_scr[j, pl.ds(r0, t_dec), PK_KD * DK:(PK_KD + 1) * DK]
            upd.append(lax.dot_general(kd.astype(BF16), uu.astype(BF16), contract0,
                                       preferred_element_type=F32))
        for (i, r0, j), up in zip(pairs, upd):
            sout_ref[i, j] = ge_ref[j, i] * s0_ref[i, j] + up
        return carry

    lax.fori_loop(0, n_seq // seq_unroll, group, 0)
    o = [oq_scr[j] + jnp.dot(pk_ref[j, :, PK_QK * DK:(PK_QK + 1) * DK], ub_scr[j].astype(BF16),
                             preferred_element_type=F32) for j in range(hg)]
    for j in range(hg):
        cols = slice(j * DV, (j + 1) * DV)
        o_ref[:, cols] = _out_gate(o[j], z_ref[:, cols], dnw_ref[...]).astype(o_ref.dtype)


def _delta_scan_sample(pk, u, ge, proj, dnw_row, s0, n_seq, t_dec):
    hg = 4
    seq_per_tile = DN_TILE // t_dec
    tiles = n_seq // seq_per_tile
    zblk = C_Z // (hg * DV)
    return pl.pallas_call(
        functools.partial(_delta_scan_sample_kernel, hg=hg, t_dec=t_dec, seq_unroll=4),
        grid=(tiles, DN_HEADS // hg),
        in_specs=[
            pl.BlockSpec((hg, DN_TILE, 4 * DK), lambda r, g: (g, r, 0)),
            pl.BlockSpec((hg, DN_TILE, DV), lambda r, g: (g, r, 0)),
            pl.BlockSpec((hg, seq_per_tile, 1, LANES), lambda r, g: (g, r, 0, 0)),
            pl.BlockSpec((DN_TILE, hg * DV), lambda r, g: (r, zblk + g)),
            pl.BlockSpec((1, DV), lambda r, g: (0, 0)),
            pl.BlockSpec((seq_per_tile, hg, DK, DV), lambda r, g: (r, g, 0, 0)),
        ],
        out_specs=[
            pl.BlockSpec((DN_TILE, hg * DV), lambda r, g: (r, g)),
            pl.BlockSpec((seq_per_tile, hg, DK, DV), lambda r, g: (r, g, 0, 0)),
        ],
        out_shape=[
            jax.ShapeDtypeStruct((n_seq * t_dec, DN_V), BF16),
            jax.ShapeDtypeStruct((n_seq, DN_HEADS, DK, DV), F32),
        ],
        scratch_shapes=[pltpu.VMEM((hg, DN_TILE, DV), F32), pltpu.VMEM((hg, DN_TILE, DV), F32),
                        pltpu.VMEM((hg, DN_TILE, 3 * DK), F32)],
        compiler_params=_params(("parallel", "parallel"), 48),
        name="delta_scan_sample",
    )(pk, u, ge, proj, dnw_row, s0)


def _merge_kernel(oa_ref, od_ref, wa_ref, wd_ref, ga_ref, gd_ref, o_ref):
    ya = jnp.dot(oa_ref[...], wa_ref[...], preferred_element_type=F32)
    yd = jnp.dot(od_ref[...], wd_ref[...], preferred_element_type=F32)
    o_ref[...] = (_sigmoid(ga_ref[...]) * ya + _sigmoid(gd_ref[...]) * yd).astype(o_ref.dtype)


def _merge(oa, od, wa, wd, proj):
    n = oa.shape[0]
    tm, tn = 512, 512
    return pl.pallas_call(
        _merge_kernel,
        grid=(n // tm, D_MODEL // tn),
        in_specs=[
            pl.BlockSpec((tm, Q_A), lambda i, j: (i, 0)),
            pl.BlockSpec((tm, DN_V), lambda i, j: (i, 0)),
            pl.BlockSpec((Q_A, tn), lambda i, j: (0, j)),
            pl.BlockSpec((DN_V, tn), lambda i, j: (0, j)),
            pl.BlockSpec((tm, tn), lambda i, j: (i, C_GA // tn + j)),
            pl.BlockSpec((tm, tn), lambda i, j: (i, C_GD // tn + j)),
        ],
        out_specs=pl.BlockSpec((tm, tn), lambda i, j: (i, j)),
        out_shape=jax.ShapeDtypeStruct((n, D_MODEL), BF16),
        compiler_params=_params(("parallel", "arbitrary"), 40),
        name="merge",
    )(oa, od, wa, wd, proj, proj)


def _out_norm_kernel(m_ref, w_ref, x_ref, g_ref, o_ref):
    t = jnp.dot(m_ref[...], w_ref[...], preferred_element_type=F32)
    o_ref[...] = x_ref[...] + _rms(t, g_ref[...])


def _out_norm(mixed, w_out, x2d, g_row):
    n = mixed.shape[0]
    tm = 512
    return pl.pallas_call(
        _out_norm_kernel,
        grid=(n // tm,),
        in_specs=[
            pl.BlockSpec((tm, D_MODEL), lambda i: (i, 0)),
            pl.BlockSpec((D_MODEL, D_MODEL), lambda i: (0, 0)),
            pl.BlockSpec((tm, D_MODEL), lambda i: (i, 0)),
            pl.BlockSpec((1, D_MODEL), lambda i: (0, 0)),
        ],
        out_specs=pl.BlockSpec((tm, D_MODEL), lambda i: (i, 0)),
        out_shape=jax.ShapeDtypeStruct((n, D_MODEL), F32),
        compiler_params=_params(("parallel",), 48),
        name="out_norm",
    )(mixed, w_out, x2d, g_row)


def _mlp_kernel(x_ref, gpre_ref, wu_ref, wd_ref, gpost_ref, o_ref, h_ref, acc_ref):
    f = pl.program_id(1)

    @pl.when(f == 0)
    def _():
        h_ref[...] = _rms(x_ref[...], gpre_ref[...]).astype(BF16)
        acc_ref[...] = jnp.zeros(acc_ref.shape, F32)

    up = jnp.dot(h_ref[...], wu_ref[...], preferred_element_type=F32)
    act = jnp.square(jnp.maximum(up, 0.0)).astype(BF16)
    acc_ref[...] += jnp.dot(act, wd_ref[...], preferred_element_type=F32)

    @pl.when(f == pl.num_programs(1) - 1)
    def _():
        o_ref[...] = x_ref[...] + _rms(acc_ref[...], gpost_ref[...])


def _mlp(x2d, gpre, w_up, w_down, gpost):
    n = x2d.shape[0]
    tm, tf = 512, 1024
    return pl.pallas_call(
        _mlp_kernel,
        grid=(n // tm, D_FF // tf),
        in_specs=[
            pl.BlockSpec((tm, D_MODEL), lambda i, f: (i, 0)),
            pl.BlockSpec((1, D_MODEL), lambda i, f: (0, 0)),
            pl.BlockSpec((D_MODEL, tf), lambda i, f: (0, f)),
            pl.BlockSpec((tf, D_MODEL), lambda i, f: (f, 0)),
            pl.BlockSpec((1, D_MODEL), lambda i, f: (0, 0)),
        ],
        out_specs=pl.BlockSpec((tm, D_MODEL), lambda i, f: (i, 0)),
        out_shape=jax.ShapeDtypeStruct((n, D_MODEL), F32),
        scratch_shapes=[pltpu.VMEM((tm, D_MODEL), BF16), pltpu.VMEM((tm, D_MODEL), F32)],
        compiler_params=_params(("parallel", "arbitrary"), 48),
        name="mlp",
    )(x2d, gpre, w_up, w_down, gpost)


def _head_major(x):
    lead = x.shape[:-1]
    return jnp.swapaxes(x.reshape(*lead, 3, DN_HEADS, DK), -3, -2).reshape(*lead, CONV_DIM)


def _part_major(x):
    lead = x.shape[:-1]
    return jnp.swapaxes(x.reshape(*lead, DN_HEADS, 3, DK), -3, -2).reshape(*lead, CONV_DIM)


def _pack_w_in(w_in):
    old_qkv = Q_A + 2 * KV_A
    old_z = old_qkv + CONV_DIM
    old_b = old_z + DN_V
    old_ga = old_b + 2 * DN_HEADS
    parts = [
        w_in[:, :old_qkv],
        _head_major(w_in[:, old_qkv:old_z]),
        w_in[:, old_z:old_b],
        w_in[:, old_ga:],
        w_in[:, old_b:old_ga],
        jnp.zeros((D_MODEL, LANES - 2 * DN_HEADS), w_in.dtype),
    ]
    return jnp.concatenate(parts, axis=1).astype(BF16)


def _lane_row(vals, offset):
    return jnp.zeros((1, LANES), F32).at[0, offset:offset + vals.shape[0]].set(vals.astype(F32))


def _layer(x2d, attend, delta, w):
    proj = _in_proj(x2d, w["g_mix_pre"], w["w_in"])
    o_a = attend(proj)
    o_d, s_final = delta(proj)
    mixed = _merge(o_a, o_d, w["w_ba"], w["w_bd"], proj)
    x1 = _out_norm(mixed, w["w_out"], x2d, w["g_mix_post"])
    y = _mlp(x1, w["g_mlp_pre"], w["w_up"], w["w_down"], w["g_mlp_post"])
    return y, proj, s_final


def kernel(x_prompt, x_sample, cache_k, cache_v, state_delta, state_conv, page_table, w_in, conv_w, a_log,
           dt_bias, dn_norm_w, w_branch_attn, w_branch_delta, w_out, g_mix_pre, g_mix_post, g_mlp_pre,
           g_mlp_post, w_up, w_down):
    depth = w_in.shape[0]
    assert depth == 1
    b_p, t_p, _ = x_prompt.shape
    b_s, t_s, _ = x_sample.shape
    l = 0
    w = {
        "w_in": _pack_w_in(w_in[l]),
        "g_mix_pre": g_mix_pre[l][None, :],
        "g_mix_post": g_mix_post[l][None, :],
        "g_mlp_pre": g_mlp_pre[l][None, :],
        "g_mlp_post": g_mlp_post[l][None, :],
        "w_ba": w_branch_attn[l].astype(BF16),
        "w_bd": w_branch_delta[l].astype(BF16),
        "w_out": w_out[l].astype(BF16),
        "w_up": w_up[l].astype(BF16),
        "w_down": w_down[l].astype(BF16),
    }
    cw = _head_major(conv_w[l])
    al_row = _lane_row(a_log[l], DN_HEADS)
    dt_row = _lane_row(dt_bias[l], DN_HEADS)
    dnw_row = dn_norm_w[l][None, :]

    def attend_p(proj):
        kaug, vb, kmean = _kprep(proj, b_p, t_p)
        return _moba_prompt(proj, kaug, vb, kmean, b_p, t_p)

    def delta_p(proj):
        pk, u, ge = _delta_prep(proj, proj, cw, al_row, dt_row, seg=DN_TILE, t_len=t_p)
        return _delta_scan_prompt(pk, u, ge, proj, dnw_row, b_p, t_p)

    y_p, proj_p, d_p = _layer(x_prompt.reshape(b_p * t_p, D_MODEL), attend_p, delta_p, w)

    n_pool = cache_k.shape[1]
    ck2d = cache_k[l].reshape(n_pool * PAGE * N_KV, HD)
    cv2d = cache_v[l].reshape(n_pool * PAGE * N_KV, HD)
    conv_rows = _head_major(state_conv[l])
    halo_s = jnp.pad(conv_rows, ((0, 0), (t_s - (CONV_W - 1), 0), (0, 0))).reshape(b_s * t_s, CONV_DIM)

    def attend_s(proj):
        return _moba_sample(proj, ck2d, cv2d, page_table, b_s, t_s)

    def delta_s(proj):
        pk, u, ge = _delta_prep(proj, halo_s, cw, al_row, dt_row, seg=t_s, t_len=t_s)
        return _delta_scan_sample(pk, u, ge, proj, dnw_row, state_delta[l], b_s, t_s)

    y_s, proj_s, d_s = _layer(x_sample.reshape(b_s * t_s, D_MODEL), attend_s, delta_s, w)

    def kv_out(proj, b, t):
        k = proj[:, C_K:C_K + KV_A].reshape(1, b, t, N_KV, HD)
        v = proj[:, C_V:C_V + KV_A].reshape(1, b, t, N_KV, HD)
        return k, v

    def conv_out(proj, b, t):
        raw = proj[:, C_QKV:C_QKV + CONV_DIM].reshape(b, t, CONV_DIM)[:, t - (CONV_W - 1):, :]
        return _part_major(raw)[None]

    k_p, v_p = kv_out(proj_p, b_p, t_p)
    k_s, v_s = kv_out(proj_s, b_s, t_s)
    return (y_p.reshape(b_p, t_p, D_MODEL), y_s.reshape(b_s, t_s, D_MODEL), k_p, v_p, d_p[None],
            conv_out(proj_p, b_p, t_p), k_s, v_s, d_s[None], conv_out(proj_s, b_s, t_s))
```

```python
import functools
import math

import numpy as np
import jax
import jax.numpy as jnp
from jax import lax
from jax.experimental import pallas as pl
from jax.experimental.pallas import tpu as pltpu

F32 = jnp.float32
BF16 = jnp.bfloat16

D_MODEL = 2048
N_HEADS = 16
N_KV = 4
HD = 128
GROUP = N_HEADS // N_KV
MOBA_BLOCK = 256
MOBA_TOPK = 3
PAGE = 128
DN_HEADS = 16
DK = 128
DV = 128
CONV_W = 4
Q_A = N_HEADS * HD
KV_A = N_KV * HD
DN_QK = DN_HEADS * DK
DN_V = DN_HEADS * DV
CONV_DIM = 2 * DN_QK + DN_V
D_FF = 4 * D_MODEL
EPS = 1e-6

LANES = 128
SUBLANES = 8

C_Q = 0
C_K = C_Q + Q_A
C_V = C_K + KV_A
C_QKV = C_V + KV_A
C_Z = C_QKV + CONV_DIM
C_GA = C_Z + DN_V
C_GD = C_GA + D_MODEL
C_BA = C_GD + D_MODEL
P_DIM = C_BA + LANES
HEAD_COLS = 3 * DK

NEG = -1e30
SCALE = HD ** -0.5
EXP2_C = SCALE * math.log2(math.e)


def _params(sem, vmem_mib):
    return pltpu.CompilerParams(dimension_semantics=sem, vmem_limit_bytes=vmem_mib * 2**20)


def _rms(x, g):
    return x * lax.rsqrt(jnp.mean(x * x, axis=-1, keepdims=True) + EPS) * g


def _sigmoid(x):
    return 1.0 / (1.0 + jnp.exp(-x))


def _in_proj_kernel(x_ref, g_ref, w_ref, o_ref, h_ref):
    @pl.when(pl.program_id(1) == 0)
    def _():
        h_ref[...] = _rms(x_ref[...], g_ref[...]).astype(BF16)

    o_ref[...] = jnp.dot(h_ref[...], w_ref[...], preferred_element_type=F32)


def _in_proj(x2d, g_row, wp):
    n = x2d.shape[0]
    tm = 1024 if n % 1024 == 0 else 512
    tn = 1408
    assert n % tm == 0 and P_DIM % tn == 0
    return pl.pallas_call(
        _in_proj_kernel,
        grid=(n // tm, P_DIM // tn),
        in_specs=[
            pl.BlockSpec((tm, D_MODEL), lambda i, j: (i, 0)),
            pl.BlockSpec((1, D_MODEL), lambda i, j: (0, 0)),
            pl.BlockSpec((D_MODEL, tn), lambda i, j: (0, j)),
        ],
        out_specs=pl.BlockSpec((tm, tn), lambda i, j: (i, j)),
        out_shape=jax.ShapeDtypeStruct((n, P_DIM), F32),
        scratch_shapes=[pltpu.VMEM((tm, D_MODEL), BF16)],
        compiler_params=_params(("parallel", "arbitrary"), 52),
        name="in_proj",
    )(x2d, g_row, wp)


N_SLOPE_FEATS = 4
F_SEL = 0
F_HI = 16
F_LO = F_HI + N_SLOPE_FEATS
MAX_BLOCKS = 16


def _kprep_kernel(k_ref, v_ref, kaug_ref, vb_ref, kmean_ref, *, blocks_per_step):
    step = pl.program_id(1)
    lane = lax.broadcasted_iota(jnp.int32, (MOBA_BLOCK, LANES), 1)
    row = lax.broadcasted_iota(jnp.int32, (MOBA_BLOCK, LANES), 0).astype(F32)
    is_hi = jnp.where(lane >= F_HI, jnp.where(lane < F_LO, 1.0, 0.0), 0.0)
    is_lo = jnp.where(lane >= F_LO, jnp.where(lane < F_LO + N_SLOPE_FEATS, 1.0, 0.0), 0.0)
    ones_col = jnp.where(lane == 0, 1.0, 0.0).astype(BF16)
    for j in range(blocks_per_step):
        n = step * blocks_per_step + j
        rows = slice(j * MOBA_BLOCK, (j + 1) * MOBA_BLOCK)
        k = k_ref[rows, :]
        v = v_ref[rows, :]
        mean = jnp.mean(k, axis=0, keepdims=True)
        feat = (jnp.where(lane == n, 1.0, 0.0)
                + is_hi * (n * MOBA_BLOCK).astype(F32)
                + is_lo * row).astype(BF16)
        for c in range(N_KV):
            cols = slice(c * HD, (c + 1) * HD)
            kaug_ref[c, rows, 0:HD] = k[:, cols].astype(BF16)
            kaug_ref[c, rows, HD:2 * HD] = feat
            vb_ref[c, rows, 0:HD] = v[:, cols].astype(BF16)
            vb_ref[c, rows, HD:2 * HD] = ones_col
            kmean_ref[c, j:j + 1, :] = mean[:, cols]


def _kprep(proj, batch, t_len):
    nblk = t_len // MOBA_BLOCK
    bps = min(nblk, 8)
    assert nblk % bps == 0 and nblk <= MAX_BLOCKS
    steps = nblk // bps
    rows = bps * MOBA_BLOCK
    return pl.pallas_call(
        functools.partial(_kprep_kernel, blocks_per_step=bps),
        grid=(batch, steps),
        in_specs=[
            pl.BlockSpec((rows, KV_A), lambda b, s: (b * steps + s, C_K // KV_A)),
            pl.BlockSpec((rows, KV_A), lambda b, s: (b * steps + s, C_V // KV_A)),
        ],
        out_specs=[
            pl.BlockSpec((None, N_KV, rows, 2 * HD), lambda b, s: (b, 0, s, 0)),
            pl.BlockSpec((None, N_KV, rows, 2 * HD), lambda b, s: (b, 0, s, 0)),
            pl.BlockSpec((None, N_KV, bps, HD), lambda b, s: (b, 0, s, 0)),
        ],
        out_shape=[
            jax.ShapeDtypeStruct((batch, N_KV, t_len, 2 * HD), BF16),
            jax.ShapeDtypeStruct((batch, N_KV, t_len, 2 * HD), BF16),
            jax.ShapeDtypeStruct((batch, N_KV, nblk, HD), F32),
        ],
        compiler_params=_params(("parallel", "arbitrary"), 40),
        name="kprep",
    )(proj, proj)


def _slope_pieces():
    slopes = np.asarray(2.0 ** (-8.0 * np.arange(1, N_HEADS + 1) / N_HEADS), np.float32).astype(np.float64)
    x = slopes / SCALE
    pieces = []
    for _ in range(N_SLOPE_FEATS):
        p = x.astype(np.float32).astype(jnp.bfloat16).astype(np.float64)
        pieces.append(p)
        x = x - p
    return np.stack(pieces, axis=1)


def _slope_feature_table():
    pieces = _slope_pieces()
    tab = np.zeros((N_KV, 2 * N_SLOPE_FEATS, GROUP * MOBA_BLOCK), np.float32)
    for c in range(N_KV):
        for g in range(GROUP):
            cols = slice(g * MOBA_BLOCK, (g + 1) * MOBA_BLOCK)
            for f in range(N_SLOPE_FEATS):
                tab[c, f, cols] = pieces[c * GROUP + g, f]
                tab[c, N_SLOPE_FEATS + f, cols] = pieces[c * GROUP + g, f]
    return jnp.asarray(tab)


def _select_bias(gate_t, own):
    nblk = gate_t.shape[0]
    blk = lax.broadcasted_iota(jnp.int32, gate_t.shape, 0)
    past = blk < own
    gm = jnp.where(past, gate_t, -jnp.inf)
    rank = jnp.zeros(gate_t.shape, F32)
    for m in range(nblk):
        row = gm[m:m + 1, :]
        tie = jnp.where(blk > m, 1.0, 0.0)
        rank = rank + jnp.where(row > gm, 1.0, jnp.where(row == gm, tie, 0.0))
    keep_past = jnp.where(past, jnp.where(rank < MOBA_TOPK - 0.5, 0.0, NEG), NEG)
    return jnp.where(blk == own, 0.0, keep_past)


def _moba_prompt_kernel(q_ref, kmean_ref, sf_ref, kaug_ref, vb_ref, o_ref,
                        feat_t, qaug, m_s, acc_s, s_a, s_b, *, nblk):
    i = pl.program_id(2)
    rows = GROUP * MOBA_BLOCK
    q = q_ref[...]
    qs = jnp.concatenate([q[:, g * HD:(g + 1) * HD] for g in range(GROUP)], axis=0).astype(BF16)
    qaug[:, 0:HD] = qs
    gate_t = lax.dot_general(kmean_ref[...].astype(BF16), qs, (((1,), (1,)), ((), ())),
                             preferred_element_type=F32)
    feat_t[...] = jnp.zeros(feat_t.shape, F32)
    feat_t[F_SEL:F_SEL + nblk, :] = _select_bias(gate_t, i)
    feat_t[F_HI:F_HI + 2 * N_SLOPE_FEATS, :] = sf_ref[...]
    qaug[:, HD:2 * HD] = feat_t[...].T.astype(BF16)

    def raw_scores(n):
        start = pl.multiple_of(n * MOBA_BLOCK, MOBA_BLOCK)
        kn = kaug_ref[pl.ds(start, MOBA_BLOCK), :]
        return lax.dot_general(qaug[...], kn, (((1,), (1,)), ((), ())), preferred_element_type=F32)

    def values(n):
        return vb_ref[pl.ds(pl.multiple_of(n * MOBA_BLOCK, MOBA_BLOCK), MOBA_BLOCK), :]

    def probs(s, m):
        return jnp.concatenate([jnp.exp2((s[:, :LANES] - m) * EXP2_C),
                                jnp.exp2((s[:, LANES:] - m) * EXP2_C)], axis=1).astype(BF16)

    def row_max(s):
        cur = jnp.max(jnp.maximum(s[:, :LANES], s[:, LANES:]), axis=-1, keepdims=True)
        return jnp.broadcast_to(cur, (rows, LANES))

    s = raw_scores(i)
    qi = lax.broadcasted_iota(jnp.int32, s.shape, 0) & (MOBA_BLOCK - 1)
    kj = lax.broadcasted_iota(jnp.int32, s.shape, 1)
    s = jnp.where(kj <= qi, s, NEG)
    m = row_max(s)
    m_s[...] = m
    acc_s[...] = jnp.dot(probs(s, m), values(i), preferred_element_type=F32)

    s_a[...] = raw_scores(0)

    def step(n, cur, nxt):
        if nxt is not None:
            nxt[...] = raw_scores(jnp.minimum(n + 1, i - 1))
        s = cur[...]
        m_old = m_s[...]
        m_new = jnp.maximum(m_old, row_max(s))
        alpha = jnp.exp2((m_old - m_new) * EXP2_C)
        pv = jnp.dot(probs(s, m_new), values(n), preferred_element_type=F32)
        acc_s[...] = jnp.concatenate([alpha, alpha], axis=1) * acc_s[...] + pv
        m_s[...] = m_new

    def body(j, carry):
        step(2 * j, s_a, s_b)
        step(2 * j + 1, s_b, s_a)
        return carry

    lax.fori_loop(0, i // 2, body, 0)

    @pl.when(i % 2 == 1)
    def _():
        step(i - 1, s_a, None)

    acc = acc_s[...]
    out = acc[:, 0:HD] / acc[:, HD:HD + 1]
    for g in range(GROUP):
        o_ref[:, g * HD:(g + 1) * HD] = out[g * MOBA_BLOCK:(g + 1) * MOBA_BLOCK, :].astype(o_ref.dtype)


def _moba_prompt(proj, kaug, vb, kmean, batch, t_len):
    nq = t_len // MOBA_BLOCK
    nblk = nq
    rows = GROUP * MOBA_BLOCK
    qcols = GROUP * HD
    return pl.pallas_call(
        functools.partial(_moba_prompt_kernel, nblk=nblk),
        grid=(batch, N_KV, nq),
        in_specs=[
            pl.BlockSpec((MOBA_BLOCK, qcols), lambda b, c, i: (b * nq + i, C_Q // qcols + c)),
            pl.BlockSpec((None, None, nblk, HD), lambda b, c, i: (b, c, 0, 0)),
            pl.BlockSpec((None, 2 * N_SLOPE_FEATS, rows), lambda b, c, i: (c, 0, 0)),
            pl.BlockSpec((None, None, t_len, 2 * HD), lambda b, c, i: (b, c, 0, 0)),
            pl.BlockSpec((None, None, t_len, 2 * HD), lambda b, c, i: (b, c, 0, 0)),
        ],
        out_specs=pl.BlockSpec((MOBA_BLOCK, qcols), lambda b, c, i: (b * nq + i, c)),
        out_shape=jax.ShapeDtypeStruct((batch * t_len, Q_A), BF16),
        scratch_shapes=[
            pltpu.VMEM((LANES, rows), F32),
            pltpu.VMEM((rows, 2 * HD), BF16),
            pltpu.VMEM((rows, LANES), F32),
            pltpu.VMEM((rows, 2 * HD), F32),
            pltpu.VMEM((rows, MOBA_BLOCK), F32),
            pltpu.VMEM((rows, MOBA_BLOCK), F32),
        ],
        compiler_params=_params(("parallel", "parallel", "arbitrary"), 40),
        name="moba_prompt",
    )(proj, kmean, _slope_feature_table(), kaug, vb)


def _moba_sample_kernel(pt_ref, q_ref, kn_ref, vn_ref, slope_ref, *rest, n_pages, t_dec):
    del pt_ref
    k_pages = rest[:n_pages]
    v_pages = rest[n_pages:2 * n_pages]
    o_ref = rest[2 * n_pages]
    s_scr, kb_scr = rest[2 * n_pages + 1:]
    past = n_pages * PAGE
    nblk = past // MOBA_BLOCK
    pages_per_blk = MOBA_BLOCK // PAGE
    rows = N_HEADS * t_dec
    assert rows == LANES

    q = q_ref[...]
    q_rows = jnp.concatenate([q[:, h * HD:(h + 1) * HD] for h in range(N_HEADS)], axis=0)
    q_t = q_rows.T.astype(BF16)
    lane = lax.broadcasted_iota(jnp.int32, (HD, LANES), 1)
    rows_per_kv = GROUP * t_dec
    zero = jnp.zeros((HD, LANES), BF16)
    q_bd = [jnp.where(lane // rows_per_kv == c, q_t, zero) for c in range(N_KV)]

    def scores_t(k2d_rows):
        acc = None
        for c in range(N_KV):
            part = jnp.dot(k2d_rows(c), q_bd[c], preferred_element_type=F32)
            acc = part if acc is None else acc + part
        return acc

    for n in range(nblk):
        sums = [jnp.zeros((1, HD), F32) for _ in range(N_KV)]
        for pp in range(pages_per_blk):
            p = n * pages_per_blk + pp
            kc = []
            for c in range(N_KV):
                kf = k_pages[p][pl.ds(c, PAGE, stride=N_KV), :]
                sums[c] = sums[c] + jnp.sum(kf, axis=0, keepdims=True)
                kc.append(kf.astype(BF16))
            s_scr[p * PAGE:(p + 1) * PAGE, :] = scores_t(lambda c: kc[c])
        for c in range(N_KV):
            kb_scr[c, n:n + 1, :] = sums[c] * (1.0 / MOBA_BLOCK)
    gate_t = scores_t(lambda c: kb_scr[c].astype(BF16))
    sel = _select_bias(gate_t, nblk)

    slope = slope_ref[...]
    t_q = lax.broadcasted_iota(jnp.int32, (1, LANES), 1) % t_dec
    q_pos = (past + t_q).astype(F32)

    def logits(raw, k_pos):
        return raw * SCALE - slope * (q_pos - k_pos)

    kn = kn_ref[...]
    vn = vn_ref[...]
    knc = [kn[:, c * HD:(c + 1) * HD].astype(BF16) for c in range(N_KV)]
    t_k = lax.broadcasted_iota(jnp.int32, (t_dec, LANES), 0)
    s_own = logits(scores_t(lambda c: knc[c]), (past + t_k).astype(F32))
    s_own = jnp.where(t_k <= t_q, s_own, NEG)
    m = jnp.max(s_own, axis=0, keepdims=True)

    sub = lax.broadcasted_iota(jnp.int32, (PAGE, LANES), 0)
    for p in range(n_pages):
        n = p // pages_per_blk
        k_pos = (sub + p * PAGE).astype(F32)
        s = logits(s_scr[p * PAGE:(p + 1) * PAGE, :], k_pos) + sel[n:n + 1, :]
        s_scr[p * PAGE:(p + 1) * PAGE, :] = s
        m = jnp.maximum(m, jnp.max(s, axis=0, keepdims=True))

    p_own = jnp.exp(s_own - m)
    l = jnp.sum(p_own, axis=0, keepdims=True)
    row_kv = lax.broadcasted_iota(jnp.int32, (LANES, HD), 0) // rows_per_kv
    contract0 = (((0,), (0,)), ((), ()))
    p_own_b = p_own.astype(BF16)
    acc = jnp.zeros((LANES, HD), F32)
    for c in range(N_KV):
        part = lax.dot_general(p_own_b, vn[:, c * HD:(c + 1) * HD].astype(BF16), contract0,
                               preferred_element_type=F32)
        acc = acc + jnp.where(row_kv == c, part, 0.0)
    for p in range(n_pages):
        pr = jnp.exp(s_scr[p * PAGE:(p + 1) * PAGE, :] - m)
        l = l + jnp.sum(pr, axis=0, keepdims=True)
        pb = pr.astype(BF16)
        for c in range(N_KV):
            vf = v_pages[p][pl.ds(c, PAGE, stride=N_KV), :].astype(BF16)
            part = lax.dot_general(pb, vf, contract0, preferred_element_type=F32)
            acc = acc + jnp.where(row_kv == c, part, 0.0)
    l_col = jnp.broadcast_to(l, (LANES, LANES)).T
    out = acc / l_col
    for h in range(N_HEADS):
        o_ref[:, h * HD:(h + 1) * HD] = out[h * t_dec:(h + 1) * t_dec, :].astype(o_ref.dtype)


def _moba_sample(proj, cache_k2d, cache_v2d, page_table, n_seq, t_dec):
    n_pages = page_table.shape[1]
    past = n_pages * PAGE
    nblk = past // MOBA_BLOCK
    page_rows = PAGE * N_KV
    slopes = np.asarray(2.0 ** (-8.0 * np.arange(1, N_HEADS + 1) / N_HEADS), np.float32)
    slope_row = jnp.asarray(np.repeat(slopes, t_dec)[None, :])

    def page_spec(p):
        return pl.BlockSpec((page_rows, HD), lambda b, pt, p=p: (pt[b, p], 0))

    grid_spec = pltpu.PrefetchScalarGridSpec(
        num_scalar_prefetch=1,
        grid=(n_seq,),
        in_specs=[
            pl.BlockSpec((t_dec, Q_A), lambda b, pt: (b, C_Q // Q_A)),
            pl.BlockSpec((t_dec, KV_A), lambda b, pt: (b, C_K // KV_A)),
            pl.BlockSpec((t_dec, KV_A), lambda b, pt: (b, C_V // KV_A)),
            pl.BlockSpec((1, LANES), lambda b, pt: (0, 0)),
        ] + [page_spec(p) for p in range(n_pages)] * 2,
        out_specs=pl.BlockSpec((t_dec, Q_A), lambda b, pt: (b, 0)),
        scratch_shapes=[
            pltpu.VMEM((past, LANES), F32),
            pltpu.VMEM((N_KV, nblk, HD), F32),
        ],
    )
    return pl.pallas_call(
        functools.partial(_moba_sample_kernel, n_pages=n_pages, t_dec=t_dec),
        grid_spec=grid_spec,
        out_shape=jax.ShapeDtypeStruct((n_seq * t_dec, Q_A), BF16),
        compiler_params=_params(("arbitrary",), 40),
        name="moba_sample",
    )(page_table, proj, proj, proj, slope_row, *([cache_k2d] * n_pages), *([cache_v2d] * n_pages))


DN_TILE = 128
PK_W, PK_QD, PK_KD, PK_QK = 0, 1, 2, 3
INV_BASE_LEVELS = 3


def _conv_silu(x, w, shifted):
    y = x * w[CONV_W - 1:CONV_W, :]
    for d in range(1, CONV_W):
        y = y + shifted(d) * w[CONV_W - 1 - d:CONV_W - d, :]
    return y * _sigmoid(y)


def _delta_prep_kernel(x_ref, halo_ref, ba_ref, cw_ref, al_ref, dt_ref, pk_ref, u_ref, ge_ref,
                       *, seg, tiles_per_step, tiles_per_seq):
    step = pl.program_id(0)
    h = pl.program_id(1)
    levels = int(math.log2(seg))
    ri = lax.broadcasted_iota(jnp.int32, (DN_TILE, DN_TILE), 0)
    ci = lax.broadcasted_iota(jnp.int32, (DN_TILE, DN_TILE), 1)
    same = jnp.where((ri >> levels) == (ci >> levels), 1.0, 0.0)
    incl = jnp.where(ci <= ri, same, 0.0)
    strict = jnp.where(ci < ri, same, 0.0)
    eye = jnp.where(ci == ri, 1.0, 0.0)
    base_levels = min(INV_BASE_LEVELS, levels)
    merge_masks = [jnp.where((ri >> lv) == (ci >> lv), 1.0, 0.0) for lv in range(base_levels, levels + 1)]
    idx_b = jnp.full((DN_TILE, LANES), h, jnp.int32)
    idx_g = idx_b + DN_HEADS
    cw = cw_ref[...]
    nt = (((1,), (1,)), ((), ()))
    after = jnp.where(ci > ri, same, 0.0)
    prefix_suffix = jnp.concatenate([incl, after], axis=0).astype(BF16)

    def front(t):
        rows = slice(t * DN_TILE, (t + 1) * DN_TILE)
        x = x_ref[rows, :]
        if seg == DN_TILE:
            if t == 0:
                tile = step * tiles_per_step
                first = ((tile % tiles_per_seq) == 0).astype(F32)
                halo = halo_ref[...] * (1.0 - first)
            else:
                halo = x_ref[t * DN_TILE - SUBLANES:t * DN_TILE, :]
            xe = jnp.concatenate([halo, x], axis=0)

            def shifted(d):
                return pltpu.roll(xe, d, 0)[SUBLANES:, :]
        else:
            bx = halo_ref[rows, :]
            t_in = lax.broadcasted_iota(jnp.int32, x.shape, 0) & (seg - 1)

            def shifted(d):
                xs = pltpu.roll(x, d, 0)
                bs = pltpu.roll(bx, (d - seg) % DN_TILE, 0)
                return jnp.where(t_in >= d, xs, bs)
        y = _conv_silu(x, cw, shifted)
        qc, kc, v = y[:, 0:DK], y[:, DK:2 * DK], y[:, 2 * DK:3 * DK]
        q = qc * lax.rsqrt(jnp.sum(qc * qc, axis=-1, keepdims=True) + EPS) * (DK ** -0.5)
        k = kc * lax.rsqrt(jnp.sum(kc * kc, axis=-1, keepdims=True) + EPS)

        ba = ba_ref[rows, :]
        beta_all = _sigmoid(ba)
        xg = ba + dt_ref[...]
        softplus = jnp.maximum(xg, 0.0) + jnp.log(1.0 + jnp.exp(-jnp.abs(xg)))
        g_all = -jnp.exp(al_ref[...]) * softplus
        beta = jnp.take_along_axis(beta_all, idx_b, axis=1, mode="promise_in_bounds")
        g = jnp.take_along_axis(g_all, idx_g, axis=1, mode="promise_in_bounds")
        g_hi = g.astype(BF16)
        g_r = g - g_hi.astype(F32)
        g_mid = g_r.astype(BF16)
        g_lo = (g_r - g_mid.astype(F32)).astype(BF16)
        return dict(rows=rows, q=q, k=k, v=v, beta=beta, g3=(g_hi, g_mid, g_lo), kb=k.astype(BF16),
                    qb=q.astype(BF16))

    tiles = [front(t) for t in range(tiles_per_step)]

    def mm(x, y):
        return jnp.dot(x, y, preferred_element_type=F32)

    for piece in range(3):
        for ts in tiles:
            part = mm(prefix_suffix, ts["g3"][piece])
            ts["sums"] = part if piece == 0 else ts["sums"] + part
    for ts in tiles:
        ts["kk"] = lax.dot_general(ts["kb"], ts["kb"], nt, preferred_element_type=F32)
    for ts in tiles:
        ts["qk"] = lax.dot_general(ts["qb"], ts["kb"], nt, preferred_element_type=F32)
    for ts in tiles:
        cum = ts["sums"][:DN_TILE]
        ts["cum"] = cum
        ts["rem"] = ts["sums"][DN_TILE:]
        diff = cum - cum.T
        ts["decay"] = jnp.exp(jnp.where(incl > 0.0, diff, NEG))
        ts["a"] = strict * (ts["beta"] * ts["decay"] * ts["kk"])
        a0 = ts["a"] * merge_masks[0]
        ts["t_inv"] = eye - a0
        ts["pw"] = a0
    for _ in range(base_levels - 1):
        for ts in tiles:
            pwb = ts["pw"].astype(BF16)
            ts["pw"] = mm(pwb, pwb)
        for ts in tiles:
            ts["t_inv"] = ts["t_inv"] + mm(ts["t_inv"].astype(BF16), ts["pw"].astype(BF16))
    for lv in range(base_levels, levels):
        lmask = merge_masks[lv - base_levels + 1] - merge_masks[lv - base_levels]
        for ts in tiles:
            ts["tb"] = ts["t_inv"].astype(BF16)
            ts["tl"] = mm(ts["tb"], (ts["a"] * lmask).astype(BF16))
        for ts in tiles:
            ts["t_inv"] = ts["t_inv"] - mm(ts["tl"].astype(BF16), ts["tb"])
    for ts in tiles:
        ts["gam"] = jnp.exp(ts["cum"])
        ts["tb"] = ts["t_inv"].astype(BF16)
        ts["w"] = mm(ts["tb"], (ts["beta"] * ts["gam"] * ts["k"]).astype(BF16))
    for ts in tiles:
        ts["u"] = mm(ts["tb"], (ts["beta"] * ts["v"]).astype(BF16))
    segs = DN_TILE // seg
    for t, ts in enumerate(tiles):
        rows = ts["rows"]
        pk_ref[rows, PK_W * DK:(PK_W + 1) * DK] = ts["w"].astype(BF16)
        pk_ref[rows, PK_QD * DK:(PK_QD + 1) * DK] = (ts["q"] * ts["gam"]).astype(BF16)
        pk_ref[rows, PK_KD * DK:(PK_KD + 1) * DK] = (ts["k"] * jnp.exp(ts["rem"])).astype(BF16)
        pk_ref[rows, PK_QK * DK:(PK_QK + 1) * DK] = (ts["qk"] * ts["decay"]).astype(BF16)
        u_ref[rows, :] = ts["u"]
        g_end = jnp.exp(ts["cum"] + ts["rem"])
        for j in range(segs):
            ge_ref[t * segs + j] = g_end[j * seg:j * seg + 1, :]


def _delta_prep(proj, halo_src, conv_w_perm, al_row, dt_row, *, seg, t_len):
    n = proj.shape[0]
    tiles_per_step = 8 if n % (8 * DN_TILE) == 0 else 4
    tr = tiles_per_step * DN_TILE
    assert n % tr == 0
    steps = n // tr
    qkv_blk = C_QKV // HEAD_COLS
    segs = tr // seg
    if seg == DN_TILE:
        assert t_len % tr == 0
        sub_per_step = tr // SUBLANES
        halo_spec = pl.BlockSpec(
            (SUBLANES, HEAD_COLS), lambda s, h: (jnp.maximum(s * sub_per_step - 1, 0), qkv_blk + h))
    else:
        halo_spec = pl.BlockSpec((tr, HEAD_COLS), lambda s, h: (s, h))
    return pl.pallas_call(
        functools.partial(_delta_prep_kernel, seg=seg, tiles_per_step=tiles_per_step,
                          tiles_per_seq=max(t_len // DN_TILE, 1)),
        grid=(steps, DN_HEADS),
        in_specs=[
            pl.BlockSpec((tr, HEAD_COLS), lambda s, h: (s, qkv_blk + h)),
            halo_spec,
            pl.BlockSpec((tr, LANES), lambda s, h: (s, C_BA // LANES)),
            pl.BlockSpec((CONV_W, HEAD_COLS), lambda s, h: (0, h)),
            pl.BlockSpec((1, LANES), lambda s, h: (0, 0)),
            pl.BlockSpec((1, LANES), lambda s, h: (0, 0)),
        ],
        out_specs=[
            pl.BlockSpec((None, tr, 4 * DK), lambda s, h: (h, s, 0)),
            pl.BlockSpec((None, tr, DV), lambda s, h: (h, s, 0)),
            pl.BlockSpec((None, segs, 1, LANES), lambda s, h: (h, s, 0, 0)),
        ],
        out_shape=[
            jax.ShapeDtypeStruct((DN_HEADS, n, 4 * DK), BF16),
            jax.ShapeDtypeStruct((DN_HEADS, n, DV), F32),
            jax.ShapeDtypeStruct((DN_HEADS, n // seg, 1, LANES), F32),
        ],
        compiler_params=_params(("parallel", "arbitrary"), 32),
        name="delta_prep",
    )(proj, halo_src, proj, conv_w_perm, al_row, dt_row)


def _out_gate(o, z, dnw):
    return _rms(o, dnw) * (z * _sigmoid(z))


def _delta_scan_prompt_kernel(pk_ref, u_ref, ge_ref, z_ref, dnw_ref, o_ref, sout_ref, s_scr, *, hg):
    c = pl.program_id(2)

    @pl.when(c == 0)
    def _():
        s_scr[...] = jnp.zeros(s_scr.shape, F32)

    contract0 = (((0,), (0,)), ((), ()))
    heads = range(hg)
    res = []
    for j in heads:
        wq = jnp.concatenate([pk_ref[j, :, PK_W * DK:(PK_W + 1) * DK],
                              pk_ref[j, :, PK_QD * DK:(PK_QD + 1) * DK]], axis=0)
        res.append(jnp.dot(wq, s_scr[j].astype(BF16), preferred_element_type=F32))
    ub = [(u_ref[j] - res[j][:DN_TILE]).astype(BF16) for j in heads]
    o = [res[j][DN_TILE:] + jnp.dot(pk_ref[j, :, PK_QK * DK:(PK_QK + 1) * DK], ub[j],
                                    preferred_element_type=F32) for j in heads]
    upd = [lax.dot_general(pk_ref[j, :, PK_KD * DK:(PK_KD + 1) * DK], ub[j], contract0,
                           preferred_element_type=F32) for j in heads]
    for j in heads:
        s_scr[j] = ge_ref[j, 0] * s_scr[j] + upd[j]
        cols = slice(j * DV, (j + 1) * DV)
        o_ref[:, cols] = _out_gate(o[j], z_ref[:, cols], dnw_ref[...]).astype(o_ref.dtype)

    @pl.when(c == pl.num_programs(2) - 1)
    def _():
        sout_ref[...] = s_scr[...]


def _delta_scan_prompt(pk, u, ge, proj, dnw_row, batch, t_len):
    hg = 8
    nc = t_len // DN_TILE
    zblk = C_Z // (hg * DV)
    return pl.pallas_call(
        functools.partial(_delta_scan_prompt_kernel, hg=hg),
        grid=(batch, DN_HEADS // hg, nc),
        in_specs=[
            pl.BlockSpec((hg, DN_TILE, 4 * DK), lambda b, g, c: (g, b * nc + c, 0)),
            pl.BlockSpec((hg, DN_TILE, DV), lambda b, g, c: (g, b * nc + c, 0)),
            pl.BlockSpec((hg, 1, 1, LANES), lambda b, g, c: (g, b * nc + c, 0, 0)),
            pl.BlockSpec((DN_TILE, hg * DV), lambda b, g, c: (b * nc + c, zblk + g)),
            pl.BlockSpec((1, DV), lambda b, g, c: (0, 0)),
        ],
        out_specs=[
            pl.BlockSpec((DN_TILE, hg * DV), lambda b, g, c: (b * nc + c, g)),
            pl.BlockSpec((None, hg, DK, DV), lambda b, g, c: (b, g, 0, 0)),
        ],
        out_shape=[
            jax.ShapeDtypeStruct((batch * t_len, DN_V), BF16),
            jax.ShapeDtypeStruct((batch, DN_HEADS, DK, DV), F32),
        ],
        scratch_shapes=[pltpu.VMEM((hg, DK, DV), F32)],
        compiler_params=_params(("parallel", "parallel", "arbitrary"), 32),
        name="delta_scan_prompt",
    )(pk, u, ge, proj, dnw_row)


def _delta_scan_sample_kernel(pk_ref, u_ref, ge_ref, z_ref, dnw_ref, s0_ref, o_ref, sout_ref,
                              ub_scr, oq_scr, pk32_scr, *, hg, t_dec, seq_unroll):
    n_seq = DN_TILE // t_dec
    contract0 = (((0,), (0,)), ((), ()))
    for j in range(hg):
        pk32_scr[j] = pk_ref[j, :, 0:3 * DK].astype(F32)

    def group(gi, carry):
        pairs = []
        for a in range(seq_unroll):
            i = gi * seq_unroll + a
            r0 = pl.multiple_of(i * t_dec, t_dec)
            pairs += [(i, r0, j) for j in range(hg)]
        res = []
        for i, r0, j in pairs:
            wq = jnp.concatenate([pk32_scr[j, pl.ds(r0, t_dec), PK_W * DK:(PK_W + 1) * DK],
                                  pk32_scr[j, pl.ds(r0, t_dec), PK_QD * DK:(PK_QD + 1) * DK]], axis=0)
            res.append(jnp.dot(wq.astype(BF16), s0_ref[i, j].astype(BF16), preferred_element_type=F32))
        upd = []
        for (i, r0, j), r in zip(pairs, res):
            uu = u_ref[j, pl.ds(r0, t_dec), :] - r[:t_dec]
            ub_scr[j, pl.ds(r0, t_dec), :] = uu
            oq_scr[j, pl.ds(r0, t_dec), :] = r[t_dec:]
            kd = pk32_scr[j, pl.ds(r0, t_dec), PK_KD * DK:(PK_KD + 1) * DK]
            upd.append(lax.dot_general(kd.astype(BF16), uu.astype(BF16), contract0,
                                       preferred_element_type=F32))
        for (i, r0, j), up in zip(pairs, upd):
            sout_ref[i, j] = ge_ref[j, i] * s0_ref[i, j] + up
        return carry

    lax.fori_loop(0, n_seq // seq_unroll, group, 0)
    o = [oq_scr[j] + jnp.dot(pk_ref[j, :, PK_QK * DK:(PK_QK + 1) * DK], ub_scr[j].astype(BF16),
                             preferred_element_type=F32) for j in range(hg)]
    for j in range(hg):
        cols = slice(j * DV, (j + 1) * DV)
        o_ref[:, cols] = _out_gate(o[j], z_ref[:, cols], dnw_ref[...]).astype(o_ref.dtype)


def _delta_scan_sample(pk, u, ge, proj, dnw_row, s0, n_seq, t_dec):
    hg = 4
    seq_per_tile = DN_TILE // t_dec
    tiles = n_seq // seq_per_tile
    zblk = C_Z // (hg * DV)
    return pl.pallas_call(
        functools.partial(_delta_scan_sample_kernel, hg=hg, t_dec=t_dec, seq_unroll=4),
        grid=(tiles, DN_HEADS // hg),
        in_specs=[
            pl.BlockSpec((hg, DN_TILE, 4 * DK), lambda r, g: (g, r, 0)),
            pl.BlockSpec((hg, DN_TILE, DV), lambda r, g: (g, r, 0)),
            pl.BlockSpec((hg, seq_per_tile, 1, LANES), lambda r, g: (g, r, 0, 0)),
            pl.BlockSpec((DN_TILE, hg * DV), lambda r, g: (r, zblk + g)),
            pl.BlockSpec((1, DV), lambda r, g: (0, 0)),
            pl.BlockSpec((seq_per_tile, hg, DK, DV), lambda r, g: (r, g, 0, 0)),
        ],
        out_specs=[
            pl.BlockSpec((DN_TILE, hg * DV), lambda r, g: (r, g)),
            pl.BlockSpec((seq_per_tile, hg, DK, DV), lambda r, g: (r, g, 0, 0)),
        ],
        out_shape=[
            jax.ShapeDtypeStruct((n_seq * t_dec, DN_V), BF16),
            jax.ShapeDtypeStruct((n_seq, DN_HEADS, DK, DV), F32),
        ],
        scratch_shapes=[pltpu.VMEM((hg, DN_TILE, DV), F32), pltpu.VMEM((hg, DN_TILE, DV), F32),
                        pltpu.VMEM((hg, DN_TILE, 3 * DK), F32)],
        compiler_params=_params(("parallel", "parallel"), 48),
        name="delta_scan_sample",
    )(pk, u, ge, proj, dnw_row, s0)


def _merge_kernel(oa_ref, od_ref, wa_ref, wd_ref, ga_ref, gd_ref, o_ref):
    ya = jnp.dot(oa_ref[...], wa_ref[...], preferred_element_type=F32)
    yd = jnp.dot(od_ref[...], wd_ref[...], preferred_element_type=F32)
    o_ref[...] = (_sigmoid(ga_ref[...]) * ya + _sigmoid(gd_ref[...]) * yd).astype(o_ref.dtype)


def _merge(oa, od, wa, wd, proj):
    n = oa.shape[0]
    tm, tn = 512, 512
    return pl.pallas_call(
        _merge_kernel,
        grid=(n // tm, D_MODEL // tn),
        in_specs=[
            pl.BlockSpec((tm, Q_A), lambda i, j: (i, 0)),
            pl.BlockSpec((tm, DN_V), lambda i, j: (i, 0)),
            pl.BlockSpec((Q_A, tn), lambda i, j: (0, j)),
            pl.BlockSpec((DN_V, tn), lambda i, j: (0, j)),
            pl.BlockSpec((tm, tn), lambda i, j: (i, C_GA // tn + j)),
            pl.BlockSpec((tm, tn), lambda i, j: (i, C_GD // tn + j)),
        ],
        out_specs=pl.BlockSpec((tm, tn), lambda i, j: (i, j)),
        out_shape=jax.ShapeDtypeStruct((n, D_MODEL), BF16),
        compiler_params=_params(("parallel", "arbitrary"), 40),
        name="merge",
    )(oa, od, wa, wd, proj, proj)


def _out_norm_kernel(m_ref, w_ref, x_ref, g_ref, o_ref):
    t = jnp.dot(m_ref[...], w_ref[...], preferred_element_type=F32)
    o_ref[...] = x_ref[...] + _rms(t, g_ref[...])


def _out_norm(mixed, w_out, x2d, g_row):
    n = mixed.shape[0]
    tm = 512
    return pl.pallas_call(
        _out_norm_kernel,
        grid=(n // tm,),
        in_specs=[
            pl.BlockSpec((tm, D_MODEL), lambda i: (i, 0)),
            pl.BlockSpec((D_MODEL, D_MODEL), lambda i: (0, 0)),
            pl.BlockSpec((tm, D_MODEL), lambda i: (i, 0)),
            pl.BlockSpec((1, D_MODEL), lambda i: (0, 0)),
        ],
        out_specs=pl.BlockSpec((tm, D_MODEL), lambda i: (i, 0)),
        out_shape=jax.ShapeDtypeStruct((n, D_MODEL), F32),
        compiler_params=_params(("parallel",), 48),
        name="out_norm",
    )(mixed, w_out, x2d, g_row)


def _mlp_kernel(x_ref, gpre_ref, wu_ref, wd_ref, gpost_ref, o_ref, h_ref, acc_ref):
    f = pl.program_id(1)

    @pl.when(f == 0)
    def _():
        h_ref[...] = _rms(x_ref[...], gpre_ref[...]).astype(BF16)
        acc_ref[...] = jnp.zeros(acc_ref.shape, F32)

    up = jnp.dot(h_ref[...], wu_ref[...], preferred_element_type=F32)
    act = jnp.square(jnp.maximum(up, 0.0)).astype(BF16)
    acc_ref[...] += jnp.dot(act, wd_ref[...], preferred_element_type=F32)

    @pl.when(f == pl.num_programs(1) - 1)
    def _():
        o_ref[...] = x_ref[...] + _rms(acc_ref[...], gpost_ref[...])


def _mlp(x2d, gpre, w_up, w_down, gpost):
    n = x2d.shape[0]
    tm, tf = 512, 1024
    return pl.pallas_call(
        _mlp_kernel,
        grid=(n // tm, D_FF // tf),
        in_specs=[
            pl.BlockSpec((tm, D_MODEL), lambda i, f: (i, 0)),
            pl.BlockSpec((1, D_MODEL), lambda i, f: (0, 0)),
            pl.BlockSpec((D_MODEL, tf), lambda i, f: (0, f)),
            pl.BlockSpec((tf, D_MODEL), lambda i, f: (f, 0)),
            pl.BlockSpec((1, D_MODEL), lambda i, f: (0, 0)),
        ],
        out_specs=pl.BlockSpec((tm, D_MODEL), lambda i, f: (i, 0)),
        out_shape=jax.ShapeDtypeStruct((n, D_MODEL), F32),
        scratch_shapes=[pltpu.VMEM((tm, D_MODEL), BF16), pltpu.VMEM((tm, D_MODEL), F32)],
        compiler_params=_params(("parallel", "arbitrary"), 48),
        name="mlp",
    )(x2d, gpre, w_up, w_down, gpost)


def _head_major(x):
    lead = x.shape[:-1]
    return jnp.swapaxes(x.reshape(*lead, 3, DN_HEADS, DK), -3, -2).reshape(*lead, CONV_DIM)


def _part_major(x):
    lead = x.shape[:-1]
    return jnp.swapaxes(x.reshape(*lead, DN_HEADS, 3, DK), -3, -2).reshape(*lead, CONV_DIM)


W_QKV = Q_A + 2 * KV_A
W_Z = W_QKV + CONV_DIM
W_BA = W_Z + DN_V
W_GA = W_BA + 2 * DN_HEADS
W_DIM = W_GA + 2 * D_MODEL


def _pack_w_in_kernel(w_ref, o_ref):
    def cp(dst, src, width):
        o_ref[:, dst:dst + width] = w_ref[:, src:src + width].astype(BF16)

    cp(C_Q, 0, W_QKV)
    for h in range(DN_HEADS):
        for p in range(3):
            cp(C_QKV + h * HEAD_COLS + p * DK, W_QKV + p * DN_QK + h * DK, DK)
    cp(C_Z, W_Z, DN_V)
    cp(C_GA, W_GA, 2 * D_MODEL)
    o_ref[:, C_BA:P_DIM] = jnp.zeros((o_ref.shape[0], P_DIM - C_BA), BF16)
    cp(C_BA, W_BA, 2 * DN_HEADS)


def _pack_w_in(w_in):
    tr = 128
    return pl.pallas_call(
        _pack_w_in_kernel,
        grid=(D_MODEL // tr,),
        in_specs=[pl.BlockSpec((tr, W_DIM), lambda i: (i, 0))],
        out_specs=pl.BlockSpec((tr, P_DIM), lambda i: (i, 0)),
        out_shape=jax.ShapeDtypeStruct((D_MODEL, P_DIM), BF16),
        compiler_params=_params(("parallel",), 40),
        name="pack_w_in",
    )(w_in)


def _lane_row(vals, offset):
    return jnp.zeros((1, LANES), F32).at[0, offset:offset + vals.shape[0]].set(vals.astype(F32))


def _layer(x2d, attend, delta, w):
    proj = _in_proj(x2d, w["g_mix_pre"], w["w_in"])
    o_a = attend(proj)
    o_d, s_final = delta(proj)
    mixed = _merge(o_a, o_d, w["w_ba"], w["w_bd"], proj)
    x1 = _out_norm(mixed, w["w_out"], x2d, w["g_mix_post"])
    y = _mlp(x1, w["g_mlp_pre"], w["w_up"], w["w_down"], w["g_mlp_post"])
    return y, proj, s_final


def kernel(x_prompt, x_sample, cache_k, cache_v, state_delta, state_conv, page_table, w_in, conv_w, a_log,
           dt_bias, dn_norm_w, w_branch_attn, w_branch_delta, w_out, g_mix_pre, g_mix_post, g_mlp_pre,
           g_mlp_post, w_up, w_down):
    depth = w_in.shape[0]
    assert depth == 1
    b_p, t_p, _ = x_prompt.shape
    b_s, t_s, _ = x_sample.shape
    l = 0
    w = {
        "w_in": _pack_w_in(w_in[l]),
        "g_mix_pre": g_mix_pre[l][None, :],
        "g_mix_post": g_mix_post[l][None, :],
        "g_mlp_pre": g_mlp_pre[l][None, :],
        "g_mlp_post": g_mlp_post[l][None, :],
        "w_ba": w_branch_attn[l].astype(BF16),
        "w_bd": w_branch_delta[l].astype(BF16),
        "w_out": w_out[l].astype(BF16),
        "w_up": w_up[l].astype(BF16),
        "w_down": w_down[l].astype(BF16),
    }
    cw = _head_major(conv_w[l])
    al_row = _lane_row(a_log[l], DN_HEADS)
    dt_row = _lane_row(dt_bias[l], DN_HEADS)
    dnw_row = dn_norm_w[l][None, :]

    def attend_p(proj):
        kaug, vb, kmean = _kprep(proj, b_p, t_p)
        return _moba_prompt(proj, kaug, vb, kmean, b_p, t_p)

    def delta_p(proj):
        pk, u, ge = _delta_prep(proj, proj, cw, al_row, dt_row, seg=DN_TILE, t_len=t_p)
        return _delta_scan_prompt(pk, u, ge, proj, dnw_row, b_p, t_p)

    y_p, proj_p, d_p = _layer(x_prompt.reshape(b_p * t_p, D_MODEL), attend_p, delta_p, w)

    n_pool = cache_k.shape[1]
    ck2d = cache_k[l].reshape(n_pool * PAGE * N_KV, HD)
    cv2d = cache_v[l].reshape(n_pool * PAGE * N_KV, HD)
    conv_rows = _head_major(state_conv[l])
    halo_s = jnp.pad(conv_rows, ((0, 0), (t_s - (CONV_W - 1), 0), (0, 0))).reshape(b_s * t_s, CONV_DIM)

    def attend_s(proj):
        return _moba_sample(proj, ck2d, cv2d, page_table, b_s, t_s)

    def delta_s(proj):
        pk, u, ge = _delta_prep(proj, halo_s, cw, al_row, dt_row, seg=t_s, t_len=t_s)
        return _delta_scan_sample(pk, u, ge, proj, dnw_row, state_delta[l], b_s, t_s)

    y_s, proj_s, d_s = _layer(x_sample.reshape(b_s * t_s, D_MODEL), attend_s, delta_s, w)

    def kv_out(proj, b, t):
        k = proj[:, C_K:C_K + KV_A].reshape(1, b, t, N_KV, HD)
        v = proj[:, C_V:C_V + KV_A].reshape(1, b, t, N_KV, HD)
        return k, v

    def conv_out(proj, b, t):
        raw = proj.reshape(b, t, P_DIM)[:, t - (CONV_W - 1):, C_QKV:C_QKV + CONV_DIM]
        return _part_major(raw)[None]

    k_p, v_p = kv_out(proj_p, b_p, t_p)
    k_s, v_s = kv_out(proj_s, b_s, t_s)
    return (y_p.reshape(b_p, t_p, D_MODEL), y_s.reshape(b_s, t_s, D_MODEL), k_p, v_p, d_p[None],
            conv_out(proj_p, b_p, t_p), k_s, v_s, d_s[None], conv_out(proj_s, b_s, t_s))
```

```python
import functools
import math

import numpy as np
import jax
import jax.numpy as jnp
from jax import lax
from jax.experimental import pallas as pl
from jax.experimental.pallas import tpu as pltpu

F32 = jnp.float32
BF16 = jnp.bfloat16

D_MODEL = 2048
N_HEADS = 16
N_KV = 4
HD = 128
GROUP = N_HEADS // N_KV
MOBA_BLOCK = 256
MOBA_TOPK = 3
PAGE = 128
DN_HEADS = 16
DK = 128
DV = 128
CONV_W = 4
Q_A = N_HEADS * HD
KV_A = N_KV * HD
DN_QK = DN_HEADS * DK
DN_V = DN_HEADS * DV
CONV_DIM = 2 * DN_QK + DN_V
D_FF = 4 * D_MODEL
EPS = 1e-6

LANES = 128
SUBLANES = 8

C_Q = 0
C_K = C_Q + Q_A
C_V = C_K + KV_A
C_QKV = C_V + KV_A
C_Z = C_QKV + CONV_DIM
C_GA = C_Z + DN_V
C_GD = C_GA + D_MODEL
C_BA = C_GD + D_MODEL
P_DIM = C_BA + LANES
HEAD_COLS = 3 * DK

NEG = -1e30
SCALE = HD ** -0.5
EXP2_C = SCALE * math.log2(math.e)


def _params(sem, vmem_mib):
    return pltpu.CompilerParams(dimension_semantics=sem, vmem_limit_bytes=vmem_mib * 2**20)


def _rms(x, g):
    return x * lax.rsqrt(jnp.mean(x * x, axis=-1, keepdims=True) + EPS) * g


def _sigmoid(x):
    return 1.0 / (1.0 + jnp.exp(-x))


def _in_proj_kernel(x_ref, g_ref, w_ref, o_ref, h_ref):
    @pl.when(pl.program_id(1) == 0)
    def _():
        h_ref[...] = _rms(x_ref[...], g_ref[...]).astype(BF16)

    o_ref[...] = lax.dot_general(h_ref[...], w_ref[...], (((1,), (1,)), ((), ())), preferred_element_type=F32)


def _in_proj(x2d, g_row, wp):
    n = x2d.shape[0]
    tm = 1024 if n % 1024 == 0 else 512
    tn = 1408
    assert n % tm == 0 and P_DIM % tn == 0
    return pl.pallas_call(
        _in_proj_kernel,
        grid=(n // tm, P_DIM // tn),
        in_specs=[
            pl.BlockSpec((tm, D_MODEL), lambda i, j: (i, 0)),
            pl.BlockSpec((1, D_MODEL), lambda i, j: (0, 0)),
            pl.BlockSpec((tn, D_MODEL), lambda i, j: (j, 0)),
        ],
        out_specs=pl.BlockSpec((tm, tn), lambda i, j: (i, j)),
        out_shape=jax.ShapeDtypeStruct((n, P_DIM), F32),
        scratch_shapes=[pltpu.VMEM((tm, D_MODEL), BF16)],
        compiler_params=_params(("parallel", "arbitrary"), 52),
        name="in_proj",
    )(x2d, g_row, wp)


N_SLOPE_FEATS = 4
F_SEL = 0
F_HI = 16
F_LO = F_HI + N_SLOPE_FEATS
MAX_BLOCKS = 16


def _kprep_kernel(k_ref, v_ref, kaug_ref, vb_ref, kmean_ref, *, blocks_per_step):
    step = pl.program_id(1)
    lane = lax.broadcasted_iota(jnp.int32, (MOBA_BLOCK, LANES), 1)
    row = lax.broadcasted_iota(jnp.int32, (MOBA_BLOCK, LANES), 0).astype(F32)
    is_hi = jnp.where(lane >= F_HI, jnp.where(lane < F_LO, 1.0, 0.0), 0.0)
    is_lo = jnp.where(lane >= F_LO, jnp.where(lane < F_LO + N_SLOPE_FEATS, 1.0, 0.0), 0.0)
    ones_col = jnp.where(lane == 0, 1.0, 0.0).astype(BF16)
    for j in range(blocks_per_step):
        n = step * blocks_per_step + j
        rows = slice(j * MOBA_BLOCK, (j + 1) * MOBA_BLOCK)
        k = k_ref[rows, :]
        v = v_ref[rows, :]
        mean = jnp.mean(k, axis=0, keepdims=True)
        feat = (jnp.where(lane == n, 1.0, 0.0)
                + is_hi * (n * MOBA_BLOCK).astype(F32)
                + is_lo * row).astype(BF16)
        for c in range(N_KV):
            cols = slice(c * HD, (c + 1) * HD)
            kaug_ref[c, rows, 0:HD] = k[:, cols].astype(BF16)
            kaug_ref[c, rows, HD:2 * HD] = feat
            vb_ref[c, rows, 0:HD] = v[:, cols].astype(BF16)
            vb_ref[c, rows, HD:2 * HD] = ones_col
            kmean_ref[c, j:j + 1, :] = mean[:, cols]


def _kprep(proj, batch, t_len):
    nblk = t_len // MOBA_BLOCK
    bps = min(nblk, 8)
    assert nblk % bps == 0 and nblk <= MAX_BLOCKS
    steps = nblk // bps
    rows = bps * MOBA_BLOCK
    return pl.pallas_call(
        functools.partial(_kprep_kernel, blocks_per_step=bps),
        grid=(batch, steps),
        in_specs=[
            pl.BlockSpec((rows, KV_A), lambda b, s: (b * steps + s, C_K // KV_A)),
            pl.BlockSpec((rows, KV_A), lambda b, s: (b * steps + s, C_V // KV_A)),
        ],
        out_specs=[
            pl.BlockSpec((None, N_KV, rows, 2 * HD), lambda b, s: (b, 0, s, 0)),
            pl.BlockSpec((None, N_KV, rows, 2 * HD), lambda b, s: (b, 0, s, 0)),
            pl.BlockSpec((None, N_KV, bps, HD), lambda b, s: (b, 0, s, 0)),
        ],
        out_shape=[
            jax.ShapeDtypeStruct((batch, N_KV, t_len, 2 * HD), BF16),
            jax.ShapeDtypeStruct((batch, N_KV, t_len, 2 * HD), BF16),
            jax.ShapeDtypeStruct((batch, N_KV, nblk, HD), F32),
        ],
        compiler_params=_params(("parallel", "arbitrary"), 40),
        name="kprep",
    )(proj, proj)


def _slope_pieces():
    slopes = np.asarray(2.0 ** (-8.0 * np.arange(1, N_HEADS + 1) / N_HEADS), np.float32).astype(np.float64)
    x = slopes / SCALE
    pieces = []
    for _ in range(N_SLOPE_FEATS):
        p = x.astype(np.float32).astype(jnp.bfloat16).astype(np.float64)
        pieces.append(p)
        x = x - p
    return np.stack(pieces, axis=1)


def _slope_feature_table():
    pieces = _slope_pieces()
    tab = np.zeros((N_KV, 2 * N_SLOPE_FEATS, GROUP * MOBA_BLOCK), np.float32)
    for c in range(N_KV):
        for g in range(GROUP):
            cols = slice(g * MOBA_BLOCK, (g + 1) * MOBA_BLOCK)
            for f in range(N_SLOPE_FEATS):
                tab[c, f, cols] = pieces[c * GROUP + g, f]
                tab[c, N_SLOPE_FEATS + f, cols] = pieces[c * GROUP + g, f]
    return jnp.asarray(tab)


def _select_bias(gate_t, own):
    nblk = gate_t.shape[0]
    blk = lax.broadcasted_iota(jnp.int32, gate_t.shape, 0)
    past = blk < own
    gm = jnp.where(past, gate_t, -jnp.inf)
    rank = jnp.zeros(gate_t.shape, F32)
    for m in range(nblk):
        row = gm[m:m + 1, :]
        tie = jnp.where(blk > m, 1.0, 0.0)
        rank = rank + jnp.where(row > gm, 1.0, jnp.where(row == gm, tie, 0.0))
    keep_past = jnp.where(past, jnp.where(rank < MOBA_TOPK - 0.5, 0.0, NEG), NEG)
    return jnp.where(blk == own, 0.0, keep_past)


def _moba_prompt_kernel(q_ref, kmean_ref, sf_ref, kaug_ref, vb_ref, o_ref,
                        feat_t, qaug, m_s, acc_s, s_a, s_b, *, nblk):
    i = pl.program_id(2)
    rows = GROUP * MOBA_BLOCK
    q = q_ref[...]
    qs = jnp.concatenate([q[:, g * HD:(g + 1) * HD] for g in range(GROUP)], axis=0).astype(BF16)
    qaug[:, 0:HD] = qs
    gate_t = lax.dot_general(kmean_ref[...].astype(BF16), qs, (((1,), (1,)), ((), ())),
                             preferred_element_type=F32)
    feat_t[...] = jnp.zeros(feat_t.shape, F32)
    feat_t[F_SEL:F_SEL + nblk, :] = _select_bias(gate_t, i)
    feat_t[F_HI:F_HI + 2 * N_SLOPE_FEATS, :] = sf_ref[...]
    qaug[:, HD:2 * HD] = feat_t[...].T.astype(BF16)

    def raw_scores(n):
        start = pl.multiple_of(n * MOBA_BLOCK, MOBA_BLOCK)
        kn = kaug_ref[pl.ds(start, MOBA_BLOCK), :]
        return lax.dot_general(qaug[...], kn, (((1,), (1,)), ((), ())), preferred_element_type=F32)

    def values(n):
        return vb_ref[pl.ds(pl.multiple_of(n * MOBA_BLOCK, MOBA_BLOCK), MOBA_BLOCK), :]

    def probs(s, m):
        return jnp.concatenate([jnp.exp2((s[:, :LANES] - m) * EXP2_C),
                                jnp.exp2((s[:, LANES:] - m) * EXP2_C)], axis=1).astype(BF16)

    def row_max(s):
        cur = jnp.max(jnp.maximum(s[:, :LANES], s[:, LANES:]), axis=-1, keepdims=True)
        return jnp.broadcast_to(cur, (rows, LANES))

    s = raw_scores(i)
    s_a[...] = raw_scores(0)
    qi = lax.broadcasted_iota(jnp.int32, s.shape, 0) & (MOBA_BLOCK - 1)
    kj = lax.broadcasted_iota(jnp.int32, s.shape, 1)
    s = jnp.where(kj <= qi, s, NEG)
    m = row_max(s)
    m_s[...] = m
    acc_s[...] = jnp.dot(probs(s, m), values(i), preferred_element_type=F32)

    def step(n, cur, nxt):
        if nxt is not None:
            nxt[...] = raw_scores(jnp.minimum(n + 1, i - 1))
        s = cur[...]
        m_old = m_s[...]
        m_new = jnp.maximum(m_old, row_max(s))
        alpha = jnp.exp2((m_old - m_new) * EXP2_C)
        pv = jnp.dot(probs(s, m_new), values(n), preferred_element_type=F32)
        acc_s[...] = jnp.concatenate([alpha, alpha], axis=1) * acc_s[...] + pv
        m_s[...] = m_new

    def body(j, carry):
        step(2 * j, s_a, s_b)
        step(2 * j + 1, s_b, s_a)
        return carry

    lax.fori_loop(0, i // 2, body, 0)

    @pl.when(i % 2 == 1)
    def _():
        step(i - 1, s_a, None)

    acc = acc_s[...]
    out = acc[:, 0:HD] / acc[:, HD:HD + 1]
    for g in range(GROUP):
        o_ref[:, g * HD:(g + 1) * HD] = out[g * MOBA_BLOCK:(g + 1) * MOBA_BLOCK, :].astype(o_ref.dtype)


def _moba_prompt(proj, kaug, vb, kmean, batch, t_len):
    nq = t_len // MOBA_BLOCK
    nblk = nq
    rows = GROUP * MOBA_BLOCK
    qcols = GROUP * HD
    return pl.pallas_call(
        functools.partial(_moba_prompt_kernel, nblk=nblk),
        grid=(batch, N_KV, nq),
        in_specs=[
            pl.BlockSpec((MOBA_BLOCK, qcols), lambda b, c, i: (b * nq + i, C_Q // qcols + c)),
            pl.BlockSpec((None, None, nblk, HD), lambda b, c, i: (b, c, 0, 0)),
            pl.BlockSpec((None, 2 * N_SLOPE_FEATS, rows), lambda b, c, i: (c, 0, 0)),
            pl.BlockSpec((None, None, t_len, 2 * HD), lambda b, c, i: (b, c, 0, 0)),
            pl.BlockSpec((None, None, t_len, 2 * HD), lambda b, c, i: (b, c, 0, 0)),
        ],
        out_specs=pl.BlockSpec((MOBA_BLOCK, qcols), lambda b, c, i: (b * nq + i, c)),
        out_shape=jax.ShapeDtypeStruct((batch * t_len, Q_A), BF16),
        scratch_shapes=[
            pltpu.VMEM((LANES, rows), F32),
            pltpu.VMEM((rows, 2 * HD), BF16),
            pltpu.VMEM((rows, LANES), F32),
            pltpu.VMEM((rows, 2 * HD), F32),
            pltpu.VMEM((rows, MOBA_BLOCK), F32),
            pltpu.VMEM((rows, MOBA_BLOCK), F32),
        ],
        compiler_params=_params(("parallel", "parallel", "arbitrary"), 40),
        name="moba_prompt",
    )(proj, kmean, _slope_feature_table(), kaug, vb)


def _moba_sample_kernel(pt_ref, q_ref, kn_ref, vn_ref, slope_ref, *rest, n_pages, t_dec):
    del pt_ref
    k_pages = rest[:n_pages]
    v_pages = rest[n_pages:2 * n_pages]
    o_ref = rest[2 * n_pages]
    s_scr, kb_scr = rest[2 * n_pages + 1:]
    past = n_pages * PAGE
    nblk = past // MOBA_BLOCK
    pages_per_blk = MOBA_BLOCK // PAGE
    rows = N_HEADS * t_dec
    assert rows == LANES

    q = q_ref[...]
    q_rows = jnp.concatenate([q[:, h * HD:(h + 1) * HD] for h in range(N_HEADS)], axis=0)
    q_t = q_rows.T.astype(BF16)
    lane = lax.broadcasted_iota(jnp.int32, (HD, LANES), 1)
    rows_per_kv = GROUP * t_dec
    zero = jnp.zeros((HD, LANES), BF16)
    q_bd = [jnp.where(lane // rows_per_kv == c, q_t, zero) for c in range(N_KV)]

    def scores_t(k2d_rows):
        acc = None
        for c in range(N_KV):
            part = jnp.dot(k2d_rows(c), q_bd[c], preferred_element_type=F32)
            acc = part if acc is None else acc + part
        return acc

    for n in range(nblk):
        sums = [jnp.zeros((1, HD), F32) for _ in range(N_KV)]
        for pp in range(pages_per_blk):
            p = n * pages_per_blk + pp
            kc = []
            for c in range(N_KV):
                kf = k_pages[p][pl.ds(c, PAGE, stride=N_KV), :]
                sums[c] = sums[c] + jnp.sum(kf, axis=0, keepdims=True)
                kc.append(kf.astype(BF16))
            s_scr[p * PAGE:(p + 1) * PAGE, :] = scores_t(lambda c: kc[c])
        for c in range(N_KV):
            kb_scr[c, n:n + 1, :] = sums[c] * (1.0 / MOBA_BLOCK)
    gate_t = scores_t(lambda c: kb_scr[c].astype(BF16))
    sel = _select_bias(gate_t, nblk)

    slope = slope_ref[...]
    t_q = lax.broadcasted_iota(jnp.int32, (1, LANES), 1) % t_dec
    q_pos = (past + t_q).astype(F32)

    def logits(raw, k_pos):
        return raw * SCALE - slope * (q_pos - k_pos)

    kn = kn_ref[...]
    vn = vn_ref[...]
    knc = [kn[:, c * HD:(c + 1) * HD].astype(BF16) for c in range(N_KV)]
    t_k = lax.broadcasted_iota(jnp.int32, (t_dec, LANES), 0)
    s_own = logits(scores_t(lambda c: knc[c]), (past + t_k).astype(F32))
    s_own = jnp.where(t_k <= t_q, s_own, NEG)
    m = jnp.max(s_own, axis=0, keepdims=True)

    sub = lax.broadcasted_iota(jnp.int32, (PAGE, LANES), 0)
    for p in range(n_pages):
        n = p // pages_per_blk
        k_pos = (sub + p * PAGE).astype(F32)
        s = logits(s_scr[p * PAGE:(p + 1) * PAGE, :], k_pos) + sel[n:n + 1, :]
        s_scr[p * PAGE:(p + 1) * PAGE, :] = s
        m = jnp.maximum(m, jnp.max(s, axis=0, keepdims=True))

    p_own = jnp.exp(s_own - m)
    l = jnp.sum(p_own, axis=0, keepdims=True)
    row_kv = lax.broadcasted_iota(jnp.int32, (LANES, HD), 0) // rows_per_kv
    contract0 = (((0,), (0,)), ((), ()))
    p_own_b = p_own.astype(BF16)
    acc = jnp.zeros((LANES, HD), F32)
    for c in range(N_KV):
        part = lax.dot_general(p_own_b, vn[:, c * HD:(c + 1) * HD].astype(BF16), contract0,
                               preferred_element_type=F32)
        acc = acc + jnp.where(row_kv == c, part, 0.0)
    for p in range(n_pages):
        pr = jnp.exp(s_scr[p * PAGE:(p + 1) * PAGE, :] - m)
        l = l + jnp.sum(pr, axis=0, keepdims=True)
        pb = pr.astype(BF16)
        for c in range(N_KV):
            vf = v_pages[p][pl.ds(c, PAGE, stride=N_KV), :].astype(BF16)
            part = lax.dot_general(pb, vf, contract0, preferred_element_type=F32)
            acc = acc + jnp.where(row_kv == c, part, 0.0)
    l_col = jnp.broadcast_to(l, (LANES, LANES)).T
    out = acc / l_col
    for h in range(N_HEADS):
        o_ref[:, h * HD:(h + 1) * HD] = out[h * t_dec:(h + 1) * t_dec, :].astype(o_ref.dtype)


def _moba_sample(proj, cache_k2d, cache_v2d, page_table, n_seq, t_dec):
    n_pages = page_table.shape[1]
    past = n_pages * PAGE
    nblk = past // MOBA_BLOCK
    page_rows = PAGE * N_KV
    slopes = np.asarray(2.0 ** (-8.0 * np.arange(1, N_HEADS + 1) / N_HEADS), np.float32)
    slope_row = jnp.asarray(np.repeat(slopes, t_dec)[None, :])

    def page_spec(p):
        return pl.BlockSpec((page_rows, HD), lambda b, pt, p=p: (pt[b, p], 0))

    grid_spec = pltpu.PrefetchScalarGridSpec(
        num_scalar_prefetch=1,
        grid=(n_seq,),
        in_specs=[
            pl.BlockSpec((t_dec, Q_A), lambda b, pt: (b, C_Q // Q_A)),
            pl.BlockSpec((t_dec, KV_A), lambda b, pt: (b, C_K // KV_A)),
            pl.BlockSpec((t_dec, KV_A), lambda b, pt: (b, C_V // KV_A)),
            pl.BlockSpec((1, LANES), lambda b, pt: (0, 0)),
        ] + [page_spec(p) for p in range(n_pages)] * 2,
        out_specs=pl.BlockSpec((t_dec, Q_A), lambda b, pt: (b, 0)),
        scratch_shapes=[
            pltpu.VMEM((past, LANES), F32),
            pltpu.VMEM((N_KV, nblk, HD), F32),
        ],
    )
    return pl.pallas_call(
        functools.partial(_moba_sample_kernel, n_pages=n_pages, t_dec=t_dec),
        grid_spec=grid_spec,
        out_shape=jax.ShapeDtypeStruct((n_seq * t_dec, Q_A), BF16),
        compiler_params=_params(("arbitrary",), 40),
        name="moba_sample",
    )(page_table, proj, proj, proj, slope_row, *([cache_k2d] * n_pages), *([cache_v2d] * n_pages))


DN_TILE = 128
PK_W, PK_QD, PK_KD, PK_QK = 0, 1, 2, 3
INV_BASE_LEVELS = 3


def _conv_silu(x, w, shifted):
    y = x * w[CONV_W - 1:CONV_W, :]
    for d in range(1, CONV_W):
        y = y + shifted(d) * w[CONV_W - 1 - d:CONV_W - d, :]
    return y * _sigmoid(y)


def _delta_prep_kernel(x_ref, halo_ref, ba_ref, cw_ref, al_ref, dt_ref, pk_ref, u_ref, ge_ref,
                       gate_scr, xe_scr, *, seg, tiles_per_step, tiles_per_seq):
    step = pl.program_id(0)
    h = pl.program_id(1)
    levels = int(math.log2(seg))
    ri = lax.broadcasted_iota(jnp.int32, (DN_TILE, DN_TILE), 0)
    ci = lax.broadcasted_iota(jnp.int32, (DN_TILE, DN_TILE), 1)
    same = jnp.where((ri >> levels) == (ci >> levels), 1.0, 0.0)
    incl = jnp.where(ci <= ri, same, 0.0)
    strict = jnp.where(ci < ri, same, 0.0)
    eye = jnp.where(ci == ri, 1.0, 0.0)
    base_levels = min(INV_BASE_LEVELS, levels)
    merge_masks = [jnp.where((ri >> lv) == (ci >> lv), 1.0, 0.0) for lv in range(base_levels, levels + 1)]
    idx_b = jnp.full((DN_TILE, LANES), h, jnp.int32)
    idx_g = idx_b + DN_HEADS
    cw = cw_ref[...]
    nt = (((1,), (1,)), ((), ()))

    def mm(x, y):
        return jnp.dot(x, y, preferred_element_type=F32)

    @pl.when(h == 0)
    def _():
        after = jnp.where(ci > ri, same, 0.0)
        prefix_suffix = jnp.concatenate([incl, after], axis=0).astype(BF16)
        pieces = []
        for t in range(tiles_per_step):
            ba = ba_ref[t * DN_TILE:(t + 1) * DN_TILE, :]
            gate_scr[t, 0] = _sigmoid(ba)
            xg = ba + dt_ref[...]
            softplus = jnp.maximum(xg, 0.0) + jnp.log(1.0 + jnp.exp(-jnp.abs(xg)))
            g = -jnp.exp(al_ref[...]) * softplus
            g_hi = g.astype(BF16)
            g_r = g - g_hi.astype(F32)
            g_mid = g_r.astype(BF16)
            pieces.append((g_hi, g_mid, (g_r - g_mid.astype(F32)).astype(BF16)))
        sums = [mm(prefix_suffix, p[0]) for p in pieces]
        for piece in (1, 2):
            sums = [sm + mm(prefix_suffix, p[piece]) for sm, p in zip(sums, pieces)]
        for t in range(tiles_per_step):
            gate_scr[t, 1] = sums[t][:DN_TILE]
            gate_scr[t, 2] = sums[t][DN_TILE:]

    if seg == DN_TILE:
        tile0 = step * tiles_per_step
        first = ((tile0 % tiles_per_seq) == 0).astype(F32)
        xe_scr[0:SUBLANES, :] = halo_ref[...] * (1.0 - first)
        xe_scr[SUBLANES:, :] = x_ref[...]

    def front(t):
        rows = slice(t * DN_TILE, (t + 1) * DN_TILE)
        x = x_ref[rows, :]
        if seg == DN_TILE:
            def shifted(d):
                return xe_scr[pl.ds(SUBLANES + t * DN_TILE - d, DN_TILE), :]
        else:
            bx = halo_ref[rows, :]
            t_in = lax.broadcasted_iota(jnp.int32, x.shape, 0) & (seg - 1)

            def shifted(d):
                xs = pltpu.roll(x, d, 0)
                bs = pltpu.roll(bx, (d - seg) % DN_TILE, 0)
                return jnp.where(t_in >= d, xs, bs)
        y = _conv_silu(x, cw, shifted)
        qc, kc, v = y[:, 0:DK], y[:, DK:2 * DK], y[:, 2 * DK:3 * DK]
        q = qc * lax.rsqrt(jnp.sum(qc * qc, axis=-1, keepdims=True) + EPS) * (DK ** -0.5)
        k = kc * lax.rsqrt(jnp.sum(kc * kc, axis=-1, keepdims=True) + EPS)

        def head_lane(j, idx):
            return jnp.take_along_axis(gate_scr[t, j], idx, axis=1, mode="promise_in_bounds")

        return dict(rows=rows, q=q, k=k, v=v, beta=head_lane(0, idx_b), cum=head_lane(1, idx_g),
                    rem=head_lane(2, idx_g), kb=k.astype(BF16), qb=q.astype(BF16))

    tiles = [front(t) for t in range(tiles_per_step)]

    for ts in tiles:
        ts["kk"] = lax.dot_general(ts["kb"], ts["kb"], nt, preferred_element_type=F32)
    for ts in tiles:
        ts["qk"] = lax.dot_general(ts["qb"], ts["kb"], nt, preferred_element_type=F32)
    for ts in tiles:
        diff = ts["cum"] - ts["cum"].T
        ts["decay"] = jnp.exp(jnp.where(incl > 0.0, diff, NEG))
        ts["a"] = strict * (ts["beta"] * ts["decay"] * ts["kk"])
        a0 = ts["a"] * merge_masks[0]
        ts["t_inv"] = eye - a0
        ts["pw"] = a0
    for _ in range(base_levels - 1):
        for ts in tiles:
            pwb = ts["pw"].astype(BF16)
            ts["pw"] = mm(pwb, pwb)
        for ts in tiles:
            ts["t_inv"] = ts["t_inv"] + mm(ts["t_inv"].astype(BF16), ts["pw"].astype(BF16))
    for lv in range(base_levels, levels):
        lmask = merge_masks[lv - base_levels + 1] - merge_masks[lv - base_levels]
        for ts in tiles:
            ts["tb"] = ts["t_inv"].astype(BF16)
            ts["tl"] = mm(ts["tb"], (ts["a"] * lmask).astype(BF16))
        for ts in tiles:
            ts["t_inv"] = ts["t_inv"] - mm(ts["tl"].astype(BF16), ts["tb"])
    for ts in tiles:
        ts["gam"] = jnp.exp(ts["cum"])
        ts["tb"] = ts["t_inv"].astype(BF16)
        ts["w"] = mm(ts["tb"], (ts["beta"] * ts["gam"] * ts["k"]).astype(BF16))
    for ts in tiles:
        ts["u"] = mm(ts["tb"], (ts["beta"] * ts["v"]).astype(BF16))
    segs = DN_TILE // seg
    for t, ts in enumerate(tiles):
        rows = ts["rows"]
        pk_ref[rows, PK_W * DK:(PK_W + 1) * DK] = ts["w"].astype(BF16)
        pk_ref[rows, PK_QD * DK:(PK_QD + 1) * DK] = (ts["q"] * ts["gam"]).astype(BF16)
        pk_ref[rows, PK_KD * DK:(PK_KD + 1) * DK] = (ts["k"] * jnp.exp(ts["rem"])).astype(BF16)
        pk_ref[rows, PK_QK * DK:(PK_QK + 1) * DK] = (ts["qk"] * ts["decay"]).astype(BF16)
        u_ref[rows, :] = ts["u"]
        g_end = jnp.exp(ts["cum"] + ts["rem"])
        for j in range(segs):
            ge_ref[t * segs + j] = g_end[j * seg:j * seg + 1, :]


def _delta_prep(proj, halo_src, conv_w_perm, al_row, dt_row, *, seg, t_len):
    n = proj.shape[0]
    tiles_per_step = 8 if n % (8 * DN_TILE) == 0 else 4
    tr = tiles_per_step * DN_TILE
    assert n % tr == 0
    steps = n // tr
    qkv_blk = C_QKV // HEAD_COLS
    segs = tr // seg
    if seg == DN_TILE:
        assert t_len % tr == 0
        sub_per_step = tr // SUBLANES
        halo_spec = pl.BlockSpec(
            (SUBLANES, HEAD_COLS), lambda s, h: (jnp.maximum(s * sub_per_step - 1, 0), qkv_blk + h))
    else:
        halo_spec = pl.BlockSpec((tr, HEAD_COLS), lambda s, h: (s, h))
    return pl.pallas_call(
        functools.partial(_delta_prep_kernel, seg=seg, tiles_per_step=tiles_per_step,
                          tiles_per_seq=max(t_len // DN_TILE, 1)),
        grid=(steps, DN_HEADS),
        in_specs=[
            pl.BlockSpec((tr, HEAD_COLS), lambda s, h: (s, qkv_blk + h)),
            halo_spec,
            pl.BlockSpec((tr, LANES), lambda s, h: (s, C_BA // LANES)),
            pl.BlockSpec((CONV_W, HEAD_COLS), lambda s, h: (0, h)),
            pl.BlockSpec((1, LANES), lambda s, h: (0, 0)),
            pl.BlockSpec((1, LANES), lambda s, h: (0, 0)),
        ],
        out_specs=[
            pl.BlockSpec((None, tr, 4 * DK), lambda s, h: (h, s, 0)),
            pl.BlockSpec((None, tr, DV), lambda s, h: (h, s, 0)),
            pl.BlockSpec((None, segs, 1, LANES), lambda s, h: (h, s, 0, 0)),
        ],
        out_shape=[
            jax.ShapeDtypeStruct((DN_HEADS, n, 4 * DK), BF16),
            jax.ShapeDtypeStruct((DN_HEADS, n, DV), F32),
            jax.ShapeDtypeStruct((DN_HEADS, n // seg, 1, LANES), F32),
        ],
        scratch_shapes=[pltpu.VMEM((tiles_per_step, 3, DN_TILE, LANES), F32),
                        pltpu.VMEM((SUBLANES + tr, HEAD_COLS), F32)],
        compiler_params=_params(("parallel", "arbitrary"), 32),
        name="delta_prep",
    )(proj, halo_src, proj, conv_w_perm, al_row, dt_row)


def _out_gate(o, z, dnw):
    return _rms(o, dnw) * (z * _sigmoid(z))


def _delta_scan_prompt_kernel(pk_ref, u_ref, ge_ref, z_ref, dnw_ref, o_ref, sout_ref, s_scr, *, hg):
    c = pl.program_id(2)

    @pl.when(c == 0)
    def _():
        s_scr[...] = jnp.zeros(s_scr.shape, F32)

    contract0 = (((0,), (0,)), ((), ()))
    heads = range(hg)
    res = []
    for j in heads:
        wq = jnp.concatenate([pk_ref[j, :, PK_W * DK:(PK_W + 1) * DK],
                              pk_ref[j, :, PK_QD * DK:(PK_QD + 1) * DK]], axis=0)
        res.append(jnp.dot(wq, s_scr[j].astype(BF16), preferred_element_type=F32))
    ub = [(u_ref[j] - res[j][:DN_TILE]).astype(BF16) for j in heads]
    o = [res[j][DN_TILE:] + jnp.dot(pk_ref[j, :, PK_QK * DK:(PK_QK + 1) * DK], ub[j],
                                    preferred_element_type=F32) for j in heads]
    upd = [lax.dot_general(pk_ref[j, :, PK_KD * DK:(PK_KD + 1) * DK], ub[j], contract0,
                           preferred_element_type=F32) for j in heads]
    for j in heads:
        s_scr[j] = ge_ref[j, 0] * s_scr[j] + upd[j]
        cols = slice(j * DV, (j + 1) * DV)
        o_ref[:, cols] = _out_gate(o[j], z_ref[:, cols], dnw_ref[...]).astype(o_ref.dtype)

    @pl.when(c == pl.num_programs(2) - 1)
    def _():
        sout_ref[...] = s_scr[...]


def _delta_scan_prompt(pk, u, ge, proj, dnw_row, batch, t_len):
    hg = 8
    nc = t_len // DN_TILE
    zblk = C_Z // (hg * DV)
    return pl.pallas_call(
        functools.partial(_delta_scan_prompt_kernel, hg=hg),
        grid=(batch, DN_HEADS // hg, nc),
        in_specs=[
            pl.BlockSpec((hg, DN_TILE, 4 * DK), lambda b, g, c: (g, b * nc + c, 0)),
            pl.BlockSpec((hg, DN_TILE, DV), lambda b, g, c: (g, b * nc + c, 0)),
            pl.BlockSpec((hg, 1, 1, LANES), lambda b, g, c: (g, b * nc + c, 0, 0)),
            pl.BlockSpec((DN_TILE, hg * DV), lambda b, g, c: (b * nc + c, zblk + g)),
            pl.BlockSpec((1, DV), lambda b, g, c: (0, 0)),
        ],
        out_specs=[
            pl.BlockSpec((DN_TILE, hg * DV), lambda b, g, c: (b * nc + c, g)),
            pl.BlockSpec((None, hg, DK, DV), lambda b, g, c: (b, g, 0, 0)),
        ],
        out_shape=[
            jax.ShapeDtypeStruct((batch * t_len, DN_V), BF16),
            jax.ShapeDtypeStruct((batch, DN_HEADS, DK, DV), F32),
        ],
        scratch_shapes=[pltpu.VMEM((hg, DK, DV), F32)],
        compiler_params=_params(("parallel", "parallel", "arbitrary"), 32),
        name="delta_scan_prompt",
    )(pk, u, ge, proj, dnw_row)


def _delta_scan_sample_kernel(pk_ref, u_ref, ge_ref, z_ref, dnw_ref, s0_ref, o_ref, sout_ref,
                              ub_scr, oq_scr, pk32_scr, *, hg, t_dec, seq_unroll):
    n_seq = DN_TILE // t_dec
    contract0 = (((0,), (0,)), ((), ()))
    for j in range(hg):
        pk32_scr[j] = pk_ref[j, :, 0:3 * DK].astype(F32)

    def group(gi, carry):
        pairs = []
        for a in range(seq_unroll):
            i = gi * seq_unroll + a
            r0 = pl.multiple_of(i * t_dec, t_dec)
            pairs += [(i, r0, j) for j in range(hg)]
        res = []
        for i, r0, j in pairs:
            wq = jnp.concatenate([pk32_scr[j, pl.ds(r0, t_dec), PK_W * DK:(PK_W + 1) * DK],
                                  pk32_scr[j, pl.ds(r0, t_dec), PK_QD * DK:(PK_QD + 1) * DK]], axis=0)
            res.append(jnp.dot(wq.astype(BF16), s0_ref[i, j].astype(BF16), preferred_element_type=F32))
        upd = []
        for (i, r0, j), r in zip(pairs, res):
            uu = u_ref[j, pl.ds(r0, t_dec), :] - r[:t_dec]
            ub_scr[j, pl.ds(r0, t_dec), :] = uu
            oq_scr[j, pl.ds(r0, t_dec), :] = r[t_dec:]
            kd = pk32_scr[j, pl.ds(r0, t_dec), PK_KD * DK:(PK_KD + 1) * DK]
            upd.append(lax.dot_general(kd.astype(BF16), uu.astype(BF16), contract0,
                                       preferred_element_type=F32))
        for (i, r0, j), up in zip(pairs, upd):
            sout_ref[i, j] = ge_ref[j, i] * s0_ref[i, j] + up
        return carry

    lax.fori_loop(0, n_seq // seq_unroll, group, 0)
    o = [oq_scr[j] + jnp.dot(pk_ref[j, :, PK_QK * DK:(PK_QK + 1) * DK], ub_scr[j].astype(BF16),
                             preferred_element_type=F32) for j in range(hg)]
    for j in range(hg):
        cols = slice(j * DV, (j + 1) * DV)
        o_ref[:, cols] = _out_gate(o[j], z_ref[:, cols], dnw_ref[...]).astype(o_ref.dtype)


def _delta_scan_sample(pk, u, ge, proj, dnw_row, s0, n_seq, t_dec):
    hg = 4
    seq_per_tile = DN_TILE // t_dec
    tiles = n_seq // seq_per_tile
    zblk = C_Z // (hg * DV)
    return pl.pallas_call(
        functools.partial(_delta_scan_sample_kernel, hg=hg, t_dec=t_dec, seq_unroll=4),
        grid=(tiles, DN_HEADS // hg),
        in_specs=[
            pl.BlockSpec((hg, DN_TILE, 4 * DK), lambda r, g: (g, r, 0)),
            pl.BlockSpec((hg, DN_TILE, DV), lambda r, g: (g, r, 0)),
            pl.BlockSpec((hg, seq_per_tile, 1, LANES), lambda r, g: (g, r, 0, 0)),
            pl.BlockSpec((DN_TILE, hg * DV), lambda r, g: (r, zblk + g)),
            pl.BlockSpec((1, DV), lambda r, g: (0, 0)),
            pl.BlockSpec((seq_per_tile, hg, DK, DV), lambda r, g: (r, g, 0, 0)),
        ],
        out_specs=[
            pl.BlockSpec((DN_TILE, hg * DV), lambda r, g: (r, g)),
            pl.BlockSpec((seq_per_tile, hg, DK, DV), lambda r, g: (r, g, 0, 0)),
        ],
        out_shape=[
            jax.ShapeDtypeStruct((n_seq * t_dec, DN_V), BF16),
            jax.ShapeDtypeStruct((n_seq, DN_HEADS, DK, DV), F32),
        ],
        scratch_shapes=[pltpu.VMEM((hg, DN_TILE, DV), F32), pltpu.VMEM((hg, DN_TILE, DV), F32),
                        pltpu.VMEM((hg, DN_TILE, 3 * DK), F32)],
        compiler_params=_params(("parallel", "parallel"), 48),
        name="delta_scan_sample",
    )(pk, u, ge, proj, dnw_row, s0)


def _merge_kernel(oa_ref, od_ref, wa_ref, wd_ref, ga_ref, gd_ref, o_ref):
    ya = jnp.dot(oa_ref[...], wa_ref[...], preferred_element_type=F32)
    yd = jnp.dot(od_ref[...], wd_ref[...], preferred_element_type=F32)
    o_ref[...] = (_sigmoid(ga_ref[...]) * ya + _sigmoid(gd_ref[...]) * yd).astype(o_ref.dtype)


def _merge(oa, od, wa, wd, proj):
    n = oa.shape[0]
    tm, tn = 512, 512
    return pl.pallas_call(
        _merge_kernel,
        grid=(n // tm, D_MODEL // tn),
        in_specs=[
            pl.BlockSpec((tm, Q_A), lambda i, j: (i, 0)),
            pl.BlockSpec((tm, DN_V), lambda i, j: (i, 0)),
            pl.BlockSpec((Q_A, tn), lambda i, j: (0, j)),
            pl.BlockSpec((DN_V, tn), lambda i, j: (0, j)),
            pl.BlockSpec((tm, tn), lambda i, j: (i, C_GA // tn + j)),
            pl.BlockSpec((tm, tn), lambda i, j: (i, C_GD // tn + j)),
        ],
        out_specs=pl.BlockSpec((tm, tn), lambda i, j: (i, j)),
        out_shape=jax.ShapeDtypeStruct((n, D_MODEL), BF16),
        compiler_params=_params(("parallel", "arbitrary"), 40),
        name="merge",
    )(oa, od, wa, wd, proj, proj)


def _out_norm_kernel(m_ref, w_ref, x_ref, g_ref, o_ref):
    t = jnp.dot(m_ref[...], w_ref[...], preferred_element_type=F32)
    o_ref[...] = x_ref[...] + _rms(t, g_ref[...])


def _out_norm(mixed, w_out, x2d, g_row):
    n = mixed.shape[0]
    tm = 512
    return pl.pallas_call(
        _out_norm_kernel,
        grid=(n // tm,),
        in_specs=[
            pl.BlockSpec((tm, D_MODEL), lambda i: (i, 0)),
            pl.BlockSpec((D_MODEL, D_MODEL), lambda i: (0, 0)),
            pl.BlockSpec((tm, D_MODEL), lambda i: (i, 0)),
            pl.BlockSpec((1, D_MODEL), lambda i: (0, 0)),
        ],
        out_specs=pl.BlockSpec((tm, D_MODEL), lambda i: (i, 0)),
        out_shape=jax.ShapeDtypeStruct((n, D_MODEL), F32),
        compiler_params=_params(("parallel",), 48),
        name="out_norm",
    )(mixed, w_out, x2d, g_row)


def _mlp_kernel(x_ref, gpre_ref, wu_ref, wd_ref, gpost_ref, o_ref, h_ref, acc_ref):
    f = pl.program_id(1)

    @pl.when(f == 0)
    def _():
        h_ref[...] = _rms(x_ref[...], gpre_ref[...]).astype(BF16)
        acc_ref[...] = jnp.zeros(acc_ref.shape, F32)

    up = jnp.dot(h_ref[...], wu_ref[...], preferred_element_type=F32)
    act = jnp.square(jnp.maximum(up, 0.0)).astype(BF16)
    acc_ref[...] += jnp.dot(act, wd_ref[...], preferred_element_type=F32)

    @pl.when(f == pl.num_programs(1) - 1)
    def _():
        o_ref[...] = x_ref[...] + _rms(acc_ref[...], gpost_ref[...])


def _mlp(x2d, gpre, w_up, w_down, gpost):
    n = x2d.shape[0]
    tm, tf = 512, 1024
    return pl.pallas_call(
        _mlp_kernel,
        grid=(n // tm, D_FF // tf),
        in_specs=[
            pl.BlockSpec((tm, D_MODEL), lambda i, f: (i, 0)),
            pl.BlockSpec((1, D_MODEL), lambda i, f: (0, 0)),
            pl.BlockSpec((D_MODEL, tf), lambda i, f: (0, f)),
            pl.BlockSpec((tf, D_MODEL), lambda i, f: (f, 0)),
            pl.BlockSpec((1, D_MODEL), lambda i, f: (0, 0)),
        ],
        out_specs=pl.BlockSpec((tm, D_MODEL), lambda i, f: (i, 0)),
        out_shape=jax.ShapeDtypeStruct((n, D_MODEL), F32),
        scratch_shapes=[pltpu.VMEM((tm, D_MODEL), BF16), pltpu.VMEM((tm, D_MODEL), F32)],
        compiler_params=_params(("parallel", "arbitrary"), 48),
        name="mlp",
    )(x2d, gpre, w_up, w_down, gpost)


def _head_major(x):
    lead = x.shape[:-1]
    return jnp.swapaxes(x.reshape(*lead, 3, DN_HEADS, DK), -3, -2).reshape(*lead, CONV_DIM)


def _part_major(x):
    lead = x.shape[:-1]
    return jnp.swapaxes(x.reshape(*lead, DN_HEADS, 3, DK), -3, -2).reshape(*lead, CONV_DIM)


W_QKV = Q_A + 2 * KV_A
W_Z = W_QKV + CONV_DIM
W_BA = W_Z + DN_V
W_GA = W_BA + 2 * DN_HEADS
W_DIM = W_GA + 2 * D_MODEL


def _pack_w_in_kernel(w_ref, o_ref):
    def cp(dst, src, n):
        o_ref[dst:dst + n, :] = w_ref[src:src + n, :].astype(BF16)

    cp(C_Q, 0, W_QKV)
    for h in range(DN_HEADS):
        for p in range(3):
            cp(C_QKV + h * HEAD_COLS + p * DK, W_QKV + p * DN_QK + h * DK, DK)
    cp(C_Z, W_Z, DN_V)
    cp(C_GA, W_GA, 2 * D_MODEL)
    o_ref[C_BA:P_DIM, :] = jnp.zeros((P_DIM - C_BA, o_ref.shape[1]), BF16)
    cp(C_BA, W_BA, 2 * DN_HEADS)


def _pack_w_in(w_in_t):
    tc = 256
    return pl.pallas_call(
        _pack_w_in_kernel,
        grid=(D_MODEL // tc,),
        in_specs=[pl.BlockSpec((W_DIM, tc), lambda i: (0, i))],
        out_specs=pl.BlockSpec((P_DIM, tc), lambda i: (0, i)),
        out_shape=jax.ShapeDtypeStruct((P_DIM, D_MODEL), BF16),
        compiler_params=_params(("parallel",), 52),
        name="pack_w_in",
    )(w_in_t)


def _lane_row(vals, offset):
    return jnp.zeros((1, LANES), F32).at[0, offset:offset + vals.shape[0]].set(vals.astype(F32))


def _layer(x2d, attend, delta, w):
    proj = _in_proj(x2d, w["g_mix_pre"], w["w_in"])
    o_a = attend(proj)
    o_d, s_final = delta(proj)
    mixed = _merge(o_a, o_d, w["w_ba"], w["w_bd"], proj)
    x1 = _out_norm(mixed, w["w_out"], x2d, w["g_mix_post"])
    y = _mlp(x1, w["g_mlp_pre"], w["w_up"], w["w_down"], w["g_mlp_post"])
    return y, proj, s_final


def kernel(x_prompt, x_sample, cache_k, cache_v, state_delta, state_conv, page_table, w_in, conv_w, a_log,
           dt_bias, dn_norm_w, w_branch_attn, w_branch_delta, w_out, g_mix_pre, g_mix_post, g_mlp_pre,
           g_mlp_post, w_up, w_down):
    depth = w_in.shape[0]
    assert depth == 1
    b_p, t_p, _ = x_prompt.shape
    b_s, t_s, _ = x_sample.shape
    l = 0
    w = {
        "w_in": _pack_w_in(jnp.transpose(w_in[l])),
        "g_mix_pre": g_mix_pre[l][None, :],
        "g_mix_post": g_mix_post[l][None, :],
        "g_mlp_pre": g_mlp_pre[l][None, :],
        "g_mlp_post": g_mlp_post[l][None, :],
        "w_ba": w_branch_attn[l].astype(BF16),
        "w_bd": w_branch_delta[l].astype(BF16),
        "w_out": w_out[l].astype(BF16),
        "w_up": w_up[l].astype(BF16),
        "w_down": w_down[l].astype(BF16),
    }
    cw = _head_major(conv_w[l])
    al_row = _lane_row(a_log[l], DN_HEADS)
    dt_row = _lane_row(dt_bias[l], DN_HEADS)
    dnw_row = dn_norm_w[l][None, :]

    def attend_p(proj):
        kaug, vb, kmean = _kprep(proj, b_p, t_p)
        return _moba_prompt(proj, kaug, vb, kmean, b_p, t_p)

    def delta_p(proj):
        pk, u, ge = _delta_prep(proj, proj, cw, al_row, dt_row, seg=DN_TILE, t_len=t_p)
        return _delta_scan_prompt(pk, u, ge, proj, dnw_row, b_p, t_p)

    y_p, proj_p, d_p = _layer(x_prompt.reshape(b_p * t_p, D_MODEL), attend_p, delta_p, w)

    n_pool = cache_k.shape[1]
    ck2d = cache_k[l].reshape(n_pool * PAGE * N_KV, HD)
    cv2d = cache_v[l].reshape(n_pool * PAGE * N_KV, HD)
    conv_rows = _head_major(state_conv[l])
    halo_s = jnp.pad(conv_rows, ((0, 0), (t_s - (CONV_W - 1), 0), (0, 0))).reshape(b_s * t_s, CONV_DIM)

    def attend_s(proj):
        return _moba_sample(proj, ck2d, cv2d, page_table, b_s, t_s)

    def delta_s(proj):
        pk, u, ge = _delta_prep(proj, halo_s, cw, al_row, dt_row, seg=t_s, t_len=t_s)
        return _delta_scan_sample(pk, u, ge, proj, dnw_row, state_delta[l], b_s, t_s)

    y_s, proj_s, d_s = _layer(x_sample.reshape(b_s * t_s, D_MODEL), attend_s, delta_s, w)

    def kv_out(proj, b, t):
        k = proj[:, C_K:C_K + KV_A].reshape(1, b, t, N_KV, HD)
        v = proj[:, C_V:C_V + KV_A].reshape(1, b, t, N_KV, HD)
        return k, v

    def conv_out(proj, b, t):
        raw = proj.reshape(b, t, P_DIM)[:, t - (CONV_W - 1):, C_QKV:C_QKV + CONV_DIM]
        return _part_major(raw)[None]

    k_p, v_p = kv_out(proj_p, b_p, t_p)
    k_s, v_s = kv_out(proj_s, b_s, t_s)
    return (y_p.reshape(b_p, t_p, D_MODEL), y_s.reshape(b_s, t_s, D_MODEL), k_p, v_p, d_p[None],
            conv_out(proj_p, b_p, t_p), k_s, v_s, d_s[None], conv_out(proj_s, b_s, t_s))
```

```python
import functools
import math

import numpy as np
import jax
import jax.numpy as jnp
from jax import lax
from jax.experimental import pallas as pl
from jax.experimental.pallas import tpu as pltpu

F32 = jnp.float32
BF16 = jnp.bfloat16

D_MODEL = 2048
N_HEADS = 16
N_KV = 4
HD = 128
GROUP = N_HEADS // N_KV
MOBA_BLOCK = 256
MOBA_TOPK = 3
PAGE = 128
DN_HEADS = 16
DK = 128
DV = 128
CONV_W = 4
Q_A = N_HEADS * HD
KV_A = N_KV * HD
DN_QK = DN_HEADS * DK
DN_V = DN_HEADS * DV
CONV_DIM = 2 * DN_QK + DN_V
D_FF = 4 * D_MODEL
EPS = 1e-6

LANES = 128
SUBLANES = 8

C_Q = 0
C_K = C_Q + Q_A
C_V = C_K + KV_A
C_QKV = C_V + KV_A
C_Z = C_QKV + CONV_DIM
C_GA = C_Z + DN_V
C_GD = C_GA + D_MODEL
C_BA = C_GD + D_MODEL
P_DIM = C_BA + LANES
HEAD_COLS = 3 * DK

NEG = -1e30
SCALE = HD ** -0.5
EXP2_C = SCALE * math.log2(math.e)


def _params(sem, vmem_mib):
    return pltpu.CompilerParams(dimension_semantics=sem, vmem_limit_bytes=vmem_mib * 2**20)


def _rms(x, g):
    return x * lax.rsqrt(jnp.mean(x * x, axis=-1, keepdims=True) + EPS) * g


def _sigmoid(x):
    return 1.0 / (1.0 + jnp.exp(-x))


def _in_proj_kernel(x_ref, g_ref, w_ref, o_ref, h_ref):
    @pl.when(pl.program_id(1) == 0)
    def _():
        h_ref[...] = _rms(x_ref[...], g_ref[...]).astype(BF16)

    o_ref[...] = lax.dot_general(h_ref[...], w_ref[...], (((1,), (1,)), ((), ())), preferred_element_type=F32)


def _in_proj(x2d, g_row, wp):
    n = x2d.shape[0]
    tm = 1024 if n % 1024 == 0 else 512
    tn = 1408
    assert n % tm == 0 and P_DIM % tn == 0
    return pl.pallas_call(
        _in_proj_kernel,
        grid=(n // tm, P_DIM // tn),
        in_specs=[
            pl.BlockSpec((tm, D_MODEL), lambda i, j: (i, 0)),
            pl.BlockSpec((1, D_MODEL), lambda i, j: (0, 0)),
            pl.BlockSpec((tn, D_MODEL), lambda i, j: (j, 0)),
        ],
        out_specs=pl.BlockSpec((tm, tn), lambda i, j: (i, j)),
        out_shape=jax.ShapeDtypeStruct((n, P_DIM), F32),
        scratch_shapes=[pltpu.VMEM((tm, D_MODEL), BF16)],
        compiler_params=_params(("parallel", "arbitrary"), 52),
        name="in_proj",
    )(x2d, g_row, wp)


N_SLOPE_FEATS = 4
F_SEL = 0
F_HI = 16
F_LO = F_HI + N_SLOPE_FEATS
MAX_BLOCKS = 16


def _kprep_kernel(k_ref, v_ref, kaug_ref, vb_ref, kmean_ref, k5_ref, v5_ref, *, blocks_per_step):
    step = pl.program_id(1)
    lane = lax.broadcasted_iota(jnp.int32, (MOBA_BLOCK, LANES), 1)
    row = lax.broadcasted_iota(jnp.int32, (MOBA_BLOCK, LANES), 0).astype(F32)
    is_hi = jnp.where(lane >= F_HI, jnp.where(lane < F_LO, 1.0, 0.0), 0.0)
    is_lo = jnp.where(lane >= F_LO, jnp.where(lane < F_LO + N_SLOPE_FEATS, 1.0, 0.0), 0.0)
    ones_col = jnp.where(lane == 0, 1.0, 0.0).astype(BF16)
    for j in range(blocks_per_step):
        n = step * blocks_per_step + j
        rows = slice(j * MOBA_BLOCK, (j + 1) * MOBA_BLOCK)
        k = k_ref[rows, :]
        v = v_ref[rows, :]
        mean = jnp.mean(k, axis=0, keepdims=True)
        feat = (jnp.where(lane == n, 1.0, 0.0)
                + is_hi * (n * MOBA_BLOCK).astype(F32)
                + is_lo * row).astype(BF16)
        for c in range(N_KV):
            cols = slice(c * HD, (c + 1) * HD)
            kaug_ref[c, rows, 0:HD] = k[:, cols].astype(BF16)
            kaug_ref[c, rows, HD:2 * HD] = feat
            vb_ref[c, rows, 0:HD] = v[:, cols].astype(BF16)
            vb_ref[c, rows, HD:2 * HD] = ones_col
            kmean_ref[c, j:j + 1, :] = mean[:, cols]
            k5_ref[rows, c, :] = k[:, cols]
            v5_ref[rows, c, :] = v[:, cols]


def _kprep(proj, batch, t_len):
    nblk = t_len // MOBA_BLOCK
    bps = min(nblk, 8)
    assert nblk % bps == 0 and nblk <= MAX_BLOCKS
    steps = nblk // bps
    rows = bps * MOBA_BLOCK
    return pl.pallas_call(
        functools.partial(_kprep_kernel, blocks_per_step=bps),
        grid=(batch, steps),
        in_specs=[
            pl.BlockSpec((rows, KV_A), lambda b, s: (b * steps + s, C_K // KV_A)),
            pl.BlockSpec((rows, KV_A), lambda b, s: (b * steps + s, C_V // KV_A)),
        ],
        out_specs=[
            pl.BlockSpec((None, N_KV, rows, 2 * HD), lambda b, s: (b, 0, s, 0)),
            pl.BlockSpec((None, N_KV, rows, 2 * HD), lambda b, s: (b, 0, s, 0)),
            pl.BlockSpec((None, N_KV, bps, HD), lambda b, s: (b, 0, s, 0)),
            pl.BlockSpec((None, rows, N_KV, HD), lambda b, s: (b, s, 0, 0)),
            pl.BlockSpec((None, rows, N_KV, HD), lambda b, s: (b, s, 0, 0)),
        ],
        out_shape=[
            jax.ShapeDtypeStruct((batch, N_KV, t_len, 2 * HD), BF16),
            jax.ShapeDtypeStruct((batch, N_KV, t_len, 2 * HD), BF16),
            jax.ShapeDtypeStruct((batch, N_KV, nblk, HD), F32),
            jax.ShapeDtypeStruct((batch, t_len, N_KV, HD), F32),
            jax.ShapeDtypeStruct((batch, t_len, N_KV, HD), F32),
        ],
        compiler_params=_params(("parallel", "arbitrary"), 56),
        name="kprep",
    )(proj, proj)


def _slope_pieces():
    slopes = np.asarray(2.0 ** (-8.0 * np.arange(1, N_HEADS + 1) / N_HEADS), np.float32).astype(np.float64)
    x = slopes / SCALE
    pieces = []
    for _ in range(N_SLOPE_FEATS):
        p = x.astype(np.float32).astype(jnp.bfloat16).astype(np.float64)
        pieces.append(p)
        x = x - p
    return np.stack(pieces, axis=1)


def _slope_feature_table():
    pieces = _slope_pieces()
    tab = np.zeros((N_KV, 2 * N_SLOPE_FEATS, GROUP * MOBA_BLOCK), np.float32)
    for c in range(N_KV):
        for g in range(GROUP):
            cols = slice(g * MOBA_BLOCK, (g + 1) * MOBA_BLOCK)
            for f in range(N_SLOPE_FEATS):
                tab[c, f, cols] = pieces[c * GROUP + g, f]
                tab[c, N_SLOPE_FEATS + f, cols] = pieces[c * GROUP + g, f]
    return jnp.asarray(tab)


def _select_bias(gate_t, own):
    nblk = gate_t.shape[0]
    blk = lax.broadcasted_iota(jnp.int32, gate_t.shape, 0)
    past = blk < own
    gm = jnp.where(past, gate_t, -jnp.inf)
    rank = jnp.zeros(gate_t.shape, F32)
    for m in range(nblk):
        row = gm[m:m + 1, :]
        tie = jnp.where(blk > m, 1.0, 0.0)
        rank = rank + jnp.where(row > gm, 1.0, jnp.where(row == gm, tie, 0.0))
    keep_past = jnp.where(past, jnp.where(rank < MOBA_TOPK - 0.5, 0.0, NEG), NEG)
    return jnp.where(blk == own, 0.0, keep_past)


Q_TILES = 2


def _moba_prompt_kernel(q_ref, kmean_ref, sf_ref, kaug_ref, vb_ref, o_ref,
                        feat_t, qaug, m_s, acc_s, s_a, s_b, *, nblk):
    i0 = pl.program_id(2) * Q_TILES
    tile = GROUP * MOBA_BLOCK
    rows = Q_TILES * tile
    q = q_ref[...]
    qs = jnp.concatenate([q[a * MOBA_BLOCK:(a + 1) * MOBA_BLOCK, g * HD:(g + 1) * HD]
                          for a in range(Q_TILES) for g in range(GROUP)], axis=0).astype(BF16)
    qaug[:, 0:HD] = qs
    gate_t = lax.dot_general(kmean_ref[...].astype(BF16), qs, (((1,), (1,)), ((), ())),
                             preferred_element_type=F32)
    own = i0 + lax.broadcasted_iota(jnp.int32, (1, rows), 1) // tile
    feat_t[...] = jnp.zeros(feat_t.shape, F32)
    feat_t[F_SEL:F_SEL + nblk, :] = _select_bias(gate_t, own)
    for a in range(Q_TILES):
        feat_t[F_HI:F_HI + 2 * N_SLOPE_FEATS, a * tile:(a + 1) * tile] = sf_ref[...]
    qaug[:, HD:2 * HD] = feat_t[...].T.astype(BF16)

    def key_rows(n):
        return pl.ds(pl.multiple_of(n * MOBA_BLOCK, MOBA_BLOCK), MOBA_BLOCK)

    def raw_scores(n, r0=0):
        return lax.dot_general(qaug[r0:, :], kaug_ref[key_rows(n), :], (((1,), (1,)), ((), ())),
                               preferred_element_type=F32)

    def probs(s, m):
        return jnp.concatenate([jnp.exp2((s[:, :LANES] - m) * EXP2_C),
                                jnp.exp2((s[:, LANES:] - m) * EXP2_C)], axis=1).astype(BF16)

    def row_max(s):
        cur = jnp.max(jnp.maximum(s[:, :LANES], s[:, LANES:]), axis=-1, keepdims=True)
        return jnp.broadcast_to(cur, (s.shape[0], LANES))

    def update(s, n, r0=0):
        m_old = m_s[r0:, :]
        m_new = jnp.maximum(m_old, row_max(s))
        alpha = jnp.exp2((m_old - m_new) * EXP2_C)
        pv = jnp.dot(probs(s, m_new), vb_ref[key_rows(n), :], preferred_element_type=F32)
        acc_s[r0:, :] = jnp.concatenate([alpha, alpha], axis=1) * acc_s[r0:, :] + pv
        m_s[r0:, :] = m_new

    def causal(s):
        qi = lax.broadcasted_iota(jnp.int32, s.shape, 0) & (MOBA_BLOCK - 1)
        kj = lax.broadcasted_iota(jnp.int32, s.shape, 1)
        return kj <= qi

    m_s[...] = jnp.full(m_s.shape, NEG, F32)
    acc_s[...] = jnp.zeros(acc_s.shape, F32)
    for a in reversed(range(Q_TILES)):
        s = raw_scores(i0 + a, a * tile)
        if a == 0:
            s_a[...] = raw_scores(0)
        own_rows = lax.broadcasted_iota(jnp.int32, s.shape, 0) < tile
        s = jnp.where(causal(s), s, jnp.where(own_rows, NEG, s))
        update(s, i0 + a, a * tile)

    def step(n, cur, nxt):
        nxt[...] = raw_scores(jnp.minimum(n + 1, i0 - 1))
        update(cur[...], n)

    def body(j, carry):
        step(2 * j, s_a, s_b)
        step(2 * j + 1, s_b, s_a)
        return carry

    lax.fori_loop(0, i0 // 2, body, 0)

    acc = acc_s[...]
    out = acc[:, 0:HD] / acc[:, HD:HD + 1]
    for a in range(Q_TILES):
        for g in range(GROUP):
            r = a * tile + g * MOBA_BLOCK
            o_ref[a * MOBA_BLOCK:(a + 1) * MOBA_BLOCK, g * HD:(g + 1) * HD] = (
                out[r:r + MOBA_BLOCK, :].astype(o_ref.dtype))


def _moba_prompt(proj, kaug, vb, kmean, batch, t_len):
    nblk = t_len // MOBA_BLOCK
    assert nblk % Q_TILES == 0 and Q_TILES % 2 == 0
    nq = nblk // Q_TILES
    tile = GROUP * MOBA_BLOCK
    rows = Q_TILES * tile
    qrows = Q_TILES * MOBA_BLOCK
    qcols = GROUP * HD
    return pl.pallas_call(
        functools.partial(_moba_prompt_kernel, nblk=nblk),
        grid=(batch, N_KV, nq),
        in_specs=[
            pl.BlockSpec((qrows, qcols), lambda b, c, i: (b * nq + i, C_Q // qcols + c)),
            pl.BlockSpec((None, None, nblk, HD), lambda b, c, i: (b, c, 0, 0)),
            pl.BlockSpec((None, 2 * N_SLOPE_FEATS, tile), lambda b, c, i: (c, 0, 0)),
            pl.BlockSpec((None, None, t_len, 2 * HD), lambda b, c, i: (b, c, 0, 0)),
            pl.BlockSpec((None, None, t_len, 2 * HD), lambda b, c, i: (b, c, 0, 0)),
        ],
        out_specs=pl.BlockSpec((qrows, qcols), lambda b, c, i: (b * nq + i, c)),
        out_shape=jax.ShapeDtypeStruct((batch * t_len, Q_A), BF16),
        scratch_shapes=[
            pltpu.VMEM((LANES, rows), F32),
            pltpu.VMEM((rows, 2 * HD), BF16),
            pltpu.VMEM((rows, LANES), F32),
            pltpu.VMEM((rows, 2 * HD), F32),
            pltpu.VMEM((rows, MOBA_BLOCK), F32),
            pltpu.VMEM((rows, MOBA_BLOCK), F32),
        ],
        compiler_params=_params(("parallel", "parallel", "arbitrary"), 40),
        name="moba_prompt",
    )(proj, kmean, _slope_feature_table(), kaug, vb)


def _moba_sample_kernel(pt_ref, q_ref, kn_ref, vn_ref, slope_ref, *rest, n_pages, t_dec):
    del pt_ref
    k_pages = rest[:n_pages]
    v_pages = rest[n_pages:2 * n_pages]
    o_ref = rest[2 * n_pages]
    s_scr, kb_scr = rest[2 * n_pages + 1:]
    past = n_pages * PAGE
    nblk = past // MOBA_BLOCK
    pages_per_blk = MOBA_BLOCK // PAGE
    rows = N_HEADS * t_dec
    assert rows == LANES

    q = q_ref[...]
    q_rows = jnp.concatenate([q[:, h * HD:(h + 1) * HD] for h in range(N_HEADS)], axis=0)
    q_t = q_rows.T.astype(BF16)
    lane = lax.broadcasted_iota(jnp.int32, (HD, LANES), 1)
    rows_per_kv = GROUP * t_dec
    zero = jnp.zeros((HD, LANES), BF16)
    q_bd = [jnp.where(lane // rows_per_kv == c, q_t, zero) for c in range(N_KV)]

    def scores_t(k2d_rows):
        acc = None
        for c in range(N_KV):
            part = jnp.dot(k2d_rows(c), q_bd[c], preferred_element_type=F32)
            acc = part if acc is None else acc + part
        return acc

    for n in range(nblk):
        sums = [jnp.zeros((1, HD), F32) for _ in range(N_KV)]
        for pp in range(pages_per_blk):
            p = n * pages_per_blk + pp
            kc = []
            for c in range(N_KV):
                kf = k_pages[p][pl.ds(c, PAGE, stride=N_KV), :]
                sums[c] = sums[c] + jnp.sum(kf, axis=0, keepdims=True)
                kc.append(kf.astype(BF16))
            s_scr[p * PAGE:(p + 1) * PAGE, :] = scores_t(lambda c: kc[c])
        for c in range(N_KV):
            kb_scr[c, n:n + 1, :] = sums[c] * (1.0 / MOBA_BLOCK)
    gate_t = scores_t(lambda c: kb_scr[c].astype(BF16))
    sel = _select_bias(gate_t, nblk)

    slope = slope_ref[...]
    t_q = lax.broadcasted_iota(jnp.int32, (1, LANES), 1) % t_dec
    q_pos = (past + t_q).astype(F32)

    def logits(raw, k_pos):
        return raw * SCALE - slope * (q_pos - k_pos)

    kn = kn_ref[...]
    vn = vn_ref[...]
    knc = [kn[:, c * HD:(c + 1) * HD].astype(BF16) for c in range(N_KV)]
    t_k = lax.broadcasted_iota(jnp.int32, (t_dec, LANES), 0)
    s_own = logits(scores_t(lambda c: knc[c]), (past + t_k).astype(F32))
    s_own = jnp.where(t_k <= t_q, s_own, NEG)
    m = jnp.max(s_own, axis=0, keepdims=True)

    sub = lax.broadcasted_iota(jnp.int32, (PAGE, LANES), 0)
    for p in range(n_pages):
        n = p // pages_per_blk
        k_pos = (sub + p * PAGE).astype(F32)
        s = logits(s_scr[p * PAGE:(p + 1) * PAGE, :], k_pos) + sel[n:n + 1, :]
        s_scr[p * PAGE:(p + 1) * PAGE, :] = s
        m = jnp.maximum(m, jnp.max(s, axis=0, keepdims=True))

    p_own = jnp.exp(s_own - m)
    l = jnp.sum(p_own, axis=0, keepdims=True)
    row_kv = lax.broadcasted_iota(jnp.int32, (LANES, HD), 0) // rows_per_kv
    contract0 = (((0,), (0,)), ((), ()))
    p_own_b = p_own.astype(BF16)
    acc = jnp.zeros((LANES, HD), F32)
    for c in range(N_KV):
        part = lax.dot_general(p_own_b, vn[:, c * HD:(c + 1) * HD].astype(BF16), contract0,
                               preferred_element_type=F32)
        acc = acc + jnp.where(row_kv == c, part, 0.0)
    for p in range(n_pages):
        pr = jnp.exp(s_scr[p * PAGE:(p + 1) * PAGE, :] - m)
        l = l + jnp.sum(pr, axis=0, keepdims=True)
        pb = pr.astype(BF16)
        for c in range(N_KV):
            vf = v_pages[p][pl.ds(c, PAGE, stride=N_KV), :].astype(BF16)
            part = lax.dot_general(pb, vf, contract0, preferred_element_type=F32)
            acc = acc + jnp.where(row_kv == c, part, 0.0)
    l_col = jnp.broadcast_to(l, (LANES, LANES)).T
    out = acc / l_col
    for h in range(N_HEADS):
        o_ref[:, h * HD:(h + 1) * HD] = out[h * t_dec:(h + 1) * t_dec, :].astype(o_ref.dtype)


def _moba_sample(proj, cache_k2d, cache_v2d, page_table, n_seq, t_dec):
    n_pages = page_table.shape[1]
    past = n_pages * PAGE
    nblk = past // MOBA_BLOCK
    page_rows = PAGE * N_KV
    slopes = np.asarray(2.0 ** (-8.0 * np.arange(1, N_HEADS + 1) / N_HEADS), np.float32)
    slope_row = jnp.asarray(np.repeat(slopes, t_dec)[None, :])

    def page_spec(p):
        return pl.BlockSpec((page_rows, HD), lambda b, pt, p=p: (pt[b, p], 0))

    grid_spec = pltpu.PrefetchScalarGridSpec(
        num_scalar_prefetch=1,
        grid=(n_seq,),
        in_specs=[
            pl.BlockSpec((t_dec, Q_A), lambda b, pt: (b, C_Q // Q_A)),
            pl.BlockSpec((t_dec, KV_A), lambda b, pt: (b, C_K // KV_A)),
            pl.BlockSpec((t_dec, KV_A), lambda b, pt: (b, C_V // KV_A)),
            pl.BlockSpec((1, LANES), lambda b, pt: (0, 0)),
        ] + [page_spec(p) for p in range(n_pages)] * 2,
        out_specs=pl.BlockSpec((t_dec, Q_A), lambda b, pt: (b, 0)),
        scratch_shapes=[
            pltpu.VMEM((past, LANES), F32),
            pltpu.VMEM((N_KV, nblk, HD), F32),
        ],
    )
    return pl.pallas_call(
        functools.partial(_moba_sample_kernel, n_pages=n_pages, t_dec=t_dec),
        grid_spec=grid_spec,
        out_shape=jax.ShapeDtypeStruct((n_seq * t_dec, Q_A), BF16),
        compiler_params=_params(("arbitrary",), 40),
        name="moba_sample",
    )(page_table, proj, proj, proj, slope_row, *([cache_k2d] * n_pages), *([cache_v2d] * n_pages))


DN_TILE = 128
PK_W, PK_QD, PK_KD, PK_QK = 0, 1, 2, 3
INV_BASE_LEVELS = 3


def _conv_silu(x, w, shifted):
    y = x * w[CONV_W - 1:CONV_W, :]
    for d in range(1, CONV_W):
        y = y + shifted(d) * w[CONV_W - 1 - d:CONV_W - d, :]
    return y * _sigmoid(y)


def _delta_prep_kernel(x_ref, halo_ref, ba_ref, cw_ref, al_ref, dt_ref, pk_ref, u_ref, ge_ref,
                       gate_scr, xe_scr, *, seg, tiles_per_step, tiles_per_seq):
    step = pl.program_id(0)
    h = pl.program_id(1)
    levels = int(math.log2(seg))
    ri = lax.broadcasted_iota(jnp.int32, (DN_TILE, DN_TILE), 0)
    ci = lax.broadcasted_iota(jnp.int32, (DN_TILE, DN_TILE), 1)
    same = jnp.where((ri >> levels) == (ci >> levels), 1.0, 0.0)
    incl = jnp.where(ci <= ri, same, 0.0)
    strict = jnp.where(ci < ri, same, 0.0)
    eye = jnp.where(ci == ri, 1.0, 0.0)
    base_levels = min(INV_BASE_LEVELS, levels)
    merge_masks = [jnp.where((ri >> lv) == (ci >> lv), 1.0, 0.0) for lv in range(base_levels, levels + 1)]
    idx_b = jnp.full((DN_TILE, LANES), h, jnp.int32)
    idx_g = idx_b + DN_HEADS
    cw = cw_ref[...]
    nt = (((1,), (1,)), ((), ()))

    def mm(x, y):
        return jnp.dot(x, y, preferred_element_type=F32)

    @pl.when(h == 0)
    def _():
        after = jnp.where(ci > ri, same, 0.0)
        prefix_suffix = jnp.concatenate([incl, after], axis=0).astype(BF16)
        pieces = []
        for t in range(tiles_per_step):
            ba = ba_ref[t * DN_TILE:(t + 1) * DN_TILE, :]
            gate_scr[t, 0] = _sigmoid(ba)
            xg = ba + dt_ref[...]
            softplus = jnp.maximum(xg, 0.0) + jnp.log(1.0 + jnp.exp(-jnp.abs(xg)))
            g = -jnp.exp(al_ref[...]) * softplus
            g_hi = g.astype(BF16)
            g_r = g - g_hi.astype(F32)
            g_mid = g_r.astype(BF16)
            pieces.append((g_hi, g_mid, (g_r - g_mid.astype(F32)).astype(BF16)))
        sums = [mm(prefix_suffix, p[0]) for p in pieces]
        for piece in (1, 2):
            sums = [sm + mm(prefix_suffix, p[piece]) for sm, p in zip(sums, pieces)]
        for t in range(tiles_per_step):
            gate_scr[t, 1] = sums[t][:DN_TILE]
            gate_scr[t, 2] = sums[t][DN_TILE:]

    if seg == DN_TILE:
        tile0 = step * tiles_per_step
        first = ((tile0 % tiles_per_seq) == 0).astype(F32)
        xe_scr[0:SUBLANES, :] = halo_ref[...] * (1.0 - first)
        xe_scr[SUBLANES:, :] = x_ref[...]

    def front(t):
        rows = slice(t * DN_TILE, (t + 1) * DN_TILE)
        x = x_ref[rows, :]
        if seg == DN_TILE:
            def shifted(d):
                return xe_scr[pl.ds(SUBLANES + t * DN_TILE - d, DN_TILE), :]
        else:
            bx = halo_ref[rows, :]
            t_in = lax.broadcasted_iota(jnp.int32, x.shape, 0) & (seg - 1)

            def shifted(d):
                xs = pltpu.roll(x, d, 0)
                bs = pltpu.roll(bx, (d - seg) % DN_TILE, 0)
                return jnp.where(t_in >= d, xs, bs)
        y = _conv_silu(x, cw, shifted)
        qc, kc, v = y[:, 0:DK], y[:, DK:2 * DK], y[:, 2 * DK:3 * DK]
        q = qc * lax.rsqrt(jnp.sum(qc * qc, axis=-1, keepdims=True) + EPS) * (DK ** -0.5)
        k = kc * lax.rsqrt(jnp.sum(kc * kc, axis=-1, keepdims=True) + EPS)

        def head_lane(j, idx):
            return jnp.take_along_axis(gate_scr[t, j], idx, axis=1, mode="promise_in_bounds")

        return dict(rows=rows, q=q, k=k, v=v, beta=head_lane(0, idx_b), cum=head_lane(1, idx_g),
                    rem=head_lane(2, idx_g), kb=k.astype(BF16), qb=q.astype(BF16))

    tiles = [front(t) for t in range(tiles_per_step)]

    for ts in tiles:
        ts["kk"] = lax.dot_general(ts["kb"], ts["kb"], nt, preferred_element_type=F32)
    for ts in tiles:
        ts["qk"] = lax.dot_general(ts["qb"], ts["kb"], nt, preferred_element_type=F32)
    for ts in tiles:
        diff = ts["cum"] - ts["cum"].T
        ts["decay"] = jnp.exp(jnp.where(incl > 0.0, diff, NEG))
        ts["a"] = strict * (ts["beta"] * ts["decay"] * ts["kk"])
        a0 = ts["a"] * merge_masks[0]
        ts["t_inv"] = eye - a0
        ts["pw"] = a0
    for _ in range(base_levels - 1):
        for ts in tiles:
            pwb = ts["pw"].astype(BF16)
            ts["pw"] = mm(pwb, pwb)
        for ts in tiles:
            ts["t_inv"] = ts["t_inv"] + mm(ts["t_inv"].astype(BF16), ts["pw"].astype(BF16))
    for lv in range(base_levels, levels):
        lmask = merge_masks[lv - base_levels + 1] - merge_masks[lv - base_levels]
        for ts in tiles:
            ts["tb"] = ts["t_inv"].astype(BF16)
            ts["tl"] = mm(ts["tb"], (ts["a"] * lmask).astype(BF16))
        for ts in tiles:
            ts["t_inv"] = ts["t_inv"] - mm(ts["tl"].astype(BF16), ts["tb"])
    for ts in tiles:
        ts["gam"] = jnp.exp(ts["cum"])
        ts["tb"] = ts["t_inv"].astype(BF16)
        ts["w"] = mm(ts["tb"], (ts["beta"] * ts["gam"] * ts["k"]).astype(BF16))
    for ts in tiles:
        ts["u"] = mm(ts["tb"], (ts["beta"] * ts["v"]).astype(BF16))
    segs = DN_TILE // seg
    for t, ts in enumerate(tiles):
        rows = ts["rows"]
        pk_ref[rows, PK_W * DK:(PK_W + 1) * DK] = ts["w"].astype(BF16)
        pk_ref[rows, PK_QD * DK:(PK_QD + 1) * DK] = (ts["q"] * ts["gam"]).astype(BF16)
        pk_ref[rows, PK_KD * DK:(PK_KD + 1) * DK] = (ts["k"] * jnp.exp(ts["rem"])).astype(BF16)
        pk_ref[rows, PK_QK * DK:(PK_QK + 1) * DK] = (ts["qk"] * ts["decay"]).astype(BF16)
        u_ref[rows, :] = ts["u"]
        g_end = jnp.exp(ts["cum"] + ts["rem"])
        for j in range(segs):
            ge_ref[t * segs + j] = g_end[j * seg:j * seg + 1, :]


def _delta_prep(proj, halo_src, conv_w_perm, al_row, dt_row, *, seg, t_len):
    n = proj.shape[0]
    tiles_per_step = 8 if n % (8 * DN_TILE) == 0 else 4
    tr = tiles_per_step * DN_TILE
    assert n % tr == 0
    steps = n // tr
    qkv_blk = C_QKV // HEAD_COLS
    segs = tr // seg
    if seg == DN_TILE:
        assert t_len % tr == 0
        sub_per_step = tr // SUBLANES
        halo_spec = pl.BlockSpec(
            (SUBLANES, HEAD_COLS), lambda s, h: (jnp.maximum(s * sub_per_step - 1, 0), qkv_blk + h))
    else:
        halo_spec = pl.BlockSpec((tr, HEAD_COLS), lambda s, h: (s, h))
    return pl.pallas_call(
        functools.partial(_delta_prep_kernel, seg=seg, tiles_per_step=tiles_per_step,
                          tiles_per_seq=max(t_len // DN_TILE, 1)),
        grid=(steps, DN_HEADS),
        in_specs=[
            pl.BlockSpec((tr, HEAD_COLS), lambda s, h: (s, qkv_blk + h)),
            halo_spec,
            pl.BlockSpec((tr, LANES), lambda s, h: (s, C_BA // LANES)),
            pl.BlockSpec((CONV_W, HEAD_COLS), lambda s, h: (0, h)),
            pl.BlockSpec((1, LANES), lambda s, h: (0, 0)),
            pl.BlockSpec((1, LANES), lambda s, h: (0, 0)),
        ],
        out_specs=[
            pl.BlockSpec((None, tr, 4 * DK), lambda s, h: (h, s, 0)),
            pl.BlockSpec((None, tr, DV), lambda s, h: (h, s, 0)),
            pl.BlockSpec((None, segs, 1, LANES), lambda s, h: (h, s, 0, 0)),
        ],
        out_shape=[
            jax.ShapeDtypeStruct((DN_HEADS, n, 4 * DK), BF16),
            jax.ShapeDtypeStruct((DN_HEADS, n, DV), F32),
            jax.ShapeDtypeStruct((DN_HEADS, n // seg, 1, LANES), F32),
        ],
        scratch_shapes=[pltpu.VMEM((tiles_per_step, 3, DN_TILE, LANES), F32),
                        pltpu.VMEM((SUBLANES + tr, HEAD_COLS), F32)],
        compiler_params=_params(("parallel", "arbitrary"), 32),
        name="delta_prep",
    )(proj, halo_src, proj, conv_w_perm, al_row, dt_row)


def _out_gate(o, z, dnw):
    return _rms(o, dnw) * (z * _sigmoid(z))


def _delta_scan_prompt_kernel(pk_ref, u_ref, ge_ref, z_ref, dnw_ref, o_ref, sout_ref, s_scr, *, hg):
    c = pl.program_id(2)

    @pl.when(c == 0)
    def _():
        s_scr[...] = jnp.zeros(s_scr.shape, F32)

    contract0 = (((0,), (0,)), ((), ()))
    heads = range(hg)
    res = []
    for j in heads:
        wq = jnp.concatenate([pk_ref[j, :, PK_W * DK:(PK_W + 1) * DK],
                              pk_ref[j, :, PK_QD * DK:(PK_QD + 1) * DK]], axis=0)
        res.append(jnp.dot(wq, s_scr[j].astype(BF16), preferred_element_type=F32))
    ub = [(u_ref[j] - res[j][:DN_TILE]).astype(BF16) for j in heads]
    o = [res[j][DN_TILE:] + jnp.dot(pk_ref[j, :, PK_QK * DK:(PK_QK + 1) * DK], ub[j],
                                    preferred_element_type=F32) for j in heads]
    upd = [lax.dot_general(pk_ref[j, :, PK_KD * DK:(PK_KD + 1) * DK], ub[j], contract0,
                           preferred_element_type=F32) for j in heads]
    for j in heads:
        s_scr[j] = ge_ref[j, 0] * s_scr[j] + upd[j]
        cols = slice(j * DV, (j + 1) * DV)
        o_ref[:, cols] = _out_gate(o[j], z_ref[:, cols], dnw_ref[...]).astype(o_ref.dtype)

    @pl.when(c == pl.num_programs(2) - 1)
    def _():
        sout_ref[...] = s_scr[...]


def _delta_scan_prompt(pk, u, ge, proj, dnw_row, batch, t_len):
    hg = 8
    assert C_Z % (hg * DV) == 0
    nc = t_len // DN_TILE
    zblk = C_Z // (hg * DV)
    return pl.pallas_call(
        functools.partial(_delta_scan_prompt_kernel, hg=hg),
        grid=(batch, DN_HEADS // hg, nc),
        in_specs=[
            pl.BlockSpec((hg, DN_TILE, 4 * DK), lambda b, g, c: (g, b * nc + c, 0)),
            pl.BlockSpec((hg, DN_TILE, DV), lambda b, g, c: (g, b * nc + c, 0)),
            pl.BlockSpec((hg, 1, 1, LANES), lambda b, g, c: (g, b * nc + c, 0, 0)),
            pl.BlockSpec((DN_TILE, hg * DV), lambda b, g, c: (b * nc + c, zblk + g)),
            pl.BlockSpec((1, DV), lambda b, g, c: (0, 0)),
        ],
        out_specs=[
            pl.BlockSpec((DN_TILE, hg * DV), lambda b, g, c: (b * nc + c, g)),
            pl.BlockSpec((None, hg, DK, DV), lambda b, g, c: (b, g, 0, 0)),
        ],
        out_shape=[
            jax.ShapeDtypeStruct((batch * t_len, DN_V), BF16),
            jax.ShapeDtypeStruct((batch, DN_HEADS, DK, DV), F32),
        ],
        scratch_shapes=[pltpu.VMEM((hg, DK, DV), F32)],
        compiler_params=_params(("parallel", "parallel", "arbitrary"), 32),
        name="delta_scan_prompt",
    )(pk, u, ge, proj, dnw_row)


def _delta_scan_sample_kernel(pk_ref, u_ref, ge_ref, z_ref, dnw_ref, s0_ref, o_ref, sout_ref,
                              ub_scr, oq_scr, pk32_scr, *, hg, t_dec, seq_unroll):
    n_seq = DN_TILE // t_dec
    contract0 = (((0,), (0,)), ((), ()))
    for j in range(hg):
        pk32_scr[j] = pk_ref[j, :, 0:3 * DK].astype(F32)

    def group(gi, carry):
        pairs = []
        for a in range(seq_unroll):
            i = gi * seq_unroll + a
            r0 = pl.multiple_of(i * t_dec, t_dec)
            pairs += [(i, r0, j) for j in range(hg)]
        res = []
        for i, r0, j in pairs:
            wq = jnp.concatenate([pk32_scr[j, pl.ds(r0, t_dec), PK_W * DK:(PK_W + 1) * DK],
                                  pk32_scr[j, pl.ds(r0, t_dec), PK_QD * DK:(PK_QD + 1) * DK]], axis=0)
            res.append(jnp.dot(wq.astype(BF16), s0_ref[i, j].astype(BF16), preferred_element_type=F32))
        upd = []
        for (i, r0, j), r in zip(pairs, res):
            uu = u_ref[j, pl.ds(r0, t_dec), :] - r[:t_dec]
            ub_scr[j, pl.ds(r0, t_dec), :] = uu
            oq_scr[j, pl.ds(r0, t_dec), :] = r[t_dec:]
            kd = pk32_scr[j, pl.ds(r0, t_dec), PK_KD * DK:(PK_KD + 1) * DK]
            upd.append(lax.dot_general(kd.astype(BF16), uu.astype(BF16), contract0,
                                       preferred_element_type=F32))
        for (i, r0, j), up in zip(pairs, upd):
            sout_ref[i, j] = ge_ref[j, i] * s0_ref[i, j] + up
        return carry

    lax.fori_loop(0, n_seq // seq_unroll, group, 0)
    o = [oq_scr[j] + jnp.dot(pk_ref[j, :, PK_QK * DK:(PK_QK + 1) * DK], ub_scr[j].astype(BF16),
                             preferred_element_type=F32) for j in range(hg)]
    for j in range(hg):
        cols = slice(j * DV, (j + 1) * DV)
        o_ref[:, cols] = _out_gate(o[j], z_ref[:, cols], dnw_ref[...]).astype(o_ref.dtype)


def _delta_scan_sample(pk, u, ge, proj, dnw_row, s0, n_seq, t_dec):
    hg = 4
    seq_per_tile = DN_TILE // t_dec
    tiles = n_seq // seq_per_tile
    zblk = C_Z // (hg * DV)
    return pl.pallas_call(
        functools.partial(_delta_scan_sample_kernel, hg=hg, t_dec=t_dec, seq_unroll=4),
        grid=(tiles, DN_HEADS // hg),
        in_specs=[
            pl.BlockSpec((hg, DN_TILE, 4 * DK), lambda r, g: (g, r, 0)),
            pl.BlockSpec((hg, DN_TILE, DV), lambda r, g: (g, r, 0)),
            pl.BlockSpec((hg, seq_per_tile, 1, LANES), lambda r, g: (g, r, 0, 0)),
            pl.BlockSpec((DN_TILE, hg * DV), lambda r, g: (r, zblk + g)),
            pl.BlockSpec((1, DV), lambda r, g: (0, 0)),
            pl.BlockSpec((seq_per_tile, hg, DK, DV), lambda r, g: (r, g, 0, 0)),
        ],
        out_specs=[
            pl.BlockSpec((DN_TILE, hg * DV), lambda r, g: (r, g)),
            pl.BlockSpec((seq_per_tile, hg, DK, DV), lambda r, g: (r, g, 0, 0)),
        ],
        out_shape=[
            jax.ShapeDtypeStruct((n_seq * t_dec, DN_V), BF16),
            jax.ShapeDtypeStruct((n_seq, DN_HEADS, DK, DV), F32),
        ],
        scratch_shapes=[pltpu.VMEM((hg, DN_TILE, DV), F32), pltpu.VMEM((hg, DN_TILE, DV), F32),
                        pltpu.VMEM((hg, DN_TILE, 3 * DK), F32)],
        compiler_params=_params(("parallel", "parallel"), 48),
        name="delta_scan_sample",
    )(pk, u, ge, proj, dnw_row, s0)


def _mix_out_kernel(oa_ref, od_ref, wa_ref, wd_ref, wo_ref, ga0_ref, ga1_ref, gd0_ref, gd1_ref, x_ref, g_ref,
                    o_ref):
    ya = jnp.dot(oa_ref[...], wa_ref[...], preferred_element_type=F32)
    yd = jnp.dot(od_ref[...], wd_ref[...], preferred_element_type=F32)
    half = D_MODEL // 2
    mixed = jnp.concatenate(
        [_sigmoid(ga0_ref[...]) * ya[:, :half] + _sigmoid(gd0_ref[...]) * yd[:, :half],
         _sigmoid(ga1_ref[...]) * ya[:, half:] + _sigmoid(gd1_ref[...]) * yd[:, half:]], axis=1)
    t = jnp.dot(mixed.astype(BF16), wo_ref[...], preferred_element_type=F32)
    o_ref[...] = x_ref[...] + _rms(t, g_ref[...])


def _mix_out(oa, od, wa, wd, wo, proj, x2d, g_row):
    n = oa.shape[0]
    tm = 256
    half = D_MODEL // 2

    def resident(shape):
        return pl.BlockSpec(shape, lambda i: (0, 0), pipeline_mode=pl.Buffered(1))

    def gate(col):
        return pl.BlockSpec((tm, half), lambda i, c=col // half: (i, c))

    return pl.pallas_call(
        _mix_out_kernel,
        grid=(n // tm,),
        in_specs=[
            pl.BlockSpec((tm, Q_A), lambda i: (i, 0)),
            pl.BlockSpec((tm, DN_V), lambda i: (i, 0)),
            resident((Q_A, D_MODEL)),
            resident((DN_V, D_MODEL)),
            resident((D_MODEL, D_MODEL)),
            gate(C_GA), gate(C_GA + half), gate(C_GD), gate(C_GD + half),
            pl.BlockSpec((tm, D_MODEL), lambda i: (i, 0)),
            pl.BlockSpec((1, D_MODEL), lambda i: (0, 0)),
        ],
        out_specs=pl.BlockSpec((tm, D_MODEL), lambda i: (i, 0)),
        out_shape=jax.ShapeDtypeStruct((n, D_MODEL), F32),
        compiler_params=_params(("parallel",), 56),
        name="mix_out",
    )(oa, od, wa, wd, wo, proj, proj, proj, proj, x2d, g_row)


def _mlp_kernel(x_ref, gpre_ref, wu_ref, wd_ref, gpost_ref, o_ref, h_ref, acc_ref):
    f = pl.program_id(1)

    @pl.when(f == 0)
    def _():
        h_ref[...] = _rms(x_ref[...], gpre_ref[...]).astype(BF16)
        acc_ref[...] = jnp.zeros(acc_ref.shape, F32)

    up = jnp.dot(h_ref[...], wu_ref[...], preferred_element_type=F32)
    act = jnp.square(jnp.maximum(up, 0.0)).astype(BF16)
    acc_ref[...] += jnp.dot(act, wd_ref[...], preferred_element_type=F32)

    @pl.when(f == pl.num_programs(1) - 1)
    def _():
        o_ref[...] = x_ref[...] + _rms(acc_ref[...], gpost_ref[...])


def _mlp(x2d, gpre, w_up, w_down, gpost):
    n = x2d.shape[0]
    tm, tf = 512, 1024
    return pl.pallas_call(
        _mlp_kernel,
        grid=(n // tm, D_FF // tf),
        in_specs=[
            pl.BlockSpec((tm, D_MODEL), lambda i, f: (i, 0)),
            pl.BlockSpec((1, D_MODEL), lambda i, f: (0, 0)),
            pl.BlockSpec((D_MODEL, tf), lambda i, f: (0, f)),
            pl.BlockSpec((tf, D_MODEL), lambda i, f: (f, 0)),
            pl.BlockSpec((1, D_MODEL), lambda i, f: (0, 0)),
        ],
        out_specs=pl.BlockSpec((tm, D_MODEL), lambda i, f: (i, 0)),
        out_shape=jax.ShapeDtypeStruct((n, D_MODEL), F32),
        scratch_shapes=[pltpu.VMEM((tm, D_MODEL), BF16), pltpu.VMEM((tm, D_MODEL), F32)],
        compiler_params=_params(("parallel", "arbitrary"), 48),
        name="mlp",
    )(x2d, gpre, w_up, w_down, gpost)


def _head_major(x):
    lead = x.shape[:-1]
    return jnp.swapaxes(x.reshape(*lead, 3, DN_HEADS, DK), -3, -2).reshape(*lead, CONV_DIM)


def _part_major(x):
    lead = x.shape[:-1]
    return jnp.swapaxes(x.reshape(*lead, DN_HEADS, 3, DK), -3, -2).reshape(*lead, CONV_DIM)


W_QKV = Q_A + 2 * KV_A
W_Z = W_QKV + CONV_DIM
W_BA = W_Z + DN_V
W_GA = W_BA + 2 * DN_HEADS
W_DIM = W_GA + 2 * D_MODEL


def _pack_w_in_kernel(w_ref, o_ref):
    def cp(dst, src, n):
        o_ref[dst:dst + n, :] = w_ref[src:src + n, :].astype(BF16)

    cp(C_Q, 0, W_QKV)
    for h in range(DN_HEADS):
        for p in range(3):
            cp(C_QKV + h * HEAD_COLS + p * DK, W_QKV + p * DN_QK + h * DK, DK)
    cp(C_Z, W_Z, DN_V)
    cp(C_GA, W_GA, 2 * D_MODEL)
    o_ref[C_BA:P_DIM, :] = jnp.zeros((P_DIM - C_BA, o_ref.shape[1]), BF16)
    cp(C_BA, W_BA, 2 * DN_HEADS)


def _pack_w_in(w_in_t):
    tc = 256
    return pl.pallas_call(
        _pack_w_in_kernel,
        grid=(D_MODEL // tc,),
        in_specs=[pl.BlockSpec((W_DIM, tc), lambda i: (0, i))],
        out_specs=pl.BlockSpec((P_DIM, tc), lambda i: (0, i)),
        out_shape=jax.ShapeDtypeStruct((P_DIM, D_MODEL), BF16),
        compiler_params=_params(("parallel",), 52),
        name="pack_w_in",
    )(w_in_t)


def _lane_row(vals, offset):
    return jnp.zeros((1, LANES), F32).at[0, offset:offset + vals.shape[0]].set(vals.astype(F32))


def _layer(x2d, attend, delta, w):
    proj = _in_proj(x2d, w["g_mix_pre"], w["w_in"])
    o_a = attend(proj)
    o_d, s_final = delta(proj)
    x1 = _mix_out(o_a, o_d, w["w_ba"], w["w_bd"], w["w_out"], proj, x2d, w["g_mix_post"])
    y = _mlp(x1, w["g_mlp_pre"], w["w_up"], w["w_down"], w["g_mlp_post"])
    return y, proj, s_final


def kernel(x_prompt, x_sample, cache_k, cache_v, state_delta, state_conv, page_table, w_in, conv_w, a_log,
           dt_bias, dn_norm_w, w_branch_attn, w_branch_delta, w_out, g_mix_pre, g_mix_post, g_mlp_pre,
           g_mlp_post, w_up, w_down):
    depth = w_in.shape[0]
    assert depth == 1
    b_p, t_p, _ = x_prompt.shape
    b_s, t_s, _ = x_sample.shape
    l = 0
    w = {
        "w_in": _pack_w_in(jnp.transpose(w_in[l])),
        "g_mix_pre": g_mix_pre[l][None, :],
        "g_mix_post": g_mix_post[l][None, :],
        "g_mlp_pre": g_mlp_pre[l][None, :],
        "g_mlp_post": g_mlp_post[l][None, :],
        "w_ba": w_branch_attn[l].astype(BF16),
        "w_bd": w_branch_delta[l].astype(BF16),
        "w_out": w_out[l].astype(BF16),
        "w_up": w_up[l].astype(BF16),
        "w_down": w_down[l].astype(BF16),
    }
    cw = _head_major(conv_w[l])
    al_row = _lane_row(a_log[l], DN_HEADS)
    dt_row = _lane_row(dt_bias[l], DN_HEADS)
    dnw_row = dn_norm_w[l][None, :]

    kv_prompt = []

    def attend_p(proj):
        kaug, vb, kmean, k5, v5 = _kprep(proj, b_p, t_p)
        kv_prompt.extend([k5[None], v5[None]])
        return _moba_prompt(proj, kaug, vb, kmean, b_p, t_p)

    def delta_p(proj):
        pk, u, ge = _delta_prep(proj, proj, cw, al_row, dt_row, seg=DN_TILE, t_len=t_p)
        return _delta_scan_prompt(pk, u, ge, proj, dnw_row, b_p, t_p)

    y_p, proj_p, d_p = _layer(x_prompt.reshape(b_p * t_p, D_MODEL), attend_p, delta_p, w)

    n_pool = cache_k.shape[1]
    ck2d = cache_k[l].reshape(n_pool * PAGE * N_KV, HD)
    cv2d = cache_v[l].reshape(n_pool * PAGE * N_KV, HD)
    conv_rows = _head_major(state_conv[l])
    halo_s = jnp.pad(conv_rows, ((0, 0), (t_s - (CONV_W - 1), 0), (0, 0))).reshape(b_s * t_s, CONV_DIM)

    def attend_s(proj):
        return _moba_sample(proj, ck2d, cv2d, page_table, b_s, t_s)

    def delta_s(proj):
        pk, u, ge = _delta_prep(proj, halo_s, cw, al_row, dt_row, seg=t_s, t_len=t_s)
        return _delta_scan_sample(pk, u, ge, proj, dnw_row, state_delta[l], b_s, t_s)

    y_s, proj_s, d_s = _layer(x_sample.reshape(b_s * t_s, D_MODEL), attend_s, delta_s, w)

    def kv_out(proj, b, t):
        k = proj[:, C_K:C_K + KV_A].reshape(1, b, t, N_KV, HD)
        v = proj[:, C_V:C_V + KV_A].reshape(1, b, t, N_KV, HD)
        return k, v

    def conv_out(proj, b, t):
        raw = proj.reshape(b, t, P_DIM)[:, t - (CONV_W - 1):, C_QKV:C_QKV + CONV_DIM]
        return _part_major(raw)[None]

    k_p, v_p = kv_prompt
    k_s, v_s = kv_out(proj_s, b_s, t_s)
    return (y_p.reshape(b_p, t_p, D_MODEL), y_s.reshape(b_s, t_s, D_MODEL), k_p, v_p, d_p[None],
            conv_out(proj_p, b_p, t_p), k_s, v_s, d_s[None], conv_out(proj_s, b_s, t_s))
```

```python
import functools
import math

import numpy as np
import jax
import jax.numpy as jnp
from jax import lax
from jax.experimental import pallas as pl
from jax.experimental.pallas import tpu as pltpu

F32 = jnp.float32
BF16 = jnp.bfloat16

D_MODEL = 2048
N_HEADS = 16
N_KV = 4
HD = 128
GROUP = N_HEADS // N_KV
MOBA_BLOCK = 256
MOBA_TOPK = 3
PAGE = 128
DN_HEADS = 16
DK = 128
DV = 128
CONV_W = 4
Q_A = N_HEADS * HD
KV_A = N_KV * HD
DN_QK = DN_HEADS * DK
DN_V = DN_HEADS * DV
CONV_DIM = 2 * DN_QK + DN_V
D_FF = 4 * D_MODEL
EPS = 1e-6

LANES = 128
SUBLANES = 8

C_Q = 0
C_K = C_Q + Q_A
C_V = C_K + KV_A
C_QKV = C_V + KV_A
C_Z = C_QKV + CONV_DIM
C_GA = C_Z + DN_V
C_GD = C_GA + D_MODEL
C_BA = C_GD + D_MODEL
P_DIM = C_BA + LANES
HEAD_COLS = 3 * DK

NEG = -1e30
SCALE = HD ** -0.5
EXP2_C = SCALE * math.log2(math.e)


def _params(sem, vmem_mib):
    return pltpu.CompilerParams(dimension_semantics=sem, vmem_limit_bytes=vmem_mib * 2**20)


def _rms(x, g):
    return x * lax.rsqrt(jnp.mean(x * x, axis=-1, keepdims=True) + EPS) * g


def _sigmoid(x):
    return 1.0 / (1.0 + jnp.exp(-x))


def _in_proj_kernel(x_ref, g_ref, w_ref, o_ref, h_ref):
    @pl.when(pl.program_id(1) == 0)
    def _():
        h_ref[...] = _rms(x_ref[...], g_ref[...]).astype(BF16)

    o_ref[...] = lax.dot_general(h_ref[...], w_ref[...], (((1,), (1,)), ((), ())), preferred_element_type=F32)


def _in_proj(x2d, g_row, wp):
    n = x2d.shape[0]
    tm = 1024 if n % 1024 == 0 else 512
    tn = 1408
    assert n % tm == 0 and P_DIM % tn == 0
    return pl.pallas_call(
        _in_proj_kernel,
        grid=(n // tm, P_DIM // tn),
        in_specs=[
            pl.BlockSpec((tm, D_MODEL), lambda i, j: (i, 0)),
            pl.BlockSpec((1, D_MODEL), lambda i, j: (0, 0)),
            pl.BlockSpec((tn, D_MODEL), lambda i, j: (j, 0)),
        ],
        out_specs=pl.BlockSpec((tm, tn), lambda i, j: (i, j)),
        out_shape=jax.ShapeDtypeStruct((n, P_DIM), F32),
        scratch_shapes=[pltpu.VMEM((tm, D_MODEL), BF16)],
        compiler_params=_params(("parallel", "arbitrary"), 52),
        name="in_proj",
    )(x2d, g_row, wp)


N_SLOPE_FEATS = 4
F_SEL = 0
F_HI = 16
F_LO = F_HI + N_SLOPE_FEATS
MAX_BLOCKS = 16


def _kprep_kernel(k_ref, v_ref, kaug_ref, vb_ref, kmean_ref, k5_ref, v5_ref, *, blocks_per_step):
    step = pl.program_id(1)
    lane = lax.broadcasted_iota(jnp.int32, (MOBA_BLOCK, LANES), 1)
    row = lax.broadcasted_iota(jnp.int32, (MOBA_BLOCK, LANES), 0).astype(F32)
    is_hi = jnp.where(lane >= F_HI, jnp.where(lane < F_LO, 1.0, 0.0), 0.0)
    is_lo = jnp.where(lane >= F_LO, jnp.where(lane < F_LO + N_SLOPE_FEATS, 1.0, 0.0), 0.0)
    ones_col = jnp.where(lane == 0, 1.0, 0.0).astype(BF16)
    for j in range(blocks_per_step):
        n = step * blocks_per_step + j
        rows = slice(j * MOBA_BLOCK, (j + 1) * MOBA_BLOCK)
        k = k_ref[rows, :]
        v = v_ref[rows, :]
        mean = jnp.mean(k, axis=0, keepdims=True)
        feat = (jnp.where(lane == n, 1.0, 0.0)
                + is_hi * (n * MOBA_BLOCK).astype(F32)
                + is_lo * row).astype(BF16)
        for c in range(N_KV):
            cols = slice(c * HD, (c + 1) * HD)
            kaug_ref[c, rows, 0:HD] = k[:, cols].astype(BF16)
            kaug_ref[c, rows, HD:2 * HD] = feat
            vb_ref[c, rows, 0:HD] = v[:, cols].astype(BF16)
            vb_ref[c, rows, HD:2 * HD] = ones_col
            kmean_ref[c, j:j + 1, :] = mean[:, cols]
            k5_ref[rows, c, :] = k[:, cols]
            v5_ref[rows, c, :] = v[:, cols]


def _kprep(proj, batch, t_len):
    nblk = t_len // MOBA_BLOCK
    bps = min(nblk, 8)
    assert nblk % bps == 0 and nblk <= MAX_BLOCKS
    steps = nblk // bps
    rows = bps * MOBA_BLOCK
    return pl.pallas_call(
        functools.partial(_kprep_kernel, blocks_per_step=bps),
        grid=(batch, steps),
        in_specs=[
            pl.BlockSpec((rows, KV_A), lambda b, s: (b * steps + s, C_K // KV_A)),
            pl.BlockSpec((rows, KV_A), lambda b, s: (b * steps + s, C_V // KV_A)),
        ],
        out_specs=[
            pl.BlockSpec((None, N_KV, rows, 2 * HD), lambda b, s: (b, 0, s, 0)),
            pl.BlockSpec((None, N_KV, rows, 2 * HD), lambda b, s: (b, 0, s, 0)),
            pl.BlockSpec((None, N_KV, bps, HD), lambda b, s: (b, 0, s, 0)),
            pl.BlockSpec((None, rows, N_KV, HD), lambda b, s: (b, s, 0, 0)),
            pl.BlockSpec((None, rows, N_KV, HD), lambda b, s: (b, s, 0, 0)),
        ],
        out_shape=[
            jax.ShapeDtypeStruct((batch, N_KV, t_len, 2 * HD), BF16),
            jax.ShapeDtypeStruct((batch, N_KV, t_len, 2 * HD), BF16),
            jax.ShapeDtypeStruct((batch, N_KV, nblk, HD), F32),
            jax.ShapeDtypeStruct((batch, t_len, N_KV, HD), F32),
            jax.ShapeDtypeStruct((batch, t_len, N_KV, HD), F32),
        ],
        compiler_params=_params(("parallel", "arbitrary"), 56),
        name="kprep",
    )(proj, proj)


def _slope_pieces():
    slopes = np.asarray(2.0 ** (-8.0 * np.arange(1, N_HEADS + 1) / N_HEADS), np.float32).astype(np.float64)
    x = slopes / SCALE
    pieces = []
    for _ in range(N_SLOPE_FEATS):
        p = x.astype(np.float32).astype(jnp.bfloat16).astype(np.float64)
        pieces.append(p)
        x = x - p
    return np.stack(pieces, axis=1)


def _slope_feature_table():
    pieces = _slope_pieces()
    tab = np.zeros((N_KV, 2 * N_SLOPE_FEATS, GROUP * MOBA_BLOCK), np.float32)
    for c in range(N_KV):
        for g in range(GROUP):
            cols = slice(g * MOBA_BLOCK, (g + 1) * MOBA_BLOCK)
            for f in range(N_SLOPE_FEATS):
                tab[c, f, cols] = pieces[c * GROUP + g, f]
                tab[c, N_SLOPE_FEATS + f, cols] = pieces[c * GROUP + g, f]
    return jnp.asarray(tab)


def _select_bias(gate_t, own):
    nblk = gate_t.shape[0]
    blk = lax.broadcasted_iota(jnp.int32, gate_t.shape, 0)
    past = blk < own
    gm = jnp.where(past, gate_t, -jnp.inf)
    rank = jnp.zeros(gate_t.shape, F32)
    for m in range(nblk):
        row = gm[m:m + 1, :]
        tie = jnp.where(blk > m, 1.0, 0.0)
        rank = rank + jnp.where(row > gm, 1.0, jnp.where(row == gm, tie, 0.0))
    keep_past = jnp.where(past, jnp.where(rank < MOBA_TOPK - 0.5, 0.0, NEG), NEG)
    return jnp.where(blk == own, 0.0, keep_past)


Q_TILES = 2


def _moba_prompt_kernel(q_ref, kmean_ref, sf_ref, kaug_ref, vb_ref, o_ref,
                        feat_t, qaug, m_s, acc_s, s_a, s_b, s_c, s_d, *, nblk):
    i0 = pl.program_id(2) * Q_TILES
    tile = GROUP * MOBA_BLOCK
    rows = Q_TILES * tile
    q = q_ref[...]
    qs = jnp.concatenate([q[a * MOBA_BLOCK:(a + 1) * MOBA_BLOCK, g * HD:(g + 1) * HD]
                          for a in range(Q_TILES) for g in range(GROUP)], axis=0).astype(BF16)
    qaug[:, 0:HD] = qs
    gate_t = lax.dot_general(kmean_ref[...].astype(BF16), qs, (((1,), (1,)), ((), ())),
                             preferred_element_type=F32)
    own = i0 + lax.broadcasted_iota(jnp.int32, (1, rows), 1) // tile
    feat_t[...] = jnp.zeros(feat_t.shape, F32)
    feat_t[F_SEL:F_SEL + nblk, :] = _select_bias(gate_t, own)
    for a in range(Q_TILES):
        feat_t[F_HI:F_HI + 2 * N_SLOPE_FEATS, a * tile:(a + 1) * tile] = sf_ref[...]
    qaug[:, HD:2 * HD] = feat_t[...].T.astype(BF16)

    def key_rows(n):
        return pl.ds(pl.multiple_of(n * MOBA_BLOCK, MOBA_BLOCK), MOBA_BLOCK)

    def raw_scores(n, r0=0):
        return lax.dot_general(qaug[r0:, :], kaug_ref[key_rows(n), :], (((1,), (1,)), ((), ())),
                               preferred_element_type=F32)

    def pair_scores(pr, bufs):
        n0 = jnp.maximum(jnp.minimum(2 * pr, i0 - 2), 0)
        bufs[0][...] = raw_scores(n0)
        bufs[1][...] = raw_scores(n0 + 1)

    def probs(s, m):
        return jnp.concatenate([jnp.exp2((s[:, :LANES] - m) * EXP2_C),
                                jnp.exp2((s[:, LANES:] - m) * EXP2_C)], axis=1).astype(BF16)

    def row_max(s):
        cur = jnp.max(jnp.maximum(s[:, :LANES], s[:, LANES:]), axis=-1, keepdims=True)
        return jnp.broadcast_to(cur, (s.shape[0], LANES))

    def update(s, n, r0=0):
        m_old = m_s[r0:, :]
        m_new = jnp.maximum(m_old, row_max(s))
        alpha = jnp.exp2((m_old - m_new) * EXP2_C)
        pv = jnp.dot(probs(s, m_new), vb_ref[key_rows(n), :], preferred_element_type=F32)
        acc_s[r0:, :] = jnp.concatenate([alpha, alpha], axis=1) * acc_s[r0:, :] + pv
        m_s[r0:, :] = m_new

    def causal(s):
        qi = lax.broadcasted_iota(jnp.int32, s.shape, 0) & (MOBA_BLOCK - 1)
        kj = lax.broadcasted_iota(jnp.int32, s.shape, 1)
        return kj <= qi

    m_s[...] = jnp.full(m_s.shape, NEG, F32)
    acc_s[...] = jnp.zeros(acc_s.shape, F32)
    for a in reversed(range(Q_TILES)):
        s = raw_scores(i0 + a, a * tile)
        if a == 0:
            pair_scores(0, (s_a, s_b))
        own_rows = lax.broadcasted_iota(jnp.int32, s.shape, 0) < tile
        s = jnp.where(causal(s), s, jnp.where(own_rows, NEG, s))
        update(s, i0 + a, a * tile)

    n_pairs = i0 // 2

    def pair_update(pr, bufs):
        s0 = bufs[0][...]
        s1 = bufs[1][...]
        m_old = m_s[...]
        m_new = jnp.maximum(jnp.maximum(m_old, row_max(s0)), row_max(s1))
        alpha = jnp.exp2((m_old - m_new) * EXP2_C)
        p = jnp.concatenate([probs(s0, m_new), probs(s1, m_new)], axis=1)
        v2 = vb_ref[pl.ds(pl.multiple_of(pr * 2 * MOBA_BLOCK, 2 * MOBA_BLOCK), 2 * MOBA_BLOCK), :]
        pv = jnp.dot(p, v2, preferred_element_type=F32)
        acc_s[...] = jnp.concatenate([alpha, alpha], axis=1) * acc_s[...] + pv
        m_s[...] = m_new

    def body(j, carry):
        pr = 2 * j
        pair_scores(pr + 1, (s_c, s_d))
        pair_update(pr, (s_a, s_b))
        pair_scores(pr + 2, (s_a, s_b))
        pair_update(pr + 1, (s_c, s_d))
        return carry

    lax.fori_loop(0, n_pairs // 2, body, 0)

    @pl.when(n_pairs % 2 == 1)
    def _():
        pair_update(n_pairs - 1, (s_a, s_b))

    acc = acc_s[...]
    out = acc[:, 0:HD] / acc[:, HD:HD + 1]
    for a in range(Q_TILES):
        for g in range(GROUP):
            r = a * tile + g * MOBA_BLOCK
            o_ref[a * MOBA_BLOCK:(a + 1) * MOBA_BLOCK, g * HD:(g + 1) * HD] = (
                out[r:r + MOBA_BLOCK, :].astype(o_ref.dtype))


def _moba_prompt(proj, kaug, vb, kmean, batch, t_len):
    nblk = t_len // MOBA_BLOCK
    assert nblk % Q_TILES == 0 and Q_TILES % 2 == 0
    nq = nblk // Q_TILES
    tile = GROUP * MOBA_BLOCK
    rows = Q_TILES * tile
    qrows = Q_TILES * MOBA_BLOCK
    qcols = GROUP * HD
    return pl.pallas_call(
        functools.partial(_moba_prompt_kernel, nblk=nblk),
        grid=(batch, N_KV, nq),
        in_specs=[
            pl.BlockSpec((qrows, qcols), lambda b, c, i: (b * nq + i, C_Q // qcols + c)),
            pl.BlockSpec((None, None, nblk, HD), lambda b, c, i: (b, c, 0, 0)),
            pl.BlockSpec((None, 2 * N_SLOPE_FEATS, tile), lambda b, c, i: (c, 0, 0)),
            pl.BlockSpec((None, None, t_len, 2 * HD), lambda b, c, i: (b, c, 0, 0)),
            pl.BlockSpec((None, None, t_len, 2 * HD), lambda b, c, i: (b, c, 0, 0)),
        ],
        out_specs=pl.BlockSpec((qrows, qcols), lambda b, c, i: (b * nq + i, c)),
        out_shape=jax.ShapeDtypeStruct((batch * t_len, Q_A), BF16),
        scratch_shapes=[
            pltpu.VMEM((LANES, rows), F32),
            pltpu.VMEM((rows, 2 * HD), BF16),
            pltpu.VMEM((rows, LANES), F32),
            pltpu.VMEM((rows, 2 * HD), F32),
            pltpu.VMEM((rows, MOBA_BLOCK), F32),
            pltpu.VMEM((rows, MOBA_BLOCK), F32),
            pltpu.VMEM((rows, MOBA_BLOCK), F32),
            pltpu.VMEM((rows, MOBA_BLOCK), F32),
        ],
        compiler_params=_params(("parallel", "parallel", "arbitrary"), 40),
        name="moba_prompt",
    )(proj, kmean, _slope_feature_table(), kaug, vb)


def _moba_sample_kernel(pt_ref, q_ref, kn_ref, vn_ref, slope_ref, *rest, n_pages, t_dec):
    del pt_ref
    k_pages = rest[:n_pages]
    v_pages = rest[n_pages:2 * n_pages]
    o_ref = rest[2 * n_pages]
    s_scr, kb_scr = rest[2 * n_pages + 1:]
    past = n_pages * PAGE
    nblk = past // MOBA_BLOCK
    pages_per_blk = MOBA_BLOCK // PAGE
    rows = N_HEADS * t_dec
    assert rows == LANES

    q = q_ref[...]
    q_rows = jnp.concatenate([q[:, h * HD:(h + 1) * HD] for h in range(N_HEADS)], axis=0)
    q_t = q_rows.T.astype(BF16)
    lane = lax.broadcasted_iota(jnp.int32, (HD, LANES), 1)
    rows_per_kv = GROUP * t_dec
    zero = jnp.zeros((HD, LANES), BF16)
    q_bd = [jnp.where(lane // rows_per_kv == c, q_t, zero) for c in range(N_KV)]

    def scores_t(k2d_rows):
        acc = None
        for c in range(N_KV):
            part = jnp.dot(k2d_rows(c), q_bd[c], preferred_element_type=F32)
            acc = part if acc is None else acc + part
        return acc

    for n in range(nblk):
        sums = [jnp.zeros((1, HD), F32) for _ in range(N_KV)]
        for pp in range(pages_per_blk):
            p = n * pages_per_blk + pp
            kc = []
            for c in range(N_KV):
                kf = k_pages[p][pl.ds(c, PAGE, stride=N_KV), :]
                sums[c] = sums[c] + jnp.sum(kf, axis=0, keepdims=True)
                kc.append(kf.astype(BF16))
            s_scr[p * PAGE:(p + 1) * PAGE, :] = scores_t(lambda c: kc[c])
        for c in range(N_KV):
            kb_scr[c, n:n + 1, :] = sums[c] * (1.0 / MOBA_BLOCK)
    gate_t = scores_t(lambda c: kb_scr[c].astype(BF16))
    sel = _select_bias(gate_t, nblk)

    slope = slope_ref[...]
    t_q = lax.broadcasted_iota(jnp.int32, (1, LANES), 1) % t_dec
    q_pos = (past + t_q).astype(F32)

    def logits(raw, k_pos):
        return raw * SCALE - slope * (q_pos - k_pos)

    kn = kn_ref[...]
    vn = vn_ref[...]
    knc = [kn[:, c * HD:(c + 1) * HD].astype(BF16) for c in range(N_KV)]
    t_k = lax.broadcasted_iota(jnp.int32, (t_dec, LANES), 0)
    s_own = logits(scores_t(lambda c: knc[c]), (past + t_k).astype(F32))
    s_own = jnp.where(t_k <= t_q, s_own, NEG)
    m = jnp.max(s_own, axis=0, keepdims=True)

    sub = lax.broadcasted_iota(jnp.int32, (PAGE, LANES), 0)
    for p in range(n_pages):
        n = p // pages_per_blk
        k_pos = (sub + p * PAGE).astype(F32)
        s = logits(s_scr[p * PAGE:(p + 1) * PAGE, :], k_pos) + sel[n:n + 1, :]
        s_scr[p * PAGE:(p + 1) * PAGE, :] = s
        m = jnp.maximum(m, jnp.max(s, axis=0, keepdims=True))

    p_own = jnp.exp(s_own - m)
    l = jnp.sum(p_own, axis=0, keepdims=True)
    row_kv = lax.broadcasted_iota(jnp.int32, (LANES, HD), 0) // rows_per_kv
    contract0 = (((0,), (0,)), ((), ()))
    p_own_b = p_own.astype(BF16)
    acc = jnp.zeros((LANES, HD), F32)
    for c in range(N_KV):
        part = lax.dot_general(p_own_b, vn[:, c * HD:(c + 1) * HD].astype(BF16), contract0,
                               preferred_element_type=F32)
        acc = acc + jnp.where(row_kv == c, part, 0.0)
    for p in range(n_pages):
        pr = jnp.exp(s_scr[p * PAGE:(p + 1) * PAGE, :] - m)
        l = l + jnp.sum(pr, axis=0, keepdims=True)
        pb = pr.astype(BF16)
        for c in range(N_KV):
            vf = v_pages[p][pl.ds(c, PAGE, stride=N_KV), :].astype(BF16)
            part = lax.dot_general(pb, vf, contract0, preferred_element_type=F32)
            acc = acc + jnp.where(row_kv == c, part, 0.0)
    l_col = jnp.broadcast_to(l, (LANES, LANES)).T
    out = acc / l_col
    for h in range(N_HEADS):
        o_ref[:, h * HD:(h + 1) * HD] = out[h * t_dec:(h + 1) * t_dec, :].astype(o_ref.dtype)


def _moba_sample(proj, cache_k2d, cache_v2d, page_table, n_seq, t_dec):
    n_pages = page_table.shape[1]
    past = n_pages * PAGE
    nblk = past // MOBA_BLOCK
    page_rows = PAGE * N_KV
    slopes = np.asarray(2.0 ** (-8.0 * np.arange(1, N_HEADS + 1) / N_HEADS), np.float32)
    slope_row = jnp.asarray(np.repeat(slopes, t_dec)[None, :])

    def page_spec(p):
        return pl.BlockSpec((page_rows, HD), lambda b, pt, p=p: (pt[b, p], 0))

    grid_spec = pltpu.PrefetchScalarGridSpec(
        num_scalar_prefetch=1,
        grid=(n_seq,),
        in_specs=[
            pl.BlockSpec((t_dec, Q_A), lambda b, pt: (b, C_Q // Q_A)),
            pl.BlockSpec((t_dec, KV_A), lambda b, pt: (b, C_K // KV_A)),
            pl.BlockSpec((t_dec, KV_A), lambda b, pt: (b, C_V // KV_A)),
            pl.BlockSpec((1, LANES), lambda b, pt: (0, 0)),
        ] + [page_spec(p) for p in range(n_pages)] * 2,
        out_specs=pl.BlockSpec((t_dec, Q_A), lambda b, pt: (b, 0)),
        scratch_shapes=[
            pltpu.VMEM((past, LANES), F32),
            pltpu.VMEM((N_KV, nblk, HD), F32),
        ],
    )
    return pl.pallas_call(
        functools.partial(_moba_sample_kernel, n_pages=n_pages, t_dec=t_dec),
        grid_spec=grid_spec,
        out_shape=jax.ShapeDtypeStruct((n_seq * t_dec, Q_A), BF16),
        compiler_params=_params(("arbitrary",), 40),
        name="moba_sample",
    )(page_table, proj, proj, proj, slope_row, *([cache_k2d] * n_pages), *([cache_v2d] * n_pages))


DN_TILE = 128
PK_W, PK_QD, PK_KD, PK_QK = 0, 1, 2, 3
INV_BASE_LEVELS = 3


def _conv_silu(x, w, shifted):
    y = x * w[CONV_W - 1:CONV_W, :]
    for d in range(1, CONV_W):
        y = y + shifted(d) * w[CONV_W - 1 - d:CONV_W - d, :]
    return y * _sigmoid(y)


def _delta_prep_kernel(x_ref, halo_ref, ba_ref, cw_ref, al_ref, dt_ref, pk_ref, u_ref, ge_ref,
                       gate_scr, xe_scr, *, seg, tiles_per_step, tiles_per_seq):
    step = pl.program_id(0)
    h = pl.program_id(1)
    levels = int(math.log2(seg))
    ri = lax.broadcasted_iota(jnp.int32, (DN_TILE, DN_TILE), 0)
    ci = lax.broadcasted_iota(jnp.int32, (DN_TILE, DN_TILE), 1)
    same = jnp.where((ri >> levels) == (ci >> levels), 1.0, 0.0)
    incl = jnp.where(ci <= ri, same, 0.0)
    strict = jnp.where(ci < ri, same, 0.0)
    eye = jnp.where(ci == ri, 1.0, 0.0)
    base_levels = min(INV_BASE_LEVELS, levels)
    merge_masks = [jnp.where((ri >> lv) == (ci >> lv), 1.0, 0.0) for lv in range(base_levels, levels + 1)]
    idx_b = jnp.full((DN_TILE, LANES), h, jnp.int32)
    idx_g = idx_b + DN_HEADS
    cw = cw_ref[...]
    nt = (((1,), (1,)), ((), ()))

    def mm(x, y):
        return jnp.dot(x, y, preferred_element_type=F32)

    @pl.when(h == 0)
    def _():
        after = jnp.where(ci > ri, same, 0.0)
        prefix_suffix = jnp.concatenate([incl, after], axis=0).astype(BF16)
        pieces = []
        for t in range(tiles_per_step):
            ba = ba_ref[t * DN_TILE:(t + 1) * DN_TILE, :]
            gate_scr[t, 0] = _sigmoid(ba)
            xg = ba + dt_ref[...]
            softplus = jnp.maximum(xg, 0.0) + jnp.log(1.0 + jnp.exp(-jnp.abs(xg)))
            g = -jnp.exp(al_ref[...]) * softplus
            g_hi = g.astype(BF16)
            g_r = g - g_hi.astype(F32)
            g_mid = g_r.astype(BF16)
            pieces.append((g_hi, g_mid, (g_r - g_mid.astype(F32)).astype(BF16)))
        sums = [mm(prefix_suffix, p[0]) for p in pieces]
        for piece in (1, 2):
            sums = [sm + mm(prefix_suffix, p[piece]) for sm, p in zip(sums, pieces)]
        for t in range(tiles_per_step):
            gate_scr[t, 1] = sums[t][:DN_TILE]
            gate_scr[t, 2] = sums[t][DN_TILE:]

    if seg == DN_TILE:
        tile0 = step * tiles_per_step
        first = ((tile0 % tiles_per_seq) == 0).astype(F32)
        xe_scr[0:SUBLANES, :] = halo_ref[...] * (1.0 - first)
        xe_scr[SUBLANES:, :] = x_ref[...]

    def front(t):
        rows = slice(t * DN_TILE, (t + 1) * DN_TILE)
        x = x_ref[rows, :]
        if seg == DN_TILE:
            def shifted(d):
                return xe_scr[pl.ds(SUBLANES + t * DN_TILE - d, DN_TILE), :]
        else:
            bx = halo_ref[rows, :]
            t_in = lax.broadcasted_iota(jnp.int32, x.shape, 0) & (seg - 1)

            def shifted(d):
                xs = pltpu.roll(x, d, 0)
                bs = pltpu.roll(bx, (d - seg) % DN_TILE, 0)
                return jnp.where(t_in >= d, xs, bs)
        y = _conv_silu(x, cw, shifted)
        qc, kc, v = y[:, 0:DK], y[:, DK:2 * DK], y[:, 2 * DK:3 * DK]
        q = qc * lax.rsqrt(jnp.sum(qc * qc, axis=-1, keepdims=True) + EPS) * (DK ** -0.5)
        k = kc * lax.rsqrt(jnp.sum(kc * kc, axis=-1, keepdims=True) + EPS)

        def head_lane(j, idx):
            return jnp.take_along_axis(gate_scr[t, j], idx, axis=1, mode="promise_in_bounds")

        return dict(rows=rows, q=q, k=k, v=v, beta=head_lane(0, idx_b), cum=head_lane(1, idx_g),
                    rem=head_lane(2, idx_g), kb=k.astype(BF16), qb=q.astype(BF16))

    tiles = [front(t) for t in range(tiles_per_step)]

    for ts in tiles:
        ts["kk"] = lax.dot_general(ts["kb"], ts["kb"], nt, preferred_element_type=F32)
    for ts in tiles:
        ts["qk"] = lax.dot_general(ts["qb"], ts["kb"], nt, preferred_element_type=F32)
    for ts in tiles:
        diff = ts["cum"] - ts["cum"].T
        ts["decay"] = jnp.exp(jnp.where(incl > 0.0, diff, NEG))
        ts["a"] = strict * (ts["beta"] * ts["decay"] * ts["kk"])
        a0 = ts["a"] * merge_masks[0]
        ts["t_inv"] = eye - a0
        ts["pw"] = a0
    for _ in range(base_levels - 1):
        for ts in tiles:
            pwb = ts["pw"].astype(BF16)
            ts["pw"] = mm(pwb, pwb)
        for ts in tiles:
            ts["t_inv"] = ts["t_inv"] + mm(ts["t_inv"].astype(BF16), ts["pw"].astype(BF16))
    for lv in range(base_levels, levels):
        lmask = merge_masks[lv - base_levels + 1] - merge_masks[lv - base_levels]
        for ts in tiles:
            ts["tb"] = ts["t_inv"].astype(BF16)
            ts["tl"] = mm(ts["tb"], (ts["a"] * lmask).astype(BF16))
        for ts in tiles:
            ts["t_inv"] = ts["t_inv"] - mm(ts["tl"].astype(BF16), ts["tb"])
    for ts in tiles:
        ts["gam"] = jnp.exp(ts["cum"])
        ts["tb"] = ts["t_inv"].astype(BF16)
        ts["w"] = mm(ts["tb"], (ts["beta"] * ts["gam"] * ts["k"]).astype(BF16))
    for ts in tiles:
        ts["u"] = mm(ts["tb"], (ts["beta"] * ts["v"]).astype(BF16))
    segs = DN_TILE // seg
    for t, ts in enumerate(tiles):
        rows = ts["rows"]
        pk_ref[rows, PK_W * DK:(PK_W + 1) * DK] = ts["w"].astype(BF16)
        pk_ref[rows, PK_QD * DK:(PK_QD + 1) * DK] = (ts["q"] * ts["gam"]).astype(BF16)
        pk_ref[rows, PK_KD * DK:(PK_KD + 1) * DK] = (ts["k"] * jnp.exp(ts["rem"])).astype(BF16)
        pk_ref[rows, PK_QK * DK:(PK_QK + 1) * DK] = (ts["qk"] * ts["decay"]).astype(BF16)
        u_ref[rows, :] = ts["u"]
        g_end = jnp.exp(ts["cum"] + ts["rem"])
        for j in range(segs):
            ge_ref[t * segs + j] = g_end[j * seg:j * seg + 1, :]


def _delta_prep(proj, halo_src, conv_w_perm, al_row, dt_row, *, seg, t_len):
    n = proj.shape[0]
    tiles_per_step = 8 if n % (8 * DN_TILE) == 0 else 4
    tr = tiles_per_step * DN_TILE
    assert n % tr == 0
    steps = n // tr
    qkv_blk = C_QKV // HEAD_COLS
    segs = tr // seg
    if seg == DN_TILE:
        assert t_len % tr == 0
        sub_per_step = tr // SUBLANES
        halo_spec = pl.BlockSpec(
            (SUBLANES, HEAD_COLS), lambda s, h: (jnp.maximum(s * sub_per_step - 1, 0), qkv_blk + h))
    else:
        halo_spec = pl.BlockSpec((tr, HEAD_COLS), lambda s, h: (s, h))
    return pl.pallas_call(
        functools.partial(_delta_prep_kernel, seg=seg, tiles_per_step=tiles_per_step,
                          tiles_per_seq=max(t_len // DN_TILE, 1)),
        grid=(steps, DN_HEADS),
        in_specs=[
            pl.BlockSpec((tr, HEAD_COLS), lambda s, h: (s, qkv_blk + h)),
            halo_spec,
            pl.BlockSpec((tr, LANES), lambda s, h: (s, C_BA // LANES)),
            pl.BlockSpec((CONV_W, HEAD_COLS), lambda s, h: (0, h)),
            pl.BlockSpec((1, LANES), lambda s, h: (0, 0)),
            pl.BlockSpec((1, LANES), lambda s, h: (0, 0)),
        ],
        out_specs=[
            pl.BlockSpec((None, tr, 4 * DK), lambda s, h: (h, s, 0)),
            pl.BlockSpec((None, tr, DV), lambda s, h: (h, s, 0)),
            pl.BlockSpec((None, segs, 1, LANES), lambda s, h: (h, s, 0, 0)),
        ],
        out_shape=[
            jax.ShapeDtypeStruct((DN_HEADS, n, 4 * DK), BF16),
            jax.ShapeDtypeStruct((DN_HEADS, n, DV), F32),
            jax.ShapeDtypeStruct((DN_HEADS, n // seg, 1, LANES), F32),
        ],
        scratch_shapes=[pltpu.VMEM((tiles_per_step, 3, DN_TILE, LANES), F32),
                        pltpu.VMEM((SUBLANES + tr, HEAD_COLS), F32)],
        compiler_params=_params(("parallel", "arbitrary"), 32),
        name="delta_prep",
    )(proj, halo_src, proj, conv_w_perm, al_row, dt_row)


def _out_gate(o, z, dnw):
    return _rms(o, dnw) * (z * _sigmoid(z))


def _delta_scan_prompt_kernel(pk_ref, u_ref, ge_ref, z0_ref, z1_ref, dnw_ref, o_ref, sout_ref, s_scr, *, hg):
    c = pl.program_id(2)

    @pl.when(c == 0)
    def _():
        s_scr[...] = jnp.zeros(s_scr.shape, F32)

    contract0 = (((0,), (0,)), ((), ()))
    heads = range(hg)
    res = []
    for j in heads:
        wq = jnp.concatenate([pk_ref[j, :, PK_W * DK:(PK_W + 1) * DK],
                              pk_ref[j, :, PK_QD * DK:(PK_QD + 1) * DK]], axis=0)
        res.append(jnp.dot(wq, s_scr[j].astype(BF16), preferred_element_type=F32))
    ub = [(u_ref[j] - res[j][:DN_TILE]).astype(BF16) for j in heads]
    o = [res[j][DN_TILE:] + jnp.dot(pk_ref[j, :, PK_QK * DK:(PK_QK + 1) * DK], ub[j],
                                    preferred_element_type=F32) for j in heads]
    upd = [lax.dot_general(pk_ref[j, :, PK_KD * DK:(PK_KD + 1) * DK], ub[j], contract0,
                           preferred_element_type=F32) for j in heads]
    half = hg // 2
    for j in heads:
        s_scr[j] = ge_ref[j, 0] * s_scr[j] + upd[j]
        z_ref = z0_ref if j < half else z1_ref
        zc = (j % half) * DV
        o_ref[:, j * DV:(j + 1) * DV] = _out_gate(o[j], z_ref[:, zc:zc + DV], dnw_ref[...]).astype(o_ref.dtype)

    @pl.when(c == pl.num_programs(2) - 1)
    def _():
        sout_ref[...] = s_scr[...]


def _delta_scan_prompt(pk, u, ge, proj, dnw_row, batch, t_len):
    hg = DN_HEADS
    zw = hg * DV // 2
    assert C_Z % zw == 0
    nc = t_len // DN_TILE
    zblk = C_Z // zw
    return pl.pallas_call(
        functools.partial(_delta_scan_prompt_kernel, hg=hg),
        grid=(batch, DN_HEADS // hg, nc),
        in_specs=[
            pl.BlockSpec((hg, DN_TILE, 4 * DK), lambda b, g, c: (g, b * nc + c, 0)),
            pl.BlockSpec((hg, DN_TILE, DV), lambda b, g, c: (g, b * nc + c, 0)),
            pl.BlockSpec((hg, 1, 1, LANES), lambda b, g, c: (g, b * nc + c, 0, 0)),
            pl.BlockSpec((DN_TILE, zw), lambda b, g, c: (b * nc + c, zblk)),
            pl.BlockSpec((DN_TILE, zw), lambda b, g, c: (b * nc + c, zblk + 1)),
            pl.BlockSpec((1, DV), lambda b, g, c: (0, 0)),
        ],
        out_specs=[
            pl.BlockSpec((DN_TILE, hg * DV), lambda b, g, c: (b * nc + c, g)),
            pl.BlockSpec((None, hg, DK, DV), lambda b, g, c: (b, g, 0, 0)),
        ],
        out_shape=[
            jax.ShapeDtypeStruct((batch * t_len, DN_V), BF16),
            jax.ShapeDtypeStruct((batch, DN_HEADS, DK, DV), F32),
        ],
        scratch_shapes=[pltpu.VMEM((hg, DK, DV), F32)],
        compiler_params=_params(("parallel", "parallel", "arbitrary"), 32),
        name="delta_scan_prompt",
    )(pk, u, ge, proj, proj, dnw_row)


def _delta_scan_sample_kernel(pk_ref, u_ref, ge_ref, z_ref, dnw_ref, s0_ref, o_ref, sout_ref,
                              ub_scr, oq_scr, pk32_scr, *, hg, t_dec, seq_unroll):
    n_seq = DN_TILE // t_dec
    contract0 = (((0,), (0,)), ((), ()))
    for j in range(hg):
        pk32_scr[j] = pk_ref[j, :, 0:3 * DK].astype(F32)

    def group(gi, carry):
        pairs = []
        for a in range(seq_unroll):
            i = gi * seq_unroll + a
            r0 = pl.multiple_of(i * t_dec, t_dec)
            pairs += [(i, r0, j) for j in range(hg)]
        res = []
        for i, r0, j in pairs:
            wq = jnp.concatenate([pk32_scr[j, pl.ds(r0, t_dec), PK_W * DK:(PK_W + 1) * DK],
                                  pk32_scr[j, pl.ds(r0, t_dec), PK_QD * DK:(PK_QD + 1) * DK]], axis=0)
            res.append(jnp.dot(wq.astype(BF16), s0_ref[i, j].astype(BF16), preferred_element_type=F32))
        upd = []
        for (i, r0, j), r in zip(pairs, res):
            uu = u_ref[j, pl.ds(r0, t_dec), :] - r[:t_dec]
            ub_scr[j, pl.ds(r0, t_dec), :] = uu
            oq_scr[j, pl.ds(r0, t_dec), :] = r[t_dec:]
            kd = pk32_scr[j, pl.ds(r0, t_dec), PK_KD * DK:(PK_KD + 1) * DK]
            upd.append(lax.dot_general(kd.astype(BF16), uu.astype(BF16), contract0,
                                       preferred_element_type=F32))
        for (i, r0, j), up in zip(pairs, upd):
            sout_ref[i, j] = ge_ref[j, i] * s0_ref[i, j] + up
        return carry

    lax.fori_loop(0, n_seq // seq_unroll, group, 0)
    o = [oq_scr[j] + jnp.dot(pk_ref[j, :, PK_QK * DK:(PK_QK + 1) * DK], ub_scr[j].astype(BF16),
                             preferred_element_type=F32) for j in range(hg)]
    for j in range(hg):
        cols = slice(j * DV, (j + 1) * DV)
        o_ref[:, cols] = _out_gate(o[j], z_ref[:, cols], dnw_ref[...]).astype(o_ref.dtype)


def _delta_scan_sample(pk, u, ge, proj, dnw_row, s0, n_seq, t_dec):
    hg = 4
    seq_per_tile = DN_TILE // t_dec
    tiles = n_seq // seq_per_tile
    zblk = C_Z // (hg * DV)
    return pl.pallas_call(
        functools.partial(_delta_scan_sample_kernel, hg=hg, t_dec=t_dec, seq_unroll=4),
        grid=(tiles, DN_HEADS // hg),
        in_specs=[
            pl.BlockSpec((hg, DN_TILE, 4 * DK), lambda r, g: (g, r, 0)),
            pl.BlockSpec((hg, DN_TILE, DV), lambda r, g: (g, r, 0)),
            pl.BlockSpec((hg, seq_per_tile, 1, LANES), lambda r, g: (g, r, 0, 0)),
            pl.BlockSpec((DN_TILE, hg * DV), lambda r, g: (r, zblk + g)),
            pl.BlockSpec((1, DV), lambda r, g: (0, 0)),
            pl.BlockSpec((seq_per_tile, hg, DK, DV), lambda r, g: (r, g, 0, 0)),
        ],
        out_specs=[
            pl.BlockSpec((DN_TILE, hg * DV), lambda r, g: (r, g)),
            pl.BlockSpec((seq_per_tile, hg, DK, DV), lambda r, g: (r, g, 0, 0)),
        ],
        out_shape=[
            jax.ShapeDtypeStruct((n_seq * t_dec, DN_V), BF16),
            jax.ShapeDtypeStruct((n_seq, DN_HEADS, DK, DV), F32),
        ],
        scratch_shapes=[pltpu.VMEM((hg, DN_TILE, DV), F32), pltpu.VMEM((hg, DN_TILE, DV), F32),
                        pltpu.VMEM((hg, DN_TILE, 3 * DK), F32)],
        compiler_params=_params(("parallel", "parallel"), 48),
        name="delta_scan_sample",
    )(pk, u, ge, proj, dnw_row, s0)


def _mix_out_kernel(oa_ref, od_ref, wa_ref, wd_ref, wo_ref, ga0_ref, ga1_ref, gd0_ref, gd1_ref, x_ref, g_ref,
                    o_ref):
    ya = jnp.dot(oa_ref[...], wa_ref[...], preferred_element_type=F32)
    yd = jnp.dot(od_ref[...], wd_ref[...], preferred_element_type=F32)
    half = D_MODEL // 2
    mixed = jnp.concatenate(
        [_sigmoid(ga0_ref[...]) * ya[:, :half] + _sigmoid(gd0_ref[...]) * yd[:, :half],
         _sigmoid(ga1_ref[...]) * ya[:, half:] + _sigmoid(gd1_ref[...]) * yd[:, half:]], axis=1)
    t = jnp.dot(mixed.astype(BF16), wo_ref[...], preferred_element_type=F32)
    o_ref[...] = x_ref[...] + _rms(t, g_ref[...])


def _mix_out(oa, od, wa, wd, wo, proj, x2d, g_row):
    n = oa.shape[0]
    tm = 256
    half = D_MODEL // 2

    def resident(shape):
        return pl.BlockSpec(shape, lambda i: (0, 0), pipeline_mode=pl.Buffered(1))

    def gate(col):
        return pl.BlockSpec((tm, half), lambda i, c=col // half: (i, c))

    return pl.pallas_call(
        _mix_out_kernel,
        grid=(n // tm,),
        in_specs=[
            pl.BlockSpec((tm, Q_A), lambda i: (i, 0)),
            pl.BlockSpec((tm, DN_V), lambda i: (i, 0)),
            resident((Q_A, D_MODEL)),
            resident((DN_V, D_MODEL)),
            resident((D_MODEL, D_MODEL)),
            gate(C_GA), gate(C_GA + half), gate(C_GD), gate(C_GD + half),
            pl.BlockSpec((tm, D_MODEL), lambda i: (i, 0)),
            pl.BlockSpec((1, D_MODEL), lambda i: (0, 0)),
        ],
        out_specs=pl.BlockSpec((tm, D_MODEL), lambda i: (i, 0)),
        out_shape=jax.ShapeDtypeStruct((n, D_MODEL), F32),
        compiler_params=_params(("parallel",), 56),
        name="mix_out",
    )(oa, od, wa, wd, wo, proj, proj, proj, proj, x2d, g_row)


def _mlp_kernel(x_ref, gpre_ref, wu_ref, wd_ref, gpost_ref, o_ref, h_ref, acc_ref):
    f = pl.program_id(1)

    @pl.when(f == 0)
    def _():
        h_ref[...] = _rms(x_ref[...], gpre_ref[...]).astype(BF16)
        acc_ref[...] = jnp.zeros(acc_ref.shape, F32)

    up = jnp.dot(h_ref[...], wu_ref[...], preferred_element_type=F32)
    act = jnp.square(jnp.maximum(up, 0.0)).astype(BF16)
    acc_ref[...] += jnp.dot(act, wd_ref[...], preferred_element_type=F32)

    @pl.when(f == pl.num_programs(1) - 1)
    def _():
        o_ref[...] = x_ref[...] + _rms(acc_ref[...], gpost_ref[...])


def _mlp(x2d, gpre, w_up, w_down, gpost):
    n = x2d.shape[0]
    tm, tf = 512, 1024
    return pl.pallas_call(
        _mlp_kernel,
        grid=(n // tm, D_FF // tf),
        in_specs=[
            pl.BlockSpec((tm, D_MODEL), lambda i, f: (i, 0)),
            pl.BlockSpec((1, D_MODEL), lambda i, f: (0, 0)),
            pl.BlockSpec((D_MODEL, tf), lambda i, f: (0, f)),
            pl.BlockSpec((tf, D_MODEL), lambda i, f: (f, 0)),
            pl.BlockSpec((1, D_MODEL), lambda i, f: (0, 0)),
        ],
        out_specs=pl.BlockSpec((tm, D_MODEL), lambda i, f: (i, 0)),
        out_shape=jax.ShapeDtypeStruct((n, D_MODEL), F32),
        scratch_shapes=[pltpu.VMEM((tm, D_MODEL), BF16), pltpu.VMEM((tm, D_MODEL), F32)],
        compiler_params=_params(("parallel", "arbitrary"), 48),
        name="mlp",
    )(x2d, gpre, w_up, w_down, gpost)


def _head_major(x):
    lead = x.shape[:-1]
    return jnp.swapaxes(x.reshape(*lead, 3, DN_HEADS, DK), -3, -2).reshape(*lead, CONV_DIM)


def _part_major(x):
    lead = x.shape[:-1]
    return jnp.swapaxes(x.reshape(*lead, DN_HEADS, 3, DK), -3, -2).reshape(*lead, CONV_DIM)


W_QKV = Q_A + 2 * KV_A
W_Z = W_QKV + CONV_DIM
W_BA = W_Z + DN_V
W_GA = W_BA + 2 * DN_HEADS
W_DIM = W_GA + 2 * D_MODEL


def _pack_w_in_kernel(w_ref, o_ref):
    def cp(dst, src, n):
        o_ref[dst:dst + n, :] = w_ref[src:src + n, :].astype(BF16)

    cp(C_Q, 0, W_QKV)
    for h in range(DN_HEADS):
        for p in range(3):
            cp(C_QKV + h * HEAD_COLS + p * DK, W_QKV + p * DN_QK + h * DK, DK)
    cp(C_Z, W_Z, DN_V)
    cp(C_GA, W_GA, 2 * D_MODEL)
    o_ref[C_BA:P_DIM, :] = jnp.zeros((P_DIM - C_BA, o_ref.shape[1]), BF16)
    cp(C_BA, W_BA, 2 * DN_HEADS)


def _pack_w_in(w_in_t):
    tc = 256
    return pl.pallas_call(
        _pack_w_in_kernel,
        grid=(D_MODEL // tc,),
        in_specs=[pl.BlockSpec((W_DIM, tc), lambda i: (0, i))],
        out_specs=pl.BlockSpec((P_DIM, tc), lambda i: (0, i)),
        out_shape=jax.ShapeDtypeStruct((P_DIM, D_MODEL), BF16),
        compiler_params=_params(("parallel",), 52),
        name="pack_w_in",
    )(w_in_t)


def _lane_row(vals, offset):
    return jnp.zeros((1, LANES), F32).at[0, offset:offset + vals.shape[0]].set(vals.astype(F32))


def _layer(x2d, attend, delta, w):
    proj = _in_proj(x2d, w["g_mix_pre"], w["w_in"])
    o_a = attend(proj)
    o_d, s_final = delta(proj)
    x1 = _mix_out(o_a, o_d, w["w_ba"], w["w_bd"], w["w_out"], proj, x2d, w["g_mix_post"])
    y = _mlp(x1, w["g_mlp_pre"], w["w_up"], w["w_down"], w["g_mlp_post"])
    return y, proj, s_final


def kernel(x_prompt, x_sample, cache_k, cache_v, state_delta, state_conv, page_table, w_in, conv_w, a_log,
           dt_bias, dn_norm_w, w_branch_attn, w_branch_delta, w_out, g_mix_pre, g_mix_post, g_mlp_pre,
           g_mlp_post, w_up, w_down):
    depth = w_in.shape[0]
    assert depth == 1
    b_p, t_p, _ = x_prompt.shape
    b_s, t_s, _ = x_sample.shape
    l = 0
    w = {
        "w_in": _pack_w_in(jnp.transpose(w_in[l])),
        "g_mix_pre": g_mix_pre[l][None, :],
        "g_mix_post": g_mix_post[l][None, :],
        "g_mlp_pre": g_mlp_pre[l][None, :],
        "g_mlp_post": g_mlp_post[l][None, :],
        "w_ba": w_branch_attn[l].astype(BF16),
        "w_bd": w_branch_delta[l].astype(BF16),
        "w_out": w_out[l].astype(BF16),
        "w_up": w_up[l].astype(BF16),
        "w_down": w_down[l].astype(BF16),
    }
    cw = _head_major(conv_w[l])
    al_row = _lane_row(a_log[l], DN_HEADS)
    dt_row = _lane_row(dt_bias[l], DN_HEADS)
    dnw_row = dn_norm_w[l][None, :]

    kv_prompt = []

    def attend_p(proj):
        kaug, vb, kmean, k5, v5 = _kprep(proj, b_p, t_p)
        kv_prompt.extend([k5[None], v5[None]])
        return _moba_prompt(proj, kaug, vb, kmean, b_p, t_p)

    def delta_p(proj):
        pk, u, ge = _delta_prep(proj, proj, cw, al_row, dt_row, seg=DN_TILE, t_len=t_p)
        return _delta_scan_prompt(pk, u, ge, proj, dnw_row, b_p, t_p)

    y_p, proj_p, d_p = _layer(x_prompt.reshape(b_p * t_p, D_MODEL), attend_p, delta_p, w)

    n_pool = cache_k.shape[1]
    ck2d = cache_k[l].reshape(n_pool * PAGE * N_KV, HD)
    cv2d = cache_v[l].reshape(n_pool * PAGE * N_KV, HD)
    conv_rows = _head_major(state_conv[l])
    halo_s = jnp.pad(conv_rows, ((0, 0), (t_s - (CONV_W - 1), 0), (0, 0))).reshape(b_s * t_s, CONV_DIM)

    def attend_s(proj):
        return _moba_sample(proj, ck2d, cv2d, page_table, b_s, t_s)

    def delta_s(proj):
        pk, u, ge = _delta_prep(proj, halo_s, cw, al_row, dt_row, seg=t_s, t_len=t_s)
        return _delta_scan_sample(pk, u, ge, proj, dnw_row, state_delta[l], b_s, t_s)

    y_s, proj_s, d_s = _layer(x_sample.reshape(b_s * t_s, D_MODEL), attend_s, delta_s, w)

    def kv_out(proj, b, t):
        k = proj[:, C_K:C_K + KV_A].reshape(1, b, t, N_KV, HD)
        v = proj[:, C_V:C_V + KV_A].reshape(1, b, t, N_KV, HD)
        return k, v

    def conv_out(proj, b, t):
        raw = proj.reshape(b, t, P_DIM)[:, t - (CONV_W - 1):, C_QKV:C_QKV + CONV_DIM]
        return _part_major(raw)[None]

    k_p, v_p = kv_prompt
    k_s, v_s = kv_out(proj_s, b_s, t_s)
    return (y_p.reshape(b_p, t_p, D_MODEL), y_s.reshape(b_s, t_s, D_MODEL), k_p, v_p, d_p[None],
            conv_out(proj_p, b_p, t_p), k_s, v_s, d_s[None], conv_out(proj_s, b_s, t_s))
```

```python
import functools
import math

import numpy as np
import jax
import jax.numpy as jnp
from jax import lax
from jax.experimental import pallas as pl
from jax.experimental.pallas import tpu as pltpu

F32 = jnp.float32
BF16 = jnp.bfloat16

D_MODEL = 2048
N_HEADS = 16
N_KV = 4
HD = 128
GROUP = N_HEADS // N_KV
MOBA_BLOCK = 256
MOBA_TOPK = 3
PAGE = 128
DN_HEADS = 16
DK = 128
DV = 128
CONV_W = 4
Q_A = N_HEADS * HD
KV_A = N_KV * HD
DN_QK = DN_HEADS * DK
DN_V = DN_HEADS * DV
CONV_DIM = 2 * DN_QK + DN_V
D_FF = 4 * D_MODEL
EPS = 1e-6

LANES = 128
SUBLANES = 8

C_Q = 0
C_K = C_Q + Q_A
C_V = C_K + KV_A
C_QKV = C_V + KV_A
C_Z = C_QKV + CONV_DIM
C_GA = C_Z + DN_V
C_GD = C_GA + D_MODEL
C_BA = C_GD + D_MODEL
P_DIM = C_BA + LANES
HEAD_COLS = 3 * DK

NEG = -1e30
SCALE = HD ** -0.5
EXP2_C = SCALE * math.log2(math.e)


V7X_VMEM_BUDGET = 56 * 2**20


def _nbytes(shape, dtype):
    return math.prod(shape) * jnp.dtype(dtype).itemsize


def _params(sem, pipelined, resident=(), temps=()):
    need = 2 * sum(_nbytes(*b) for b in pipelined) + sum(_nbytes(*b) for b in (*resident, *temps))
    limit = -(-need // 2**20) * 2**20
    assert limit <= V7X_VMEM_BUDGET, (limit, V7X_VMEM_BUDGET)
    return pltpu.CompilerParams(dimension_semantics=sem, vmem_limit_bytes=limit)


def _rms(x, g):
    return x * lax.rsqrt(jnp.mean(x * x, axis=-1, keepdims=True) + EPS) * g


def _sigmoid(x):
    return 1.0 / (1.0 + jnp.exp(-x))


def _in_proj_kernel(x_ref, g_ref, w_ref, o_ref, h_ref):
    @pl.when(pl.program_id(1) == 0)
    def _():
        h_ref[...] = _rms(x_ref[...], g_ref[...]).astype(BF16)

    o_ref[...] = lax.dot_general(h_ref[...], w_ref[...], (((1,), (1,)), ((), ())), preferred_element_type=F32)


def _in_proj(x2d, g_row, wp):
    n = x2d.shape[0]
    tm = 1024 if n % 1024 == 0 else 512
    tn = 1408
    assert n % tm == 0 and P_DIM % tn == 0
    return pl.pallas_call(
        _in_proj_kernel,
        grid=(n // tm, P_DIM // tn),
        in_specs=[
            pl.BlockSpec((tm, D_MODEL), lambda i, j: (i, 0)),
            pl.BlockSpec((1, D_MODEL), lambda i, j: (0, 0)),
            pl.BlockSpec((tn, D_MODEL), lambda i, j: (j, 0)),
        ],
        out_specs=pl.BlockSpec((tm, tn), lambda i, j: (i, j)),
        out_shape=jax.ShapeDtypeStruct((n, P_DIM), F32),
        scratch_shapes=[pltpu.VMEM((tm, D_MODEL), BF16)],
        compiler_params=_params(
            ("parallel", "arbitrary"),
            pipelined=[((tm, D_MODEL), F32), ((1, D_MODEL), F32), ((tn, D_MODEL), BF16), ((tm, tn), F32)],
            resident=[((tm, D_MODEL), BF16)], temps=[((tm, tn), F32)]),
        name="in_proj",
    )(x2d, g_row, wp)


N_SLOPE_FEATS = 4
F_SEL = 0
F_HI = 16
F_LO = F_HI + N_SLOPE_FEATS
MAX_BLOCKS = 16


def _kprep_kernel(k_ref, v_ref, kaug_ref, vb_ref, kmean_ref, k5_ref, v5_ref, *, blocks_per_step):
    step = pl.program_id(1)
    lane = lax.broadcasted_iota(jnp.int32, (MOBA_BLOCK, LANES), 1)
    row = lax.broadcasted_iota(jnp.int32, (MOBA_BLOCK, LANES), 0).astype(F32)
    is_hi = jnp.where(lane >= F_HI, jnp.where(lane < F_LO, 1.0, 0.0), 0.0)
    is_lo = jnp.where(lane >= F_LO, jnp.where(lane < F_LO + N_SLOPE_FEATS, 1.0, 0.0), 0.0)
    ones_col = jnp.where(lane == 0, 1.0, 0.0).astype(BF16)
    for j in range(blocks_per_step):
        n = step * blocks_per_step + j
        rows = slice(j * MOBA_BLOCK, (j + 1) * MOBA_BLOCK)
        k = k_ref[rows, :]
        v = v_ref[rows, :]
        mean = jnp.mean(k, axis=0, keepdims=True)
        feat = (jnp.where(lane == n, 1.0, 0.0)
                + is_hi * (n * MOBA_BLOCK).astype(F32)
                + is_lo * row).astype(BF16)
        for c in range(N_KV):
            cols = slice(c * HD, (c + 1) * HD)
            kaug_ref[c, rows, 0:HD] = k[:, cols].astype(BF16)
            kaug_ref[c, rows, HD:2 * HD] = feat
            vb_ref[c, rows, 0:HD] = v[:, cols].astype(BF16)
            vb_ref[c, rows, HD:2 * HD] = ones_col
            kmean_ref[c, j:j + 1, :] = mean[:, cols]
            k5_ref[rows, c, :] = k[:, cols]
            v5_ref[rows, c, :] = v[:, cols]


def _kprep(proj, batch, t_len):
    nblk = t_len // MOBA_BLOCK
    bps = min(nblk, 8)
    assert nblk % bps == 0 and nblk <= MAX_BLOCKS
    steps = nblk // bps
    rows = bps * MOBA_BLOCK
    return pl.pallas_call(
        functools.partial(_kprep_kernel, blocks_per_step=bps),
        grid=(batch, steps),
        in_specs=[
            pl.BlockSpec((rows, KV_A), lambda b, s: (b * steps + s, C_K // KV_A)),
            pl.BlockSpec((rows, KV_A), lambda b, s: (b * steps + s, C_V // KV_A)),
        ],
        out_specs=[
            pl.BlockSpec((None, N_KV, rows, 2 * HD), lambda b, s: (b, 0, s, 0)),
            pl.BlockSpec((None, N_KV, rows, 2 * HD), lambda b, s: (b, 0, s, 0)),
            pl.BlockSpec((None, N_KV, bps, HD), lambda b, s: (b, 0, s, 0)),
            pl.BlockSpec((None, rows, N_KV, HD), lambda b, s: (b, s, 0, 0)),
            pl.BlockSpec((None, rows, N_KV, HD), lambda b, s: (b, s, 0, 0)),
        ],
        out_shape=[
            jax.ShapeDtypeStruct((batch, N_KV, t_len, 2 * HD), BF16),
            jax.ShapeDtypeStruct((batch, N_KV, t_len, 2 * HD), BF16),
            jax.ShapeDtypeStruct((batch, N_KV, nblk, HD), F32),
            jax.ShapeDtypeStruct((batch, t_len, N_KV, HD), F32),
            jax.ShapeDtypeStruct((batch, t_len, N_KV, HD), F32),
        ],
        compiler_params=_params(
            ("parallel", "arbitrary"),
            pipelined=[((rows, KV_A), F32)] * 2 + [((N_KV, rows, 2 * HD), BF16)] * 2
            + [((N_KV, bps, HD), F32)] + [((rows, N_KV, HD), F32)] * 2),
        name="kprep",
    )(proj, proj)


def _slope_pieces():
    slopes = np.asarray(2.0 ** (-8.0 * np.arange(1, N_HEADS + 1) / N_HEADS), np.float32).astype(np.float64)
    x = slopes / SCALE
    pieces = []
    for _ in range(N_SLOPE_FEATS):
        p = x.astype(np.float32).astype(jnp.bfloat16).astype(np.float64)
        pieces.append(p)
        x = x - p
    return np.stack(pieces, axis=1)


def _slope_feature_table():
    pieces = _slope_pieces()
    tab = np.zeros((N_KV, 2 * N_SLOPE_FEATS, GROUP * MOBA_BLOCK), np.float32)
    for c in range(N_KV):
        for g in range(GROUP):
            cols = slice(g * MOBA_BLOCK, (g + 1) * MOBA_BLOCK)
            for f in range(N_SLOPE_FEATS):
                tab[c, f, cols] = pieces[c * GROUP + g, f]
                tab[c, N_SLOPE_FEATS + f, cols] = pieces[c * GROUP + g, f]
    return jnp.asarray(tab)


def _select_bias(gate_t, own):
    nblk = gate_t.shape[0]
    blk = lax.broadcasted_iota(jnp.int32, gate_t.shape, 0)
    past = blk < own
    gm = jnp.where(past, gate_t, -jnp.inf)
    rank = jnp.zeros(gate_t.shape, F32)
    for m in range(nblk):
        row = gm[m:m + 1, :]
        tie = jnp.where(blk > m, 1.0, 0.0)
        rank = rank + jnp.where(row > gm, 1.0, jnp.where(row == gm, tie, 0.0))
    keep_past = jnp.where(past, jnp.where(rank < MOBA_TOPK - 0.5, 0.0, NEG), NEG)
    return jnp.where(blk == own, 0.0, keep_past)


Q_TILES = 2


def _moba_prompt_kernel(q_ref, kmean_ref, sf_ref, kaug_ref, vb_ref, o_ref,
                        feat_t, qaug, m_s, acc_s, s_a, s_b, s_c, s_d, *, nblk):
    i0 = pl.program_id(2) * Q_TILES
    tile = GROUP * MOBA_BLOCK
    rows = Q_TILES * tile
    q = q_ref[...]
    qs = jnp.concatenate([q[a * MOBA_BLOCK:(a + 1) * MOBA_BLOCK, g * HD:(g + 1) * HD]
                          for a in range(Q_TILES) for g in range(GROUP)], axis=0).astype(BF16)
    qaug[:, 0:HD] = qs
    gate_t = lax.dot_general(kmean_ref[...].astype(BF16), qs, (((1,), (1,)), ((), ())),
                             preferred_element_type=F32)
    own = i0 + lax.broadcasted_iota(jnp.int32, (1, rows), 1) // tile
    feat_t[...] = jnp.zeros(feat_t.shape, F32)
    feat_t[F_SEL:F_SEL + nblk, :] = _select_bias(gate_t, own)
    for a in range(Q_TILES):
        feat_t[F_HI:F_HI + 2 * N_SLOPE_FEATS, a * tile:(a + 1) * tile] = sf_ref[...]
    qaug[:, HD:2 * HD] = feat_t[...].T.astype(BF16)

    def key_rows(n):
        return pl.ds(pl.multiple_of(n * MOBA_BLOCK, MOBA_BLOCK), MOBA_BLOCK)

    def raw_scores(n, r0=0):
        return lax.dot_general(qaug[r0:, :], kaug_ref[key_rows(n), :], (((1,), (1,)), ((), ())),
                               preferred_element_type=F32)

    def pair_scores(pr, bufs):
        n0 = jnp.maximum(jnp.minimum(2 * pr, i0 - 2), 0)
        bufs[0][...] = raw_scores(n0)
        bufs[1][...] = raw_scores(n0 + 1)

    def probs(s, m):
        return jnp.concatenate([jnp.exp2((s[:, :LANES] - m) * EXP2_C),
                                jnp.exp2((s[:, LANES:] - m) * EXP2_C)], axis=1).astype(BF16)

    def row_max(s):
        cur = jnp.max(jnp.maximum(s[:, :LANES], s[:, LANES:]), axis=-1, keepdims=True)
        return jnp.broadcast_to(cur, (s.shape[0], LANES))

    def update(s, n, r0=0):
        m_old = m_s[r0:, :]
        m_new = jnp.maximum(m_old, row_max(s))
        alpha = jnp.exp2((m_old - m_new) * EXP2_C)
        pv = jnp.dot(probs(s, m_new), vb_ref[key_rows(n), :], preferred_element_type=F32)
        acc_s[r0:, :] = jnp.concatenate([alpha, alpha], axis=1) * acc_s[r0:, :] + pv
        m_s[r0:, :] = m_new

    def causal(s):
        qi = lax.broadcasted_iota(jnp.int32, s.shape, 0) & (MOBA_BLOCK - 1)
        kj = lax.broadcasted_iota(jnp.int32, s.shape, 1)
        return kj <= qi

    m_s[...] = jnp.full(m_s.shape, NEG, F32)
    acc_s[...] = jnp.zeros(acc_s.shape, F32)
    for a in reversed(range(Q_TILES)):
        s = raw_scores(i0 + a, a * tile)
        if a == 0:
            pair_scores(0, (s_a, s_b))
        own_rows = lax.broadcasted_iota(jnp.int32, s.shape, 0) < tile
        s = jnp.where(causal(s), s, jnp.where(own_rows, NEG, s))
        update(s, i0 + a, a * tile)

    n_pairs = i0 // 2

    def pair_update(pr, bufs):
        s0 = bufs[0][...]
        s1 = bufs[1][...]
        m_old = m_s[...]
        m_new = jnp.maximum(jnp.maximum(m_old, row_max(s0)), row_max(s1))
        alpha = jnp.exp2((m_old - m_new) * EXP2_C)
        p = jnp.concatenate([probs(s0, m_new), probs(s1, m_new)], axis=1)
        v2 = vb_ref[pl.ds(pl.multiple_of(pr * 2 * MOBA_BLOCK, 2 * MOBA_BLOCK), 2 * MOBA_BLOCK), :]
        pv = jnp.dot(p, v2, preferred_element_type=F32)
        acc_s[...] = jnp.concatenate([alpha, alpha], axis=1) * acc_s[...] + pv
        m_s[...] = m_new

    def body(j, carry):
        pr = 2 * j
        pair_scores(pr + 1, (s_c, s_d))
        pair_update(pr, (s_a, s_b))
        pair_scores(pr + 2, (s_a, s_b))
        pair_update(pr + 1, (s_c, s_d))
        return carry

    lax.fori_loop(0, n_pairs // 2, body, 0)

    @pl.when(n_pairs % 2 == 1)
    def _():
        pair_update(n_pairs - 1, (s_a, s_b))

    acc = acc_s[...]
    out = acc[:, 0:HD] / acc[:, HD:HD + 1]
    for a in range(Q_TILES):
        for g in range(GROUP):
            r = a * tile + g * MOBA_BLOCK
            o_ref[a * MOBA_BLOCK:(a + 1) * MOBA_BLOCK, g * HD:(g + 1) * HD] = (
                out[r:r + MOBA_BLOCK, :].astype(o_ref.dtype))


def _moba_prompt(proj, kaug, vb, kmean, batch, t_len):
    nblk = t_len // MOBA_BLOCK
    assert nblk % Q_TILES == 0 and Q_TILES % 2 == 0
    nq = nblk // Q_TILES
    tile = GROUP * MOBA_BLOCK
    rows = Q_TILES * tile
    qrows = Q_TILES * MOBA_BLOCK
    qcols = GROUP * HD
    return pl.pallas_call(
        functools.partial(_moba_prompt_kernel, nblk=nblk),
        grid=(batch, N_KV, nq),
        in_specs=[
            pl.BlockSpec((qrows, qcols), lambda b, c, i: (b * nq + i, C_Q // qcols + c)),
            pl.BlockSpec((None, None, nblk, HD), lambda b, c, i: (b, c, 0, 0)),
            pl.BlockSpec((None, 2 * N_SLOPE_FEATS, tile), lambda b, c, i: (c, 0, 0)),
            pl.BlockSpec((None, None, t_len, 2 * HD), lambda b, c, i: (b, c, 0, 0)),
            pl.BlockSpec((None, None, t_len, 2 * HD), lambda b, c, i: (b, c, 0, 0)),
        ],
        out_specs=pl.BlockSpec((qrows, qcols), lambda b, c, i: (b * nq + i, c)),
        out_shape=jax.ShapeDtypeStruct((batch * t_len, Q_A), BF16),
        scratch_shapes=[
            pltpu.VMEM((LANES, rows), F32),
            pltpu.VMEM((rows, 2 * HD), BF16),
            pltpu.VMEM((rows, LANES), F32),
            pltpu.VMEM((rows, 2 * HD), F32),
            pltpu.VMEM((rows, MOBA_BLOCK), F32),
            pltpu.VMEM((rows, MOBA_BLOCK), F32),
            pltpu.VMEM((rows, MOBA_BLOCK), F32),
            pltpu.VMEM((rows, MOBA_BLOCK), F32),
        ],
        compiler_params=_params(
            ("parallel", "parallel", "arbitrary"),
            pipelined=[((qrows, qcols), F32), ((nblk, HD), F32), ((2 * N_SLOPE_FEATS, tile), F32),
                       ((t_len, 2 * HD), BF16), ((t_len, 2 * HD), BF16), ((qrows, qcols), BF16)],
            resident=[((LANES, rows), F32), ((rows, 2 * HD), BF16), ((rows, LANES), F32), ((rows, 2 * HD), F32)]
            + [((rows, MOBA_BLOCK), F32)] * 4,
            temps=[((rows, MOBA_BLOCK), F32)] * 3 + [((rows, 2 * MOBA_BLOCK), BF16)]),
        name="moba_prompt",
    )(proj, kmean, _slope_feature_table(), kaug, vb)


def _moba_sample_kernel(pt_ref, q_ref, kn_ref, vn_ref, slope_ref, *rest, n_pages, t_dec):
    del pt_ref
    k_pages = rest[:n_pages]
    v_pages = rest[n_pages:2 * n_pages]
    o_ref = rest[2 * n_pages]
    s_scr, kb_scr = rest[2 * n_pages + 1:]
    past = n_pages * PAGE
    nblk = past // MOBA_BLOCK
    pages_per_blk = MOBA_BLOCK // PAGE
    rows = N_HEADS * t_dec
    assert rows == LANES

    q = q_ref[...]
    q_rows = jnp.concatenate([q[:, h * HD:(h + 1) * HD] for h in range(N_HEADS)], axis=0)
    q_t = q_rows.T.astype(BF16)
    lane = lax.broadcasted_iota(jnp.int32, (HD, LANES), 1)
    rows_per_kv = GROUP * t_dec
    zero = jnp.zeros((HD, LANES), BF16)
    q_bd = [jnp.where(lane // rows_per_kv == c, q_t, zero) for c in range(N_KV)]

    def scores_t(k2d_rows):
        acc = None
        for c in range(N_KV):
            part = jnp.dot(k2d_rows(c), q_bd[c], preferred_element_type=F32)
            acc = part if acc is None else acc + part
        return acc

    for n in range(nblk):
        sums = [jnp.zeros((1, HD), F32) for _ in range(N_KV)]
        for pp in range(pages_per_blk):
            p = n * pages_per_blk + pp
            kc = []
            for c in range(N_KV):
                kf = k_pages[p][pl.ds(c, PAGE, stride=N_KV), :]
                sums[c] = sums[c] + jnp.sum(kf, axis=0, keepdims=True)
                kc.append(kf.astype(BF16))
            s_scr[p * PAGE:(p + 1) * PAGE, :] = scores_t(lambda c: kc[c])
        for c in range(N_KV):
            kb_scr[c, n:n + 1, :] = sums[c] * (1.0 / MOBA_BLOCK)
    gate_t = scores_t(lambda c: kb_scr[c].astype(BF16))
    sel = _select_bias(gate_t, nblk)

    slope = slope_ref[...]
    t_q = lax.broadcasted_iota(jnp.int32, (1, LANES), 1) % t_dec
    q_pos = (past + t_q).astype(F32)

    def logits(raw, k_pos):
        return raw * SCALE - slope * (q_pos - k_pos)

    kn = kn_ref[...]
    vn = vn_ref[...]
    knc = [kn[:, c * HD:(c + 1) * HD].astype(BF16) for c in range(N_KV)]
    t_k = lax.broadcasted_iota(jnp.int32, (t_dec, LANES), 0)
    s_own = logits(scores_t(lambda c: knc[c]), (past + t_k).astype(F32))
    s_own = jnp.where(t_k <= t_q, s_own, NEG)
    m = jnp.max(s_own, axis=0, keepdims=True)

    sub = lax.broadcasted_iota(jnp.int32, (PAGE, LANES), 0)
    for p in range(n_pages):
        n = p // pages_per_blk
        k_pos = (sub + p * PAGE).astype(F32)
        s = logits(s_scr[p * PAGE:(p + 1) * PAGE, :], k_pos) + sel[n:n + 1, :]
        s_scr[p * PAGE:(p + 1) * PAGE, :] = s
        m = jnp.maximum(m, jnp.max(s, axis=0, keepdims=True))

    p_own = jnp.exp(s_own - m)
    l = jnp.sum(p_own, axis=0, keepdims=True)
    row_kv = lax.broadcasted_iota(jnp.int32, (LANES, HD), 0) // rows_per_kv
    contract0 = (((0,), (0,)), ((), ()))
    p_own_b = p_own.astype(BF16)
    acc = jnp.zeros((LANES, HD), F32)
    for c in range(N_KV):
        part = lax.dot_general(p_own_b, vn[:, c * HD:(c + 1) * HD].astype(BF16), contract0,
                               preferred_element_type=F32)
        acc = acc + jnp.where(row_kv == c, part, 0.0)
    for p in range(n_pages):
        pr = jnp.exp(s_scr[p * PAGE:(p + 1) * PAGE, :] - m)
        l = l + jnp.sum(pr, axis=0, keepdims=True)
        pb = pr.astype(BF16)
        for c in range(N_KV):
            vf = v_pages[p][pl.ds(c, PAGE, stride=N_KV), :].astype(BF16)
            part = lax.dot_general(pb, vf, contract0, preferred_element_type=F32)
            acc = acc + jnp.where(row_kv == c, part, 0.0)
    l_col = jnp.broadcast_to(l, (LANES, LANES)).T
    out = acc / l_col
    for h in range(N_HEADS):
        o_ref[:, h * HD:(h + 1) * HD] = out[h * t_dec:(h + 1) * t_dec, :].astype(o_ref.dtype)


def _moba_sample(proj, cache_k2d, cache_v2d, page_table, n_seq, t_dec):
    n_pages = page_table.shape[1]
    past = n_pages * PAGE
    nblk = past // MOBA_BLOCK
    page_rows = PAGE * N_KV
    slopes = np.asarray(2.0 ** (-8.0 * np.arange(1, N_HEADS + 1) / N_HEADS), np.float32)
    slope_row = jnp.asarray(np.repeat(slopes, t_dec)[None, :])

    def page_spec(p):
        return pl.BlockSpec((page_rows, HD), lambda b, pt, p=p: (pt[b, p], 0))

    grid_spec = pltpu.PrefetchScalarGridSpec(
        num_scalar_prefetch=1,
        grid=(n_seq,),
        in_specs=[
            pl.BlockSpec((t_dec, Q_A), lambda b, pt: (b, C_Q // Q_A)),
            pl.BlockSpec((t_dec, KV_A), lambda b, pt: (b, C_K // KV_A)),
            pl.BlockSpec((t_dec, KV_A), lambda b, pt: (b, C_V // KV_A)),
            pl.BlockSpec((1, LANES), lambda b, pt: (0, 0)),
        ] + [page_spec(p) for p in range(n_pages)] * 2,
        out_specs=pl.BlockSpec((t_dec, Q_A), lambda b, pt: (b, 0)),
        scratch_shapes=[
            pltpu.VMEM((past, LANES), F32),
            pltpu.VMEM((N_KV, nblk, HD), F32),
        ],
    )
    return pl.pallas_call(
        functools.partial(_moba_sample_kernel, n_pages=n_pages, t_dec=t_dec),
        grid_spec=grid_spec,
        out_shape=jax.ShapeDtypeStruct((n_seq * t_dec, Q_A), BF16),
        compiler_params=_params(
            ("arbitrary",),
            pipelined=[((t_dec, Q_A), F32), ((t_dec, KV_A), F32), ((t_dec, KV_A), F32), ((1, LANES), F32),
                       ((t_dec, Q_A), BF16)] + [((page_rows, HD), F32)] * (2 * n_pages),
            resident=[((past, LANES), F32), ((N_KV, nblk, HD), F32)],
            temps=[((PAGE, LANES), F32)] * 16),
        name="moba_sample",
    )(page_table, proj, proj, proj, slope_row, *([cache_k2d] * n_pages), *([cache_v2d] * n_pages))


DN_TILE = 128
PK_W, PK_QD, PK_KD, PK_QK = 0, 1, 2, 3
INV_BASE_LEVELS = 3


def _conv_silu(x, w, shifted):
    y = x * w[CONV_W - 1:CONV_W, :]
    for d in range(1, CONV_W):
        y = y + shifted(d) * w[CONV_W - 1 - d:CONV_W - d, :]
    return y * _sigmoid(y)


def _delta_prep_kernel(x_ref, halo_ref, ba_ref, cw_ref, al_ref, dt_ref, pk_ref, u_ref, ge_ref,
                       gate_scr, xe_scr, *, seg, tiles_per_step, tiles_per_seq):
    step = pl.program_id(0)
    h = pl.program_id(1)
    levels = int(math.log2(seg))
    ri = lax.broadcasted_iota(jnp.int32, (DN_TILE, DN_TILE), 0)
    ci = lax.broadcasted_iota(jnp.int32, (DN_TILE, DN_TILE), 1)
    same = jnp.where((ri >> levels) == (ci >> levels), 1.0, 0.0)
    incl = jnp.where(ci <= ri, same, 0.0)
    strict = jnp.where(ci < ri, same, 0.0)
    eye = jnp.where(ci == ri, 1.0, 0.0)
    base_levels = min(INV_BASE_LEVELS, levels)
    merge_masks = [jnp.where((ri >> lv) == (ci >> lv), 1.0, 0.0) for lv in range(base_levels, levels + 1)]
    idx_b = jnp.full((DN_TILE, LANES), h, jnp.int32)
    idx_g = idx_b + DN_HEADS
    cw = cw_ref[...]
    nt = (((1,), (1,)), ((), ()))

    def mm(x, y):
        return jnp.dot(x, y, preferred_element_type=F32)

    @pl.when(h == 0)
    def _():
        after = jnp.where(ci > ri, same, 0.0)
        prefix_suffix = jnp.concatenate([incl, after], axis=0).astype(BF16)
        pieces = []
        for t in range(tiles_per_step):
            ba = ba_ref[t * DN_TILE:(t + 1) * DN_TILE, :]
            gate_scr[t, 0] = _sigmoid(ba)
            xg = ba + dt_ref[...]
            softplus = jnp.maximum(xg, 0.0) + jnp.log(1.0 + jnp.exp(-jnp.abs(xg)))
            g = -jnp.exp(al_ref[...]) * softplus
            g_hi = g.astype(BF16)
            g_r = g - g_hi.astype(F32)
            g_mid = g_r.astype(BF16)
            pieces.append((g_hi, g_mid, (g_r - g_mid.astype(F32)).astype(BF16)))
        sums = [mm(prefix_suffix, p[0]) for p in pieces]
        for piece in (1, 2):
            sums = [sm + mm(prefix_suffix, p[piece]) for sm, p in zip(sums, pieces)]
        for t in range(tiles_per_step):
            gate_scr[t, 1] = sums[t][:DN_TILE]
            gate_scr[t, 2] = sums[t][DN_TILE:]

    if seg == DN_TILE:
        tile0 = step * tiles_per_step
        first = ((tile0 % tiles_per_seq) == 0).astype(F32)
        xe_scr[0:SUBLANES, :] = halo_ref[...] * (1.0 - first)
        xe_scr[SUBLANES:, :] = x_ref[...]

    def front(t):
        rows = slice(t * DN_TILE, (t + 1) * DN_TILE)
        x = x_ref[rows, :]
        if seg == DN_TILE:
            def shifted(d):
                return xe_scr[pl.ds(SUBLANES + t * DN_TILE - d, DN_TILE), :]
        else:
            bx = halo_ref[rows, :]
            t_in = lax.broadcasted_iota(jnp.int32, x.shape, 0) & (seg - 1)

            def shifted(d):
                xs = pltpu.roll(x, d, 0)
                bs = pltpu.roll(bx, (d - seg) % DN_TILE, 0)
                return jnp.where(t_in >= d, xs, bs)
        y = _conv_silu(x, cw, shifted)
        qc, kc, v = y[:, 0:DK], y[:, DK:2 * DK], y[:, 2 * DK:3 * DK]
        q = qc * lax.rsqrt(jnp.sum(qc * qc, axis=-1, keepdims=True) + EPS) * (DK ** -0.5)
        k = kc * lax.rsqrt(jnp.sum(kc * kc, axis=-1, keepdims=True) + EPS)

        def head_lane(j, idx):
            return jnp.take_along_axis(gate_scr[t, j], idx, axis=1, mode="promise_in_bounds")

        return dict(rows=rows, q=q, k=k, v=v, beta=head_lane(0, idx_b), cum=head_lane(1, idx_g),
                    rem=head_lane(2, idx_g), kb=k.astype(BF16), qb=q.astype(BF16))

    tiles = [front(t) for t in range(tiles_per_step)]

    for ts in tiles:
        ts["kk"] = lax.dot_general(ts["kb"], ts["kb"], nt, preferred_element_type=F32)
    for ts in tiles:
        ts["qk"] = lax.dot_general(ts["qb"], ts["kb"], nt, preferred_element_type=F32)
    for ts in tiles:
        diff = ts["cum"] - ts["cum"].T
        ts["decay"] = jnp.exp(jnp.where(incl > 0.0, diff, NEG))
        ts["a"] = strict * (ts["beta"] * ts["decay"] * ts["kk"])
        a0 = ts["a"] * merge_masks[0]
        ts["t_inv"] = eye - a0
        ts["pw"] = a0
    for _ in range(base_levels - 1):
        for ts in tiles:
            pwb = ts["pw"].astype(BF16)
            ts["pw"] = mm(pwb, pwb)
        for ts in tiles:
            ts["t_inv"] = ts["t_inv"] + mm(ts["t_inv"].astype(BF16), ts["pw"].astype(BF16))
    for lv in range(base_levels, levels):
        lmask = merge_masks[lv - base_levels + 1] - merge_masks[lv - base_levels]
        for ts in tiles:
            ts["tb"] = ts["t_inv"].astype(BF16)
            ts["tl"] = mm(ts["tb"], (ts["a"] * lmask).astype(BF16))
        for ts in tiles:
            ts["t_inv"] = ts["t_inv"] - mm(ts["tl"].astype(BF16), ts["tb"])
    for ts in tiles:
        ts["gam"] = jnp.exp(ts["cum"])
        ts["tb"] = ts["t_inv"].astype(BF16)
        ts["w"] = mm(ts["tb"], (ts["beta"] * ts["gam"] * ts["k"]).astype(BF16))
    for ts in tiles:
        ts["u"] = mm(ts["tb"], (ts["beta"] * ts["v"]).astype(BF16))
    segs = DN_TILE // seg
    for t, ts in enumerate(tiles):
        rows = ts["rows"]
        pk_ref[rows, PK_W * DK:(PK_W + 1) * DK] = ts["w"].astype(BF16)
        pk_ref[rows, PK_QD * DK:(PK_QD + 1) * DK] = (ts["q"] * ts["gam"]).astype(BF16)
        pk_ref[rows, PK_KD * DK:(PK_KD + 1) * DK] = (ts["k"] * jnp.exp(ts["rem"])).astype(BF16)
        pk_ref[rows, PK_QK * DK:(PK_QK + 1) * DK] = (ts["qk"] * ts["decay"]).astype(BF16)
        u_ref[rows, :] = ts["u"]
        g_end = jnp.exp(ts["cum"] + ts["rem"])
        for j in range(segs):
            ge_ref[t * segs + j] = g_end[j * seg:j * seg + 1, :]


def _delta_prep(proj, halo_src, conv_w_perm, al_row, dt_row, *, seg, t_len):
    n = proj.shape[0]
    tiles_per_step = 8 if n % (8 * DN_TILE) == 0 else 4
    tr = tiles_per_step * DN_TILE
    assert n % tr == 0
    steps = n // tr
    qkv_blk = C_QKV // HEAD_COLS
    segs = tr // seg
    if seg == DN_TILE:
        assert t_len % tr == 0
        sub_per_step = tr // SUBLANES
        halo_spec = pl.BlockSpec(
            (SUBLANES, HEAD_COLS), lambda s, h: (jnp.maximum(s * sub_per_step - 1, 0), qkv_blk + h))
    else:
        halo_spec = pl.BlockSpec((tr, HEAD_COLS), lambda s, h: (s, h))
    return pl.pallas_call(
        functools.partial(_delta_prep_kernel, seg=seg, tiles_per_step=tiles_per_step,
                          tiles_per_seq=max(t_len // DN_TILE, 1)),
        grid=(steps, DN_HEADS),
        in_specs=[
            pl.BlockSpec((tr, HEAD_COLS), lambda s, h: (s, qkv_blk + h)),
            halo_spec,
            pl.BlockSpec((tr, LANES), lambda s, h: (s, C_BA // LANES)),
            pl.BlockSpec((CONV_W, HEAD_COLS), lambda s, h: (0, h)),
            pl.BlockSpec((1, LANES), lambda s, h: (0, 0)),
            pl.BlockSpec((1, LANES), lambda s, h: (0, 0)),
        ],
        out_specs=[
            pl.BlockSpec((None, tr, 4 * DK), lambda s, h: (h, s, 0)),
            pl.BlockSpec((None, tr, DV), lambda s, h: (h, s, 0)),
            pl.BlockSpec((None, segs, 1, LANES), lambda s, h: (h, s, 0, 0)),
        ],
        out_shape=[
            jax.ShapeDtypeStruct((DN_HEADS, n, 4 * DK), BF16),
            jax.ShapeDtypeStruct((DN_HEADS, n, DV), F32),
            jax.ShapeDtypeStruct((DN_HEADS, n // seg, 1, LANES), F32),
        ],
        scratch_shapes=[pltpu.VMEM((tiles_per_step, 3, DN_TILE, LANES), F32),
                        pltpu.VMEM((SUBLANES + tr, HEAD_COLS), F32)],
        compiler_params=_params(
            ("parallel", "arbitrary"),
            pipelined=[((tr, HEAD_COLS), F32), ((tr if seg != DN_TILE else SUBLANES, HEAD_COLS), F32),
                       ((tr, LANES), F32), ((CONV_W, HEAD_COLS), F32), ((1, LANES), F32), ((1, LANES), F32),
                       ((tr, 4 * DK), BF16), ((tr, DV), F32), ((segs, 1, LANES), F32)],
            resident=[((tiles_per_step, 3, DN_TILE, LANES), F32), ((SUBLANES + tr, HEAD_COLS), F32)],
            temps=[((DN_TILE, DN_TILE), F32)] * (24 * tiles_per_step)),
        name="delta_prep",
    )(proj, halo_src, proj, conv_w_perm, al_row, dt_row)


def _out_gate(o, z, dnw):
    return _rms(o, dnw) * (z * _sigmoid(z))


def _delta_scan_prompt_kernel(pk_ref, u_ref, ge_ref, z0_ref, z1_ref, dnw_ref, o_ref, sout_ref, s_scr, *, hg):
    c = pl.program_id(2)

    @pl.when(c == 0)
    def _():
        s_scr[...] = jnp.zeros(s_scr.shape, F32)

    contract0 = (((0,), (0,)), ((), ()))
    heads = range(hg)
    res = []
    for j in heads:
        wq = jnp.concatenate([pk_ref[j, :, PK_W * DK:(PK_W + 1) * DK],
                              pk_ref[j, :, PK_QD * DK:(PK_QD + 1) * DK]], axis=0)
        res.append(jnp.dot(wq, s_scr[j].astype(BF16), preferred_element_type=F32))
    ub = [(u_ref[j] - res[j][:DN_TILE]).astype(BF16) for j in heads]
    o = [res[j][DN_TILE:] + jnp.dot(pk_ref[j, :, PK_QK * DK:(PK_QK + 1) * DK], ub[j],
                                    preferred_element_type=F32) for j in heads]
    upd = [lax.dot_general(pk_ref[j, :, PK_KD * DK:(PK_KD + 1) * DK], ub[j], contract0,
                           preferred_element_type=F32) for j in heads]
    half = hg // 2
    for j in heads:
        s_scr[j] = ge_ref[j, 0] * s_scr[j] + upd[j]
        z_ref = z0_ref if j < half else z1_ref
        zc = (j % half) * DV
        o_ref[:, j * DV:(j + 1) * DV] = _out_gate(o[j], z_ref[:, zc:zc + DV], dnw_ref[...]).astype(o_ref.dtype)

    @pl.when(c == pl.num_programs(2) - 1)
    def _():
        sout_ref[...] = s_scr[...]


def _delta_scan_prompt(pk, u, ge, proj, dnw_row, batch, t_len):
    hg = DN_HEADS
    zw = hg * DV // 2
    assert C_Z % zw == 0
    nc = t_len // DN_TILE
    zblk = C_Z // zw
    return pl.pallas_call(
        functools.partial(_delta_scan_prompt_kernel, hg=hg),
        grid=(batch, DN_HEADS // hg, nc),
        in_specs=[
            pl.BlockSpec((hg, DN_TILE, 4 * DK), lambda b, g, c: (g, b * nc + c, 0)),
            pl.BlockSpec((hg, DN_TILE, DV), lambda b, g, c: (g, b * nc + c, 0)),
            pl.BlockSpec((hg, 1, 1, LANES), lambda b, g, c: (g, b * nc + c, 0, 0)),
            pl.BlockSpec((DN_TILE, zw), lambda b, g, c: (b * nc + c, zblk)),
            pl.BlockSpec((DN_TILE, zw), lambda b, g, c: (b * nc + c, zblk + 1)),
            pl.BlockSpec((1, DV), lambda b, g, c: (0, 0)),
        ],
        out_specs=[
            pl.BlockSpec((DN_TILE, hg * DV), lambda b, g, c: (b * nc + c, g)),
            pl.BlockSpec((None, hg, DK, DV), lambda b, g, c: (b, g, 0, 0)),
        ],
        out_shape=[
            jax.ShapeDtypeStruct((batch * t_len, DN_V), BF16),
            jax.ShapeDtypeStruct((batch, DN_HEADS, DK, DV), F32),
        ],
        scratch_shapes=[pltpu.VMEM((hg, DK, DV), F32)],
        compiler_params=_params(
            ("parallel", "parallel", "arbitrary"),
            pipelined=[((hg, DN_TILE, 4 * DK), BF16), ((hg, DN_TILE, DV), F32), ((hg, 1, 1, LANES), F32),
                       ((DN_TILE, zw), F32), ((DN_TILE, zw), F32), ((1, DV), F32),
                       ((DN_TILE, hg * DV), BF16), ((hg, DK, DV), F32)],
            resident=[((hg, DK, DV), F32)],
            temps=[((hg, 2 * DN_TILE, DV), F32)] + [((hg, DN_TILE, DV), F32)] * 3),
        name="delta_scan_prompt",
    )(pk, u, ge, proj, proj, dnw_row)


def _delta_scan_sample_kernel(pk_ref, u_ref, ge_ref, z_ref, dnw_ref, s0_ref, o_ref, sout_ref,
                              ub_scr, oq_scr, pk32_scr, *, hg, t_dec, seq_unroll):
    n_seq = DN_TILE // t_dec
    contract0 = (((0,), (0,)), ((), ()))
    for j in range(hg):
        pk32_scr[j] = pk_ref[j, :, 0:3 * DK].astype(F32)

    def group(gi, carry):
        pairs = []
        for a in range(seq_unroll):
            i = gi * seq_unroll + a
            r0 = pl.multiple_of(i * t_dec, t_dec)
            pairs += [(i, r0, j) for j in range(hg)]
        res = []
        for i, r0, j in pairs:
            wq = jnp.concatenate([pk32_scr[j, pl.ds(r0, t_dec), PK_W * DK:(PK_W + 1) * DK],
                                  pk32_scr[j, pl.ds(r0, t_dec), PK_QD * DK:(PK_QD + 1) * DK]], axis=0)
            res.append(jnp.dot(wq.astype(BF16), s0_ref[i, j].astype(BF16), preferred_element_type=F32))
        upd = []
        for (i, r0, j), r in zip(pairs, res):
            uu = u_ref[j, pl.ds(r0, t_dec), :] - r[:t_dec]
            ub_scr[j, pl.ds(r0, t_dec), :] = uu
            oq_scr[j, pl.ds(r0, t_dec), :] = r[t_dec:]
            kd = pk32_scr[j, pl.ds(r0, t_dec), PK_KD * DK:(PK_KD + 1) * DK]
            upd.append(lax.dot_general(kd.astype(BF16), uu.astype(BF16), contract0,
                                       preferred_element_type=F32))
        for (i, r0, j), up in zip(pairs, upd):
            sout_ref[i, j] = ge_ref[j, i] * s0_ref[i, j] + up
        return carry

    lax.fori_loop(0, n_seq // seq_unroll, group, 0)
    o = [oq_scr[j] + jnp.dot(pk_ref[j, :, PK_QK * DK:(PK_QK + 1) * DK], ub_scr[j].astype(BF16),
                             preferred_element_type=F32) for j in range(hg)]
    for j in range(hg):
        cols = slice(j * DV, (j + 1) * DV)
        o_ref[:, cols] = _out_gate(o[j], z_ref[:, cols], dnw_ref[...]).astype(o_ref.dtype)


def _delta_scan_sample(pk, u, ge, proj, dnw_row, s0, n_seq, t_dec):
    hg = 4
    seq_per_tile = DN_TILE // t_dec
    tiles = n_seq // seq_per_tile
    zblk = C_Z // (hg * DV)
    return pl.pallas_call(
        functools.partial(_delta_scan_sample_kernel, hg=hg, t_dec=t_dec, seq_unroll=4),
        grid=(tiles, DN_HEADS // hg),
        in_specs=[
            pl.BlockSpec((hg, DN_TILE, 4 * DK), lambda r, g: (g, r, 0)),
            pl.BlockSpec((hg, DN_TILE, DV), lambda r, g: (g, r, 0)),
            pl.BlockSpec((hg, seq_per_tile, 1, LANES), lambda r, g: (g, r, 0, 0)),
            pl.BlockSpec((DN_TILE, hg * DV), lambda r, g: (r, zblk + g)),
            pl.BlockSpec((1, DV), lambda r, g: (0, 0)),
            pl.BlockSpec((seq_per_tile, hg, DK, DV), lambda r, g: (r, g, 0, 0)),
        ],
        out_specs=[
            pl.BlockSpec((DN_TILE, hg * DV), lambda r, g: (r, g)),
            pl.BlockSpec((seq_per_tile, hg, DK, DV), lambda r, g: (r, g, 0, 0)),
        ],
        out_shape=[
            jax.ShapeDtypeStruct((n_seq * t_dec, DN_V), BF16),
            jax.ShapeDtypeStruct((n_seq, DN_HEADS, DK, DV), F32),
        ],
        scratch_shapes=[pltpu.VMEM((hg, DN_TILE, DV), F32), pltpu.VMEM((hg, DN_TILE, DV), F32),
                        pltpu.VMEM((hg, DN_TILE, 3 * DK), F32)],
        compiler_params=_params(
            ("parallel", "parallel"),
            pipelined=[((hg, DN_TILE, 4 * DK), BF16), ((hg, DN_TILE, DV), F32), ((hg, seq_per_tile, 1, LANES), F32),
                       ((DN_TILE, hg * DV), F32), ((1, DV), F32), ((seq_per_tile, hg, DK, DV), F32),
                       ((DN_TILE, hg * DV), BF16), ((seq_per_tile, hg, DK, DV), F32)],
            resident=[((hg, DN_TILE, DV), F32)] * 2 + [((hg, DN_TILE, 3 * DK), F32)],
            temps=[((DK, DV), F32)] * 32),
        name="delta_scan_sample",
    )(pk, u, ge, proj, dnw_row, s0)


def _mix_out_kernel(oa_ref, od_ref, wa_ref, wd_ref, wo_ref, ga0_ref, ga1_ref, gd0_ref, gd1_ref, x_ref, g_ref,
                    o_ref):
    ya = jnp.dot(oa_ref[...], wa_ref[...], preferred_element_type=F32)
    yd = jnp.dot(od_ref[...], wd_ref[...], preferred_element_type=F32)
    half = D_MODEL // 2
    mixed = jnp.concatenate(
        [_sigmoid(ga0_ref[...]) * ya[:, :half] + _sigmoid(gd0_ref[...]) * yd[:, :half],
         _sigmoid(ga1_ref[...]) * ya[:, half:] + _sigmoid(gd1_ref[...]) * yd[:, half:]], axis=1)
    t = jnp.dot(mixed.astype(BF16), wo_ref[...], preferred_element_type=F32)
    o_ref[...] = x_ref[...] + _rms(t, g_ref[...])


def _mix_out(oa, od, wa, wd, wo, proj, x2d, g_row):
    n = oa.shape[0]
    tm = 256
    half = D_MODEL // 2

    def resident(shape):
        return pl.BlockSpec(shape, lambda i: (0, 0), pipeline_mode=pl.Buffered(1))

    def gate(col):
        return pl.BlockSpec((tm, half), lambda i, c=col // half: (i, c))

    return pl.pallas_call(
        _mix_out_kernel,
        grid=(n // tm,),
        in_specs=[
            pl.BlockSpec((tm, Q_A), lambda i: (i, 0)),
            pl.BlockSpec((tm, DN_V), lambda i: (i, 0)),
            resident((Q_A, D_MODEL)),
            resident((DN_V, D_MODEL)),
            resident((D_MODEL, D_MODEL)),
            gate(C_GA), gate(C_GA + half), gate(C_GD), gate(C_GD + half),
            pl.BlockSpec((tm, D_MODEL), lambda i: (i, 0)),
            pl.BlockSpec((1, D_MODEL), lambda i: (0, 0)),
        ],
        out_specs=pl.BlockSpec((tm, D_MODEL), lambda i: (i, 0)),
        out_shape=jax.ShapeDtypeStruct((n, D_MODEL), F32),
        compiler_params=_params(
            ("parallel",),
            pipelined=[((tm, Q_A), BF16), ((tm, DN_V), BF16)] + [((tm, half), F32)] * 4
            + [((tm, D_MODEL), F32), ((1, D_MODEL), F32), ((tm, D_MODEL), F32)],
            resident=[((D_MODEL, D_MODEL), BF16)] * 3,
            temps=[((tm, D_MODEL), F32)] * 3 + [((tm, D_MODEL), BF16)]),
        name="mix_out",
    )(oa, od, wa, wd, wo, proj, proj, proj, proj, x2d, g_row)


def _mlp_kernel(x_ref, gpre_ref, wu_ref, wd_ref, gpost_ref, o_ref, h_ref, acc_ref):
    f = pl.program_id(1)

    @pl.when(f == 0)
    def _():
        h_ref[...] = _rms(x_ref[...], gpre_ref[...]).astype(BF16)
        acc_ref[...] = jnp.zeros(acc_ref.shape, F32)

    up = jnp.dot(h_ref[...], wu_ref[...], preferred_element_type=F32)
    act = jnp.square(jnp.maximum(up, 0.0)).astype(BF16)
    acc_ref[...] += jnp.dot(act, wd_ref[...], preferred_element_type=F32)

    @pl.when(f == pl.num_programs(1) - 1)
    def _():
        o_ref[...] = x_ref[...] + _rms(acc_ref[...], gpost_ref[...])


def _mlp(x2d, gpre, w_up, w_down, gpost):
    n = x2d.shape[0]
    tm, tf = 512, 1024
    return pl.pallas_call(
        _mlp_kernel,
        grid=(n // tm, D_FF // tf),
        in_specs=[
            pl.BlockSpec((tm, D_MODEL), lambda i, f: (i, 0)),
            pl.BlockSpec((1, D_MODEL), lambda i, f: (0, 0)),
            pl.BlockSpec((D_MODEL, tf), lambda i, f: (0, f)),
            pl.BlockSpec((tf, D_MODEL), lambda i, f: (f, 0)),
            pl.BlockSpec((1, D_MODEL), lambda i, f: (0, 0)),
        ],
        out_specs=pl.BlockSpec((tm, D_MODEL), lambda i, f: (i, 0)),
        out_shape=jax.ShapeDtypeStruct((n, D_MODEL), F32),
        scratch_shapes=[pltpu.VMEM((tm, D_MODEL), BF16), pltpu.VMEM((tm, D_MODEL), F32)],
        compiler_params=_params(
            ("parallel", "arbitrary"),
            pipelined=[((tm, D_MODEL), F32), ((1, D_MODEL), F32), ((D_MODEL, tf), BF16), ((tf, D_MODEL), BF16),
                       ((1, D_MODEL), F32), ((tm, D_MODEL), F32)],
            resident=[((tm, D_MODEL), BF16), ((tm, D_MODEL), F32)],
            temps=[((tm, tf), F32), ((tm, tf), BF16), ((tm, D_MODEL), F32)]),
        name="mlp",
    )(x2d, gpre, w_up, w_down, gpost)


def _head_major(x):
    lead = x.shape[:-1]
    return jnp.swapaxes(x.reshape(*lead, 3, DN_HEADS, DK), -3, -2).reshape(*lead, CONV_DIM)


def _part_major(x):
    lead = x.shape[:-1]
    return jnp.swapaxes(x.reshape(*lead, DN_HEADS, 3, DK), -3, -2).reshape(*lead, CONV_DIM)


W_QKV = Q_A + 2 * KV_A
W_Z = W_QKV + CONV_DIM
W_BA = W_Z + DN_V
W_GA = W_BA + 2 * DN_HEADS
W_DIM = W_GA + 2 * D_MODEL


def _pack_w_in_kernel(w_ref, o_ref):
    def cp(dst, src, n):
        o_ref[dst:dst + n, :] = w_ref[src:src + n, :].astype(BF16)

    cp(C_Q, 0, W_QKV)
    for h in range(DN_HEADS):
        for p in range(3):
            cp(C_QKV + h * HEAD_COLS + p * DK, W_QKV + p * DN_QK + h * DK, DK)
    cp(C_Z, W_Z, DN_V)
    cp(C_GA, W_GA, 2 * D_MODEL)
    o_ref[C_BA:P_DIM, :] = jnp.zeros((P_DIM - C_BA, o_ref.shape[1]), BF16)
    cp(C_BA, W_BA, 2 * DN_HEADS)


def _pack_w_in(w_in_t):
    tc = 256
    return pl.pallas_call(
        _pack_w_in_kernel,
        grid=(D_MODEL // tc,),
        in_specs=[pl.BlockSpec((W_DIM, tc), lambda i: (0, i))],
        out_specs=pl.BlockSpec((P_DIM, tc), lambda i: (0, i)),
        out_shape=jax.ShapeDtypeStruct((P_DIM, D_MODEL), BF16),
        compiler_params=_params(("parallel",), pipelined=[((W_DIM, tc), F32), ((P_DIM, tc), BF16)]),
        name="pack_w_in",
    )(w_in_t)


def _lane_row(vals, offset):
    return jnp.zeros((1, LANES), F32).at[0, offset:offset + vals.shape[0]].set(vals.astype(F32))


def _layer(x2d, attend, delta, w):
    proj = _in_proj(x2d, w["g_mix_pre"], w["w_in"])
    o_a = attend(proj)
    o_d, s_final = delta(proj)
    x1 = _mix_out(o_a, o_d, w["w_ba"], w["w_bd"], w["w_out"], proj, x2d, w["g_mix_post"])
    y = _mlp(x1, w["g_mlp_pre"], w["w_up"], w["w_down"], w["g_mlp_post"])
    return y, proj, s_final


def kernel(x_prompt, x_sample, cache_k, cache_v, state_delta, state_conv, page_table, w_in, conv_w, a_log,
           dt_bias, dn_norm_w, w_branch_attn, w_branch_delta, w_out, g_mix_pre, g_mix_post, g_mlp_pre,
           g_mlp_post, w_up, w_down):
    depth = w_in.shape[0]
    assert depth == 1
    b_p, t_p, _ = x_prompt.shape
    b_s, t_s, _ = x_sample.shape
    l = 0
    w = {
        "w_in": _pack_w_in(jnp.transpose(w_in[l])),
        "g_mix_pre": g_mix_pre[l][None, :],
        "g_mix_post": g_mix_post[l][None, :],
        "g_mlp_pre": g_mlp_pre[l][None, :],
        "g_mlp_post": g_mlp_post[l][None, :],
        "w_ba": w_branch_attn[l].astype(BF16),
        "w_bd": w_branch_delta[l].astype(BF16),
        "w_out": w_out[l].astype(BF16),
        "w_up": w_up[l].astype(BF16),
        "w_down": w_down[l].astype(BF16),
    }
    cw = _head_major(conv_w[l])
    al_row = _lane_row(a_log[l], DN_HEADS)
    dt_row = _lane_row(dt_bias[l], DN_HEADS)
    dnw_row = dn_norm_w[l][None, :]

    kv_prompt = []

    def attend_p(proj):
        kaug, vb, kmean, k5, v5 = _kprep(proj, b_p, t_p)
        kv_prompt.extend([k5[None], v5[None]])
        return _moba_prompt(proj, kaug, vb, kmean, b_p, t_p)

    def delta_p(proj):
        pk, u, ge = _delta_prep(proj, proj, cw, al_row, dt_row, seg=DN_TILE, t_len=t_p)
        return _delta_scan_prompt(pk, u, ge, proj, dnw_row, b_p, t_p)

    y_p, proj_p, d_p = _layer(x_prompt.reshape(b_p * t_p, D_MODEL), attend_p, delta_p, w)

    n_pool = cache_k.shape[1]
    ck2d = cache_k[l].reshape(n_pool * PAGE * N_KV, HD)
    cv2d = cache_v[l].reshape(n_pool * PAGE * N_KV, HD)
    conv_rows = _head_major(state_conv[l])
    halo_s = jnp.pad(conv_rows, ((0, 0), (t_s - (CONV_W - 1), 0), (0, 0))).reshape(b_s * t_s, CONV_DIM)

    def attend_s(proj):
        return _moba_sample(proj, ck2d, cv2d, page_table, b_s, t_s)

    def delta_s(proj):
        pk, u, ge = _delta_prep(proj, halo_s, cw, al_row, dt_row, seg=t_s, t_len=t_s)
        return _delta_scan_sample(pk, u, ge, proj, dnw_row, state_delta[l], b_s, t_s)

    y_s, proj_s, d_s = _layer(x_sample.reshape(b_s * t_s, D_MODEL), attend_s, delta_s, w)

    def kv_out(proj, b, t):
        k = proj[:, C_K:C_K + KV_A].reshape(1, b, t, N_KV, HD)
        v = proj[:, C_V:C_V + KV_A].reshape(1, b, t, N_KV, HD)
        return k, v

    def conv_out(proj, b, t):
        raw = proj.reshape(b, t, P_DIM)[:, t - (CONV_W - 1):, C_QKV:C_QKV + CONV_DIM]
        return _part_major(raw)[None]

    k_p, v_p = kv_prompt
    k_s, v_s = kv_out(proj_s, b_s, t_s)
    return (y_p.reshape(b_p, t_p, D_MODEL), y_s.reshape(b_s, t_s, D_MODEL), k_p, v_p, d_p[None],
            conv_out(proj_p, b_p, t_p), k_s, v_s, d_s[None], conv_out(proj_s, b_s, t_s))
```

```python
import functools
import math

import numpy as np
import jax
import jax.numpy as jnp
from jax import lax
from jax.experimental import pallas as pl
from jax.experimental.pallas import tpu as pltpu

F32 = jnp.float32
BF16 = jnp.bfloat16

D_MODEL = 2048
N_HEADS = 16
N_KV = 4
HD = 128
GROUP = N_HEADS // N_KV
MOBA_BLOCK = 256
MOBA_TOPK = 3
PAGE = 128
DN_HEADS = 16
DK = 128
DV = 128
CONV_W = 4
Q_A = N_HEADS * HD
KV_A = N_KV * HD
DN_QK = DN_HEADS * DK
DN_V = DN_HEADS * DV
CONV_DIM = 2 * DN_QK + DN_V
D_FF = 4 * D_MODEL
EPS = 1e-6

LANES = 128
SUBLANES = 8

C_Q = 0
C_K = C_Q + Q_A
C_V = C_K + KV_A
C_QKV = C_V + KV_A
C_Z = C_QKV + CONV_DIM
C_GA = C_Z + DN_V
C_GD = C_GA + D_MODEL
C_BA = C_GD + D_MODEL
P_DIM = C_BA + LANES
HEAD_COLS = 3 * DK

NEG = -1e30
SCALE = HD ** -0.5
EXP2_C = SCALE * math.log2(math.e)


V7X_VMEM_BUDGET = 56 * 2**20


def _nbytes(shape, dtype):
    return math.prod(shape) * jnp.dtype(dtype).itemsize


def _params(sem, pipelined, resident=(), temps=()):
    need = 2 * sum(_nbytes(*b) for b in pipelined) + sum(_nbytes(*b) for b in (*resident, *temps))
    limit = -(-need // 2**20) * 2**20
    assert limit <= V7X_VMEM_BUDGET, (limit, V7X_VMEM_BUDGET)
    return pltpu.CompilerParams(dimension_semantics=sem, vmem_limit_bytes=limit)


def _rms(x, g):
    return x * lax.rsqrt(jnp.mean(x * x, axis=-1, keepdims=True) + EPS) * g


def _sigmoid(x):
    return 1.0 / (1.0 + jnp.exp(-x))


def _in_proj_kernel(x_ref, g_ref, w_ref, o_ref, h_ref):
    @pl.when(pl.program_id(1) == 0)
    def _():
        h_ref[...] = _rms(x_ref[...], g_ref[...]).astype(BF16)

    o_ref[...] = lax.dot_general(h_ref[...], w_ref[...], (((1,), (1,)), ((), ())), preferred_element_type=F32)


def _in_proj(x2d, g_row, wp):
    n = x2d.shape[0]
    tm = 1024 if n % 1024 == 0 else 512
    tn = 1408
    assert n % tm == 0 and P_DIM % tn == 0
    return pl.pallas_call(
        _in_proj_kernel,
        grid=(n // tm, P_DIM // tn),
        in_specs=[
            pl.BlockSpec((tm, D_MODEL), lambda i, j: (i, 0)),
            pl.BlockSpec((1, D_MODEL), lambda i, j: (0, 0)),
            pl.BlockSpec((tn, D_MODEL), lambda i, j: (j, 0)),
        ],
        out_specs=pl.BlockSpec((tm, tn), lambda i, j: (i, j)),
        out_shape=jax.ShapeDtypeStruct((n, P_DIM), F32),
        scratch_shapes=[pltpu.VMEM((tm, D_MODEL), BF16)],
        compiler_params=_params(
            ("parallel", "arbitrary"),
            pipelined=[((tm, D_MODEL), F32), ((1, D_MODEL), F32), ((tn, D_MODEL), BF16), ((tm, tn), F32)],
            resident=[((tm, D_MODEL), BF16)], temps=[((tm, tn), F32)]),
        name="in_proj",
    )(x2d, g_row, wp)


N_SLOPE_FEATS = 4
F_SEL = 0
F_HI = 16
F_LO = F_HI + N_SLOPE_FEATS
MAX_BLOCKS = 16


def _kprep_kernel(k_ref, v_ref, kaug_ref, vb_ref, kmean_ref, k5_ref, v5_ref, *, blocks_per_step):
    step = pl.program_id(1)
    lane = lax.broadcasted_iota(jnp.int32, (MOBA_BLOCK, LANES), 1)
    row = lax.broadcasted_iota(jnp.int32, (MOBA_BLOCK, LANES), 0).astype(F32)
    is_hi = jnp.where(lane >= F_HI, jnp.where(lane < F_LO, 1.0, 0.0), 0.0)
    is_lo = jnp.where(lane >= F_LO, jnp.where(lane < F_LO + N_SLOPE_FEATS, 1.0, 0.0), 0.0)
    ones_col = jnp.where(lane == 0, 1.0, 0.0).astype(BF16)
    for j in range(blocks_per_step):
        n = step * blocks_per_step + j
        rows = slice(j * MOBA_BLOCK, (j + 1) * MOBA_BLOCK)
        k = k_ref[rows, :]
        v = v_ref[rows, :]
        mean = jnp.mean(k, axis=0, keepdims=True)
        feat = (jnp.where(lane == n, 1.0, 0.0)
                + is_hi * (n * MOBA_BLOCK).astype(F32)
                + is_lo * row).astype(BF16)
        for c in range(N_KV):
            cols = slice(c * HD, (c + 1) * HD)
            kaug_ref[c, rows, 0:HD] = k[:, cols].astype(BF16)
            kaug_ref[c, rows, HD:2 * HD] = feat
            vb_ref[c, rows, 0:HD] = v[:, cols].astype(BF16)
            vb_ref[c, rows, HD:2 * HD] = ones_col
            kmean_ref[c, j:j + 1, :] = mean[:, cols]
            k5_ref[rows, c, :] = k[:, cols]
            v5_ref[rows, c, :] = v[:, cols]


def _kprep(proj, batch, t_len):
    nblk = t_len // MOBA_BLOCK
    bps = min(nblk, 8)
    assert nblk % bps == 0 and nblk <= MAX_BLOCKS
    steps = nblk // bps
    rows = bps * MOBA_BLOCK
    return pl.pallas_call(
        functools.partial(_kprep_kernel, blocks_per_step=bps),
        grid=(batch, steps),
        in_specs=[
            pl.BlockSpec((rows, KV_A), lambda b, s: (b * steps + s, C_K // KV_A)),
            pl.BlockSpec((rows, KV_A), lambda b, s: (b * steps + s, C_V // KV_A)),
        ],
        out_specs=[
            pl.BlockSpec((None, N_KV, rows, 2 * HD), lambda b, s: (b, 0, s, 0)),
            pl.BlockSpec((None, N_KV, rows, 2 * HD), lambda b, s: (b, 0, s, 0)),
            pl.BlockSpec((None, N_KV, bps, HD), lambda b, s: (b, 0, s, 0)),
            pl.BlockSpec((None, rows, N_KV, HD), lambda b, s: (b, s, 0, 0)),
            pl.BlockSpec((None, rows, N_KV, HD), lambda b, s: (b, s, 0, 0)),
        ],
        out_shape=[
            jax.ShapeDtypeStruct((batch, N_KV, t_len, 2 * HD), BF16),
            jax.ShapeDtypeStruct((batch, N_KV, t_len, 2 * HD), BF16),
            jax.ShapeDtypeStruct((batch, N_KV, nblk, HD), F32),
            jax.ShapeDtypeStruct((batch, t_len, N_KV, HD), F32),
            jax.ShapeDtypeStruct((batch, t_len, N_KV, HD), F32),
        ],
        compiler_params=_params(
            ("parallel", "arbitrary"),
            pipelined=[((rows, KV_A), F32)] * 2 + [((N_KV, rows, 2 * HD), BF16)] * 2
            + [((N_KV, bps, HD), F32)] + [((rows, N_KV, HD), F32)] * 2),
        name="kprep",
    )(proj, proj)


def _slope_pieces():
    slopes = np.asarray(2.0 ** (-8.0 * np.arange(1, N_HEADS + 1) / N_HEADS), np.float32).astype(np.float64)
    x = slopes / SCALE
    pieces = []
    for _ in range(N_SLOPE_FEATS):
        p = x.astype(np.float32).astype(jnp.bfloat16).astype(np.float64)
        pieces.append(p)
        x = x - p
    return np.stack(pieces, axis=1)


def _slope_feature_table():
    pieces = _slope_pieces()
    tab = np.zeros((N_KV, 2 * N_SLOPE_FEATS, GROUP * MOBA_BLOCK), np.float32)
    for c in range(N_KV):
        for g in range(GROUP):
            cols = slice(g * MOBA_BLOCK, (g + 1) * MOBA_BLOCK)
            for f in range(N_SLOPE_FEATS):
                tab[c, f, cols] = pieces[c * GROUP + g, f]
                tab[c, N_SLOPE_FEATS + f, cols] = pieces[c * GROUP + g, f]
    return jnp.asarray(tab)


def _select_bias(gate_t, own):
    nblk = gate_t.shape[0]
    blk = lax.broadcasted_iota(jnp.int32, gate_t.shape, 0)
    past = blk < own
    gm = jnp.where(past, gate_t, -jnp.inf)
    rank = jnp.zeros(gate_t.shape, F32)
    for m in range(nblk):
        row = gm[m:m + 1, :]
        tie = jnp.where(blk > m, 1.0, 0.0)
        rank = rank + jnp.where(row > gm, 1.0, jnp.where(row == gm, tie, 0.0))
    keep_past = jnp.where(past, jnp.where(rank < MOBA_TOPK - 0.5, 0.0, NEG), NEG)
    return jnp.where(blk == own, 0.0, keep_past)


Q_TILES = 2


def _moba_prompt_kernel(q_ref, kmean_ref, sf_ref, kaug_ref, vb_ref, o_ref,
                        feat_t, qaug, m_s, acc_s, s_a, s_b, s_c, s_d, *, nblk):
    i0 = pl.program_id(2) * Q_TILES
    tile = GROUP * MOBA_BLOCK
    rows = Q_TILES * tile
    q = q_ref[...]
    qs = jnp.concatenate([q[a * MOBA_BLOCK:(a + 1) * MOBA_BLOCK, g * HD:(g + 1) * HD]
                          for a in range(Q_TILES) for g in range(GROUP)], axis=0).astype(BF16)
    qaug[:, 0:HD] = qs
    gate_t = lax.dot_general(kmean_ref[...].astype(BF16), qs, (((1,), (1,)), ((), ())),
                             preferred_element_type=F32)
    own = i0 + lax.broadcasted_iota(jnp.int32, (1, rows), 1) // tile
    feat_t[...] = jnp.zeros(feat_t.shape, F32)
    feat_t[F_SEL:F_SEL + nblk, :] = _select_bias(gate_t, own)
    for a in range(Q_TILES):
        feat_t[F_HI:F_HI + 2 * N_SLOPE_FEATS, a * tile:(a + 1) * tile] = sf_ref[...]
    qaug[:, HD:2 * HD] = feat_t[...].T.astype(BF16)

    def key_rows(n):
        return pl.ds(pl.multiple_of(n * MOBA_BLOCK, MOBA_BLOCK), MOBA_BLOCK)

    def raw_scores(n, r0=0):
        return lax.dot_general(qaug[r0:, :], kaug_ref[key_rows(n), :], (((1,), (1,)), ((), ())),
                               preferred_element_type=F32)

    def pair_scores(pr, bufs):
        n0 = jnp.maximum(jnp.minimum(2 * pr, i0 - 2), 0)
        bufs[0][...] = raw_scores(n0)
        bufs[1][...] = raw_scores(n0 + 1)

    def probs(s, m):
        return jnp.concatenate([jnp.exp2((s[:, :LANES] - m) * EXP2_C),
                                jnp.exp2((s[:, LANES:] - m) * EXP2_C)], axis=1).astype(BF16)

    def row_max(s):
        cur = jnp.max(jnp.maximum(s[:, :LANES], s[:, LANES:]), axis=-1, keepdims=True)
        return jnp.broadcast_to(cur, (s.shape[0], LANES))

    def update(s, n, r0=0):
        m_old = m_s[r0:, :]
        m_new = jnp.maximum(m_old, row_max(s))
        alpha = jnp.exp2((m_old - m_new) * EXP2_C)
        pv = jnp.dot(probs(s, m_new), vb_ref[key_rows(n), :], preferred_element_type=F32)
        acc_s[r0:, :] = jnp.concatenate([alpha, alpha], axis=1) * acc_s[r0:, :] + pv
        m_s[r0:, :] = m_new

    def causal(s):
        qi = lax.broadcasted_iota(jnp.int32, s.shape, 0) & (MOBA_BLOCK - 1)
        kj = lax.broadcasted_iota(jnp.int32, s.shape, 1)
        return kj <= qi

    m_s[...] = jnp.full(m_s.shape, NEG, F32)
    acc_s[...] = jnp.zeros(acc_s.shape, F32)
    for a in reversed(range(Q_TILES)):
        s = raw_scores(i0 + a, a * tile)
        if a == 0:
            pair_scores(0, (s_a, s_b))
        own_rows = lax.broadcasted_iota(jnp.int32, s.shape, 0) < tile
        s = jnp.where(causal(s), s, jnp.where(own_rows, NEG, s))
        update(s, i0 + a, a * tile)

    n_pairs = i0 // 2

    def pair_update(pr, bufs):
        s0 = bufs[0][...]
        s1 = bufs[1][...]
        m_old = m_s[...]
        m_new = jnp.maximum(jnp.maximum(m_old, row_max(s0)), row_max(s1))
        alpha = jnp.exp2((m_old - m_new) * EXP2_C)
        p = jnp.concatenate([probs(s0, m_new), probs(s1, m_new)], axis=1)
        v2 = vb_ref[pl.ds(pl.multiple_of(pr * 2 * MOBA_BLOCK, 2 * MOBA_BLOCK), 2 * MOBA_BLOCK), :]
        pv = jnp.dot(p, v2, preferred_element_type=F32)
        acc_s[...] = jnp.concatenate([alpha, alpha], axis=1) * acc_s[...] + pv
        m_s[...] = m_new

    def body(j, carry):
        pr = 2 * j
        pair_scores(pr + 1, (s_c, s_d))
        pair_update(pr, (s_a, s_b))
        pair_scores(pr + 2, (s_a, s_b))
        pair_update(pr + 1, (s_c, s_d))
        return carry

    lax.fori_loop(0, n_pairs // 2, body, 0)

    @pl.when(n_pairs % 2 == 1)
    def _():
        pair_update(n_pairs - 1, (s_a, s_b))

    acc = acc_s[...]
    out = acc[:, 0:HD] / acc[:, HD:HD + 1]
    for a in range(Q_TILES):
        for g in range(GROUP):
            r = a * tile + g * MOBA_BLOCK
            o_ref[a * MOBA_BLOCK:(a + 1) * MOBA_BLOCK, g * HD:(g + 1) * HD] = (
                out[r:r + MOBA_BLOCK, :].astype(o_ref.dtype))


def _moba_prompt(proj, kaug, vb, kmean, batch, t_len):
    nblk = t_len // MOBA_BLOCK
    assert nblk % Q_TILES == 0 and Q_TILES % 2 == 0
    nq = nblk // Q_TILES
    tile = GROUP * MOBA_BLOCK
    rows = Q_TILES * tile
    qrows = Q_TILES * MOBA_BLOCK
    qcols = GROUP * HD
    return pl.pallas_call(
        functools.partial(_moba_prompt_kernel, nblk=nblk),
        grid=(batch, N_KV, nq),
        in_specs=[
            pl.BlockSpec((qrows, qcols), lambda b, c, i: (b * nq + i, C_Q // qcols + c)),
            pl.BlockSpec((None, None, nblk, HD), lambda b, c, i: (b, c, 0, 0)),
            pl.BlockSpec((None, 2 * N_SLOPE_FEATS, tile), lambda b, c, i: (c, 0, 0)),
            pl.BlockSpec((None, None, t_len, 2 * HD), lambda b, c, i: (b, c, 0, 0)),
            pl.BlockSpec((None, None, t_len, 2 * HD), lambda b, c, i: (b, c, 0, 0)),
        ],
        out_specs=pl.BlockSpec((qrows, qcols), lambda b, c, i: (b * nq + i, c)),
        out_shape=jax.ShapeDtypeStruct((batch * t_len, Q_A), BF16),
        scratch_shapes=[
            pltpu.VMEM((LANES, rows), F32),
            pltpu.VMEM((rows, 2 * HD), BF16),
            pltpu.VMEM((rows, LANES), F32),
            pltpu.VMEM((rows, 2 * HD), F32),
            pltpu.VMEM((rows, MOBA_BLOCK), F32),
            pltpu.VMEM((rows, MOBA_BLOCK), F32),
            pltpu.VMEM((rows, MOBA_BLOCK), F32),
            pltpu.VMEM((rows, MOBA_BLOCK), F32),
        ],
        compiler_params=_params(
            ("parallel", "parallel", "arbitrary"),
            pipelined=[((qrows, qcols), F32), ((nblk, HD), F32), ((2 * N_SLOPE_FEATS, tile), F32),
                       ((t_len, 2 * HD), BF16), ((t_len, 2 * HD), BF16), ((qrows, qcols), BF16)],
            resident=[((LANES, rows), F32), ((rows, 2 * HD), BF16), ((rows, LANES), F32), ((rows, 2 * HD), F32)]
            + [((rows, MOBA_BLOCK), F32)] * 4,
            temps=[((rows, MOBA_BLOCK), F32)] * 3 + [((rows, 2 * MOBA_BLOCK), BF16)]),
        name="moba_prompt",
    )(proj, kmean, _slope_feature_table(), kaug, vb)


def _moba_sample_kernel(pt_ref, q_ref, kn_ref, vn_ref, slope_ref, *rest, n_pages, t_dec):
    del pt_ref
    k_pages = rest[:n_pages]
    v_pages = rest[n_pages:2 * n_pages]
    o_ref = rest[2 * n_pages]
    s_scr, kb_scr = rest[2 * n_pages + 1:]
    past = n_pages * PAGE
    nblk = past // MOBA_BLOCK
    pages_per_blk = MOBA_BLOCK // PAGE
    rows = N_HEADS * t_dec
    assert rows == LANES

    q = q_ref[...]
    q_rows = jnp.concatenate([q[:, h * HD:(h + 1) * HD] for h in range(N_HEADS)], axis=0)
    q_t = q_rows.T.astype(BF16)
    lane = lax.broadcasted_iota(jnp.int32, (HD, LANES), 1)
    rows_per_kv = GROUP * t_dec
    zero = jnp.zeros((HD, LANES), BF16)
    q_bd = [jnp.where(lane // rows_per_kv == c, q_t, zero) for c in range(N_KV)]

    def scores_t(k2d_rows):
        acc = None
        for c in range(N_KV):
            part = jnp.dot(k2d_rows(c), q_bd[c], preferred_element_type=F32)
            acc = part if acc is None else acc + part
        return acc

    for n in range(nblk):
        sums = [jnp.zeros((1, HD), F32) for _ in range(N_KV)]
        for pp in range(pages_per_blk):
            p = n * pages_per_blk + pp
            kc = []
            for c in range(N_KV):
                kf = k_pages[p][pl.ds(c, PAGE, stride=N_KV), :]
                sums[c] = sums[c] + jnp.sum(kf, axis=0, keepdims=True)
                kc.append(kf.astype(BF16))
            s_scr[p * PAGE:(p + 1) * PAGE, :] = scores_t(lambda c: kc[c])
        for c in range(N_KV):
            kb_scr[c, n:n + 1, :] = sums[c] * (1.0 / MOBA_BLOCK)
    gate_t = scores_t(lambda c: kb_scr[c].astype(BF16))
    sel = _select_bias(gate_t, nblk)

    slope = slope_ref[...]
    t_q = lax.broadcasted_iota(jnp.int32, (1, LANES), 1) % t_dec
    q_pos = (past + t_q).astype(F32)

    def logits(raw, k_pos):
        return raw * SCALE - slope * (q_pos - k_pos)

    kn = kn_ref[...]
    vn = vn_ref[...]
    knc = [kn[:, c * HD:(c + 1) * HD].astype(BF16) for c in range(N_KV)]
    t_k = lax.broadcasted_iota(jnp.int32, (t_dec, LANES), 0)
    s_own = logits(scores_t(lambda c: knc[c]), (past + t_k).astype(F32))
    s_own = jnp.where(t_k <= t_q, s_own, NEG)
    m = jnp.max(s_own, axis=0, keepdims=True)

    sub = lax.broadcasted_iota(jnp.int32, (PAGE, LANES), 0)
    for p in range(n_pages):
        n = p // pages_per_blk
        k_pos = (sub + p * PAGE).astype(F32)
        s = logits(s_scr[p * PAGE:(p + 1) * PAGE, :], k_pos) + sel[n:n + 1, :]
        s_scr[p * PAGE:(p + 1) * PAGE, :] = s
        m = jnp.maximum(m, jnp.max(s, axis=0, keepdims=True))

    p_own = jnp.exp(s_own - m)
    l = jnp.sum(p_own, axis=0, keepdims=True)
    row_kv = lax.broadcasted_iota(jnp.int32, (LANES, HD), 0) // rows_per_kv
    contract0 = (((0,), (0,)), ((), ()))
    p_own_b = p_own.astype(BF16)
    acc = jnp.zeros((LANES, HD), F32)
    for c in range(N_KV):
        part = lax.dot_general(p_own_b, vn[:, c * HD:(c + 1) * HD].astype(BF16), contract0,
                               preferred_element_type=F32)
        acc = acc + jnp.where(row_kv == c, part, 0.0)
    for p in range(n_pages):
        pr = jnp.exp(s_scr[p * PAGE:(p + 1) * PAGE, :] - m)
        l = l + jnp.sum(pr, axis=0, keepdims=True)
        pb = pr.astype(BF16)
        for c in range(N_KV):
            vf = v_pages[p][pl.ds(c, PAGE, stride=N_KV), :].astype(BF16)
            part = lax.dot_general(pb, vf, contract0, preferred_element_type=F32)
            acc = acc + jnp.where(row_kv == c, part, 0.0)
    l_col = jnp.broadcast_to(l, (LANES, LANES)).T
    out = acc / l_col
    for h in range(N_HEADS):
        o_ref[:, h * HD:(h + 1) * HD] = out[h * t_dec:(h + 1) * t_dec, :].astype(o_ref.dtype)


def _moba_sample(proj, cache_k2d, cache_v2d, page_table, n_seq, t_dec):
    n_pages = page_table.shape[1]
    past = n_pages * PAGE
    nblk = past // MOBA_BLOCK
    page_rows = PAGE * N_KV
    slopes = np.asarray(2.0 ** (-8.0 * np.arange(1, N_HEADS + 1) / N_HEADS), np.float32)
    slope_row = jnp.asarray(np.repeat(slopes, t_dec)[None, :])

    def page_spec(p):
        return pl.BlockSpec((page_rows, HD), lambda b, pt, p=p: (pt[b, p], 0))

    grid_spec = pltpu.PrefetchScalarGridSpec(
        num_scalar_prefetch=1,
        grid=(n_seq,),
        in_specs=[
            pl.BlockSpec((t_dec, Q_A), lambda b, pt: (b, C_Q // Q_A)),
            pl.BlockSpec((t_dec, KV_A), lambda b, pt: (b, C_K // KV_A)),
            pl.BlockSpec((t_dec, KV_A), lambda b, pt: (b, C_V // KV_A)),
            pl.BlockSpec((1, LANES), lambda b, pt: (0, 0)),
        ] + [page_spec(p) for p in range(n_pages)] * 2,
        out_specs=pl.BlockSpec((t_dec, Q_A), lambda b, pt: (b, 0)),
        scratch_shapes=[
            pltpu.VMEM((past, LANES), F32),
            pltpu.VMEM((N_KV, nblk, HD), F32),
        ],
    )
    return pl.pallas_call(
        functools.partial(_moba_sample_kernel, n_pages=n_pages, t_dec=t_dec),
        grid_spec=grid_spec,
        out_shape=jax.ShapeDtypeStruct((n_seq * t_dec, Q_A), BF16),
        compiler_params=_params(
            ("arbitrary",),
            pipelined=[((t_dec, Q_A), F32), ((t_dec, KV_A), F32), ((t_dec, KV_A), F32), ((1, LANES), F32),
                       ((t_dec, Q_A), BF16)] + [((page_rows, HD), F32)] * (2 * n_pages),
            resident=[((past, LANES), F32), ((N_KV, nblk, HD), F32)],
            temps=[((PAGE, LANES), F32)] * 16),
        name="moba_sample",
    )(page_table, proj, proj, proj, slope_row, *([cache_k2d] * n_pages), *([cache_v2d] * n_pages))


DN_TILE = 128
PK_W, PK_QD, PK_KD, PK_QK = 0, 1, 2, 3
INV_BASE_LEVELS = 3


def _conv_silu(x, w, shifted):
    y = x * w[CONV_W - 1:CONV_W, :]
    for d in range(1, CONV_W):
        y = y + shifted(d) * w[CONV_W - 1 - d:CONV_W - d, :]
    return y * _sigmoid(y)


def _delta_prep_kernel(x_ref, halo_ref, ba_ref, cw_ref, al_ref, dt_ref, pk_ref, u_ref, ge_ref,
                       gate_scr, xe_scr, *, seg, tiles_per_step, tiles_per_seq):
    step = pl.program_id(0)
    h = pl.program_id(1)
    levels = int(math.log2(seg))
    ri = lax.broadcasted_iota(jnp.int32, (DN_TILE, DN_TILE), 0)
    ci = lax.broadcasted_iota(jnp.int32, (DN_TILE, DN_TILE), 1)
    same = jnp.where((ri >> levels) == (ci >> levels), 1.0, 0.0)
    incl = jnp.where(ci <= ri, same, 0.0)
    strict = jnp.where(ci < ri, same, 0.0)
    eye = jnp.where(ci == ri, 1.0, 0.0)
    base_levels = min(INV_BASE_LEVELS, levels)
    merge_masks = [jnp.where((ri >> lv) == (ci >> lv), 1.0, 0.0) for lv in range(base_levels, levels + 1)]
    idx_b = jnp.full((DN_TILE, LANES), h, jnp.int32)
    idx_g = idx_b + DN_HEADS
    cw = cw_ref[...]
    nt = (((1,), (1,)), ((), ()))

    def mm(x, y):
        return jnp.dot(x, y, preferred_element_type=F32)

    @pl.when(h == 0)
    def _():
        after = jnp.where(ci > ri, same, 0.0)
        prefix_suffix = jnp.concatenate([incl, after], axis=0).astype(BF16)
        pieces = []
        for t in range(tiles_per_step):
            ba = ba_ref[t * DN_TILE:(t + 1) * DN_TILE, :]
            gate_scr[t, 0] = _sigmoid(ba)
            xg = ba + dt_ref[...]
            softplus = jnp.maximum(xg, 0.0) + jnp.log(1.0 + jnp.exp(-jnp.abs(xg)))
            g = -jnp.exp(al_ref[...]) * softplus
            g_hi = g.astype(BF16)
            g_r = g - g_hi.astype(F32)
            g_mid = g_r.astype(BF16)
            pieces.append((g_hi, g_mid, (g_r - g_mid.astype(F32)).astype(BF16)))
        sums = [mm(prefix_suffix, p[0]) for p in pieces]
        for piece in (1, 2):
            sums = [sm + mm(prefix_suffix, p[piece]) for sm, p in zip(sums, pieces)]
        for t in range(tiles_per_step):
            gate_scr[t, 1] = sums[t][:DN_TILE]
            gate_scr[t, 2] = sums[t][DN_TILE:]

    if seg == DN_TILE:
        tile0 = step * tiles_per_step
        first = ((tile0 % tiles_per_seq) == 0).astype(F32)
        xe_scr[0:SUBLANES, :] = halo_ref[...] * (1.0 - first)
        xe_scr[SUBLANES:, :] = x_ref[...]

    def front(t):
        rows = slice(t * DN_TILE, (t + 1) * DN_TILE)
        x = x_ref[rows, :]
        if seg == DN_TILE:
            def shifted(d):
                return xe_scr[pl.ds(SUBLANES + t * DN_TILE - d, DN_TILE), :]
        else:
            bx = halo_ref[rows, :]
            t_in = lax.broadcasted_iota(jnp.int32, x.shape, 0) & (seg - 1)

            def shifted(d):
                xs = pltpu.roll(x, d, 0)
                bs = pltpu.roll(bx, (d - seg) % DN_TILE, 0)
                return jnp.where(t_in >= d, xs, bs)
        y = _conv_silu(x, cw, shifted)
        qc, kc, v = y[:, 0:DK], y[:, DK:2 * DK], y[:, 2 * DK:3 * DK]
        q = qc * lax.rsqrt(jnp.sum(qc * qc, axis=-1, keepdims=True) + EPS) * (DK ** -0.5)
        k = kc * lax.rsqrt(jnp.sum(kc * kc, axis=-1, keepdims=True) + EPS)

        def head_lane(j, idx):
            return jnp.take_along_axis(gate_scr[t, j], idx, axis=1, mode="promise_in_bounds")

        return dict(rows=rows, q=q, k=k, v=v, beta=head_lane(0, idx_b), cum=head_lane(1, idx_g),
                    rem=head_lane(2, idx_g), kb=k.astype(BF16), qb=q.astype(BF16))

    tiles = [front(t) for t in range(tiles_per_step)]

    for ts in tiles:
        ts["kk"] = lax.dot_general(ts["kb"], ts["kb"], nt, preferred_element_type=F32)
    for ts in tiles:
        ts["qk"] = lax.dot_general(ts["qb"], ts["kb"], nt, preferred_element_type=F32)
    for ts in tiles:
        diff = ts["cum"] - ts["cum"].T
        ts["decay"] = jnp.exp(jnp.where(incl > 0.0, diff, NEG))
        ts["a"] = strict * (ts["beta"] * ts["decay"] * ts["kk"])
        a0 = ts["a"] * merge_masks[0]
        ts["t_inv"] = eye - a0
        ts["pw"] = a0
    for _ in range(base_levels - 1):
        for ts in tiles:
            pwb = ts["pw"].astype(BF16)
            ts["pw"] = mm(pwb, pwb)
        for ts in tiles:
            ts["t_inv"] = ts["t_inv"] + mm(ts["t_inv"].astype(BF16), ts["pw"].astype(BF16))
    for lv in range(base_levels, levels):
        lmask = merge_masks[lv - base_levels + 1] - merge_masks[lv - base_levels]
        for ts in tiles:
            ts["tb"] = ts["t_inv"].astype(BF16)
            ts["tl"] = mm(ts["tb"], (ts["a"] * lmask).astype(BF16))
        for ts in tiles:
            ts["t_inv"] = ts["t_inv"] - mm(ts["tl"].astype(BF16), ts["tb"])
    for ts in tiles:
        ts["gam"] = jnp.exp(ts["cum"])
        ts["tb"] = ts["t_inv"].astype(BF16)
        ts["w"] = mm(ts["tb"], (ts["beta"] * ts["gam"] * ts["k"]).astype(BF16))
    for ts in tiles:
        ts["u"] = mm(ts["tb"], (ts["beta"] * ts["v"]).astype(BF16))
    segs = DN_TILE // seg
    for t, ts in enumerate(tiles):
        rows = ts["rows"]
        pk_ref[rows, PK_W * DK:(PK_W + 1) * DK] = ts["w"].astype(BF16)
        pk_ref[rows, PK_QD * DK:(PK_QD + 1) * DK] = (ts["q"] * ts["gam"]).astype(BF16)
        pk_ref[rows, PK_KD * DK:(PK_KD + 1) * DK] = (ts["k"] * jnp.exp(ts["rem"])).astype(BF16)
        pk_ref[rows, PK_QK * DK:(PK_QK + 1) * DK] = (ts["qk"] * ts["decay"]).astype(BF16)
        u_ref[rows, :] = ts["u"]
        g_end = jnp.exp(ts["cum"] + ts["rem"])
        for j in range(segs):
            ge_ref[t * segs + j] = g_end[j * seg:j * seg + 1, :]


def _delta_prep(proj, halo_src, conv_w_perm, al_row, dt_row, *, seg, t_len):
    n = proj.shape[0]
    tiles_per_step = 16 if n % (16 * DN_TILE) == 0 else (8 if n % (8 * DN_TILE) == 0 else 4)
    tr = tiles_per_step * DN_TILE
    assert n % tr == 0
    steps = n // tr
    qkv_blk = C_QKV // HEAD_COLS
    segs = tr // seg
    if seg == DN_TILE:
        assert t_len % tr == 0
        sub_per_step = tr // SUBLANES
        halo_spec = pl.BlockSpec(
            (SUBLANES, HEAD_COLS), lambda s, h: (jnp.maximum(s * sub_per_step - 1, 0), qkv_blk + h))
    else:
        halo_spec = pl.BlockSpec((tr, HEAD_COLS), lambda s, h: (s, h))
    return pl.pallas_call(
        functools.partial(_delta_prep_kernel, seg=seg, tiles_per_step=tiles_per_step,
                          tiles_per_seq=max(t_len // DN_TILE, 1)),
        grid=(steps, DN_HEADS),
        in_specs=[
            pl.BlockSpec((tr, HEAD_COLS), lambda s, h: (s, qkv_blk + h)),
            halo_spec,
            pl.BlockSpec((tr, LANES), lambda s, h: (s, C_BA // LANES)),
            pl.BlockSpec((CONV_W, HEAD_COLS), lambda s, h: (0, h)),
            pl.BlockSpec((1, LANES), lambda s, h: (0, 0)),
            pl.BlockSpec((1, LANES), lambda s, h: (0, 0)),
        ],
        out_specs=[
            pl.BlockSpec((None, tr, 4 * DK), lambda s, h: (h, s, 0)),
            pl.BlockSpec((None, tr, DV), lambda s, h: (h, s, 0)),
            pl.BlockSpec((None, segs, 1, LANES), lambda s, h: (h, s, 0, 0)),
        ],
        out_shape=[
            jax.ShapeDtypeStruct((DN_HEADS, n, 4 * DK), BF16),
            jax.ShapeDtypeStruct((DN_HEADS, n, DV), F32),
            jax.ShapeDtypeStruct((DN_HEADS, n // seg, 1, LANES), F32),
        ],
        scratch_shapes=[pltpu.VMEM((tiles_per_step, 3, DN_TILE, LANES), F32),
                        pltpu.VMEM((SUBLANES + tr, HEAD_COLS), F32)],
        compiler_params=_params(
            ("parallel", "arbitrary"),
            pipelined=[((tr, HEAD_COLS), F32), ((tr if seg != DN_TILE else SUBLANES, HEAD_COLS), F32),
                       ((tr, LANES), F32), ((CONV_W, HEAD_COLS), F32), ((1, LANES), F32), ((1, LANES), F32),
                       ((tr, 4 * DK), BF16), ((tr, DV), F32), ((segs, 1, LANES), F32)],
            resident=[((tiles_per_step, 3, DN_TILE, LANES), F32), ((SUBLANES + tr, HEAD_COLS), F32)],
            temps=[((DN_TILE, DN_TILE), F32)] * (24 * tiles_per_step)),
        name="delta_prep",
    )(proj, halo_src, proj, conv_w_perm, al_row, dt_row)


def _out_gate(o, z, dnw):
    return _rms(o, dnw) * (z * _sigmoid(z))


def _delta_scan_prompt_kernel(pk_ref, u_ref, ge_ref, z0_ref, z1_ref, dnw_ref, o_ref, sout_ref, s_scr, *, hg):
    c = pl.program_id(2)

    @pl.when(c == 0)
    def _():
        s_scr[...] = jnp.zeros(s_scr.shape, F32)

    contract0 = (((0,), (0,)), ((), ()))
    heads = range(hg)
    res = []
    for j in heads:
        wq = jnp.concatenate([pk_ref[j, :, PK_W * DK:(PK_W + 1) * DK],
                              pk_ref[j, :, PK_QD * DK:(PK_QD + 1) * DK]], axis=0)
        res.append(jnp.dot(wq, s_scr[j].astype(BF16), preferred_element_type=F32))
    ub = [(u_ref[j] - res[j][:DN_TILE]).astype(BF16) for j in heads]
    o = [res[j][DN_TILE:] + jnp.dot(pk_ref[j, :, PK_QK * DK:(PK_QK + 1) * DK], ub[j],
                                    preferred_element_type=F32) for j in heads]
    upd = [lax.dot_general(pk_ref[j, :, PK_KD * DK:(PK_KD + 1) * DK], ub[j], contract0,
                           preferred_element_type=F32) for j in heads]
    half = hg // 2
    for j in heads:
        s_scr[j] = ge_ref[j, 0] * s_scr[j] + upd[j]
        z_ref = z0_ref if j < half else z1_ref
        zc = (j % half) * DV
        o_ref[:, j * DV:(j + 1) * DV] = _out_gate(o[j], z_ref[:, zc:zc + DV], dnw_ref[...]).astype(o_ref.dtype)

    @pl.when(c == pl.num_programs(2) - 1)
    def _():
        sout_ref[...] = s_scr[...]


def _delta_scan_prompt(pk, u, ge, proj, dnw_row, batch, t_len):
    hg = DN_HEADS
    zw = hg * DV // 2
    assert C_Z % zw == 0
    nc = t_len // DN_TILE
    zblk = C_Z // zw
    return pl.pallas_call(
        functools.partial(_delta_scan_prompt_kernel, hg=hg),
        grid=(batch, DN_HEADS // hg, nc),
        in_specs=[
            pl.BlockSpec((hg, DN_TILE, 4 * DK), lambda b, g, c: (g, b * nc + c, 0)),
            pl.BlockSpec((hg, DN_TILE, DV), lambda b, g, c: (g, b * nc + c, 0)),
            pl.BlockSpec((hg, 1, 1, LANES), lambda b, g, c: (g, b * nc + c, 0, 0)),
            pl.BlockSpec((DN_TILE, zw), lambda b, g, c: (b * nc + c, zblk)),
            pl.BlockSpec((DN_TILE, zw), lambda b, g, c: (b * nc + c, zblk + 1)),
            pl.BlockSpec((1, DV), lambda b, g, c: (0, 0)),
        ],
        out_specs=[
            pl.BlockSpec((DN_TILE, hg * DV), lambda b, g, c: (b * nc + c, g)),
            pl.BlockSpec((None, hg, DK, DV), lambda b, g, c: (b, g, 0, 0)),
        ],
        out_shape=[
            jax.ShapeDtypeStruct((batch * t_len, DN_V), BF16),
            jax.ShapeDtypeStruct((batch, DN_HEADS, DK, DV), F32),
        ],
        scratch_shapes=[pltpu.VMEM((hg, DK, DV), F32)],
        compiler_params=_params(
            ("parallel", "parallel", "arbitrary"),
            pipelined=[((hg, DN_TILE, 4 * DK), BF16), ((hg, DN_TILE, DV), F32), ((hg, 1, 1, LANES), F32),
                       ((DN_TILE, zw), F32), ((DN_TILE, zw), F32), ((1, DV), F32),
                       ((DN_TILE, hg * DV), BF16), ((hg, DK, DV), F32)],
            resident=[((hg, DK, DV), F32)],
            temps=[((hg, 2 * DN_TILE, DV), F32)] + [((hg, DN_TILE, DV), F32)] * 3),
        name="delta_scan_prompt",
    )(pk, u, ge, proj, proj, dnw_row)


def _delta_scan_sample_kernel(pk_ref, u_ref, ge_ref, z_ref, dnw_ref, s0_ref, o_ref, sout_ref,
                              ub_scr, oq_scr, pk32_scr, *, hg, t_dec, seq_unroll):
    n_seq = DN_TILE // t_dec
    contract0 = (((0,), (0,)), ((), ()))
    for j in range(hg):
        pk32_scr[j] = pk_ref[j, :, 0:3 * DK].astype(F32)

    def group(gi, carry):
        pairs = []
        for a in range(seq_unroll):
            i = gi * seq_unroll + a
            r0 = pl.multiple_of(i * t_dec, t_dec)
            pairs += [(i, r0, j) for j in range(hg)]
        res = []
        for i, r0, j in pairs:
            wq = jnp.concatenate([pk32_scr[j, pl.ds(r0, t_dec), PK_W * DK:(PK_W + 1) * DK],
                                  pk32_scr[j, pl.ds(r0, t_dec), PK_QD * DK:(PK_QD + 1) * DK]], axis=0)
            res.append(jnp.dot(wq.astype(BF16), s0_ref[i, j].astype(BF16), preferred_element_type=F32))
        upd = []
        for (i, r0, j), r in zip(pairs, res):
            uu = u_ref[j, pl.ds(r0, t_dec), :] - r[:t_dec]
            ub_scr[j, pl.ds(r0, t_dec), :] = uu
            oq_scr[j, pl.ds(r0, t_dec), :] = r[t_dec:]
            kd = pk32_scr[j, pl.ds(r0, t_dec), PK_KD * DK:(PK_KD + 1) * DK]
            upd.append(lax.dot_general(kd.astype(BF16), uu.astype(BF16), contract0,
                                       preferred_element_type=F32))
        for (i, r0, j), up in zip(pairs, upd):
            sout_ref[i, j] = ge_ref[j, i] * s0_ref[i, j] + up
        return carry

    lax.fori_loop(0, n_seq // seq_unroll, group, 0)
    o = [oq_scr[j] + jnp.dot(pk_ref[j, :, PK_QK * DK:(PK_QK + 1) * DK], ub_scr[j].astype(BF16),
                             preferred_element_type=F32) for j in range(hg)]
    for j in range(hg):
        cols = slice(j * DV, (j + 1) * DV)
        o_ref[:, cols] = _out_gate(o[j], z_ref[:, cols], dnw_ref[...]).astype(o_ref.dtype)


def _delta_scan_sample(pk, u, ge, proj, dnw_row, s0, n_seq, t_dec):
    hg = 4
    seq_per_tile = DN_TILE // t_dec
    tiles = n_seq // seq_per_tile
    zblk = C_Z // (hg * DV)
    return pl.pallas_call(
        functools.partial(_delta_scan_sample_kernel, hg=hg, t_dec=t_dec, seq_unroll=4),
        grid=(tiles, DN_HEADS // hg),
        in_specs=[
            pl.BlockSpec((hg, DN_TILE, 4 * DK), lambda r, g: (g, r, 0)),
            pl.BlockSpec((hg, DN_TILE, DV), lambda r, g: (g, r, 0)),
            pl.BlockSpec((hg, seq_per_tile, 1, LANES), lambda r, g: (g, r, 0, 0)),
            pl.BlockSpec((DN_TILE, hg * DV), lambda r, g: (r, zblk + g)),
            pl.BlockSpec((1, DV), lambda r, g: (0, 0)),
            pl.BlockSpec((seq_per_tile, hg, DK, DV), lambda r, g: (r, g, 0, 0)),
        ],
        out_specs=[
            pl.BlockSpec((DN_TILE, hg * DV), lambda r, g: (r, g)),
            pl.BlockSpec((seq_per_tile, hg, DK, DV), lambda r, g: (r, g, 0, 0)),
        ],
        out_shape=[
            jax.ShapeDtypeStruct((n_seq * t_dec, DN_V), BF16),
            jax.ShapeDtypeStruct((n_seq, DN_HEADS, DK, DV), F32),
        ],
        scratch_shapes=[pltpu.VMEM((hg, DN_TILE, DV), F32), pltpu.VMEM((hg, DN_TILE, DV), F32),
                        pltpu.VMEM((hg, DN_TILE, 3 * DK), F32)],
        compiler_params=_params(
            ("parallel", "parallel"),
            pipelined=[((hg, DN_TILE, 4 * DK), BF16), ((hg, DN_TILE, DV), F32), ((hg, seq_per_tile, 1, LANES), F32),
                       ((DN_TILE, hg * DV), F32), ((1, DV), F32), ((seq_per_tile, hg, DK, DV), F32),
                       ((DN_TILE, hg * DV), BF16), ((seq_per_tile, hg, DK, DV), F32)],
            resident=[((hg, DN_TILE, DV), F32)] * 2 + [((hg, DN_TILE, 3 * DK), F32)],
            temps=[((DK, DV), F32)] * 32),
        name="delta_scan_sample",
    )(pk, u, ge, proj, dnw_row, s0)


def _mix_out_kernel(oa_ref, od_ref, wa_ref, wd_ref, wo_ref, ga0_ref, ga1_ref, gd0_ref, gd1_ref, x_ref, g_ref,
                    o_ref):
    ya = jnp.dot(oa_ref[...], wa_ref[...], preferred_element_type=F32)
    yd = jnp.dot(od_ref[...], wd_ref[...], preferred_element_type=F32)
    half = D_MODEL // 2
    mixed = jnp.concatenate(
        [_sigmoid(ga0_ref[...]) * ya[:, :half] + _sigmoid(gd0_ref[...]) * yd[:, :half],
         _sigmoid(ga1_ref[...]) * ya[:, half:] + _sigmoid(gd1_ref[...]) * yd[:, half:]], axis=1)
    t = jnp.dot(mixed.astype(BF16), wo_ref[...], preferred_element_type=F32)
    o_ref[...] = x_ref[...] + _rms(t, g_ref[...])


def _mix_out(oa, od, wa, wd, wo, proj, x2d, g_row):
    n = oa.shape[0]
    tm = 256
    half = D_MODEL // 2

    def resident(shape):
        return pl.BlockSpec(shape, lambda i: (0, 0), pipeline_mode=pl.Buffered(1))

    def gate(col):
        return pl.BlockSpec((tm, half), lambda i, c=col // half: (i, c))

    return pl.pallas_call(
        _mix_out_kernel,
        grid=(n // tm,),
        in_specs=[
            pl.BlockSpec((tm, Q_A), lambda i: (i, 0)),
            pl.BlockSpec((tm, DN_V), lambda i: (i, 0)),
            resident((Q_A, D_MODEL)),
            resident((DN_V, D_MODEL)),
            resident((D_MODEL, D_MODEL)),
            gate(C_GA), gate(C_GA + half), gate(C_GD), gate(C_GD + half),
            pl.BlockSpec((tm, D_MODEL), lambda i: (i, 0)),
            pl.BlockSpec((1, D_MODEL), lambda i: (0, 0)),
        ],
        out_specs=pl.BlockSpec((tm, D_MODEL), lambda i: (i, 0)),
        out_shape=jax.ShapeDtypeStruct((n, D_MODEL), F32),
        compiler_params=_params(
            ("parallel",),
            pipelined=[((tm, Q_A), BF16), ((tm, DN_V), BF16)] + [((tm, half), F32)] * 4
            + [((tm, D_MODEL), F32), ((1, D_MODEL), F32), ((tm, D_MODEL), F32)],
            resident=[((D_MODEL, D_MODEL), BF16)] * 3,
            temps=[((tm, D_MODEL), F32)] * 3 + [((tm, D_MODEL), BF16)]),
        name="mix_out",
    )(oa, od, wa, wd, wo, proj, proj, proj, proj, x2d, g_row)


def _mlp_kernel(x_ref, gpre_ref, wu_ref, wd_ref, gpost_ref, o_ref, h_ref, acc_ref):
    f = pl.program_id(1)

    @pl.when(f == 0)
    def _():
        h_ref[...] = _rms(x_ref[...], gpre_ref[...]).astype(BF16)
        acc_ref[...] = jnp.zeros(acc_ref.shape, F32)

    up = jnp.dot(h_ref[...], wu_ref[...], preferred_element_type=F32)
    act = jnp.square(jnp.maximum(up, 0.0)).astype(BF16)
    acc_ref[...] += jnp.dot(act, wd_ref[...], preferred_element_type=F32)

    @pl.when(f == pl.num_programs(1) - 1)
    def _():
        o_ref[...] = x_ref[...] + _rms(acc_ref[...], gpost_ref[...])


def _mlp(x2d, gpre, w_up, w_down, gpost):
    n = x2d.shape[0]
    tm, tf = 512, 1024
    return pl.pallas_call(
        _mlp_kernel,
        grid=(n // tm, D_FF // tf),
        in_specs=[
            pl.BlockSpec((tm, D_MODEL), lambda i, f: (i, 0)),
            pl.BlockSpec((1, D_MODEL), lambda i, f: (0, 0)),
            pl.BlockSpec((D_MODEL, tf), lambda i, f: (0, f)),
            pl.BlockSpec((tf, D_MODEL), lambda i, f: (f, 0)),
            pl.BlockSpec((1, D_MODEL), lambda i, f: (0, 0)),
        ],
        out_specs=pl.BlockSpec((tm, D_MODEL), lambda i, f: (i, 0)),
        out_shape=jax.ShapeDtypeStruct((n, D_MODEL), F32),
        scratch_shapes=[pltpu.VMEM((tm, D_MODEL), BF16), pltpu.VMEM((tm, D_MODEL), F32)],
        compiler_params=_params(
            ("parallel", "arbitrary"),
            pipelined=[((tm, D_MODEL), F32), ((1, D_MODEL), F32), ((D_MODEL, tf), BF16), ((tf, D_MODEL), BF16),
                       ((1, D_MODEL), F32), ((tm, D_MODEL), F32)],
            resident=[((tm, D_MODEL), BF16), ((tm, D_MODEL), F32)],
            temps=[((tm, tf), F32), ((tm, tf), BF16), ((tm, D_MODEL), F32)]),
        name="mlp",
    )(x2d, gpre, w_up, w_down, gpost)


def _head_major(x):
    lead = x.shape[:-1]
    return jnp.swapaxes(x.reshape(*lead, 3, DN_HEADS, DK), -3, -2).reshape(*lead, CONV_DIM)


def _part_major(x):
    lead = x.shape[:-1]
    return jnp.swapaxes(x.reshape(*lead, DN_HEADS, 3, DK), -3, -2).reshape(*lead, CONV_DIM)


W_QKV = Q_A + 2 * KV_A
W_Z = W_QKV + CONV_DIM
W_BA = W_Z + DN_V
W_GA = W_BA + 2 * DN_HEADS
W_DIM = W_GA + 2 * D_MODEL


def _pack_w_in_kernel(w_ref, o_ref):
    def cp(dst, src, n):
        o_ref[dst:dst + n, :] = w_ref[src:src + n, :].astype(BF16)

    cp(C_Q, 0, W_QKV)
    for h in range(DN_HEADS):
        for p in range(3):
            cp(C_QKV + h * HEAD_COLS + p * DK, W_QKV + p * DN_QK + h * DK, DK)
    cp(C_Z, W_Z, DN_V)
    cp(C_GA, W_GA, 2 * D_MODEL)
    o_ref[C_BA:P_DIM, :] = jnp.zeros((P_DIM - C_BA, o_ref.shape[1]), BF16)
    cp(C_BA, W_BA, 2 * DN_HEADS)


def _pack_w_in(w_in_t):
    tc = 256
    return pl.pallas_call(
        _pack_w_in_kernel,
        grid=(D_MODEL // tc,),
        in_specs=[pl.BlockSpec((W_DIM, tc), lambda i: (0, i))],
        out_specs=pl.BlockSpec((P_DIM, tc), lambda i: (0, i)),
        out_shape=jax.ShapeDtypeStruct((P_DIM, D_MODEL), BF16),
        compiler_params=_params(("parallel",), pipelined=[((W_DIM, tc), F32), ((P_DIM, tc), BF16)]),
        name="pack_w_in",
    )(w_in_t)


def _lane_row(vals, offset):
    return jnp.zeros((1, LANES), F32).at[0, offset:offset + vals.shape[0]].set(vals.astype(F32))


def _layer(x2d, attend, delta, w):
    proj = _in_proj(x2d, w["g_mix_pre"], w["w_in"])
    o_a = attend(proj)
    o_d, s_final = delta(proj)
    x1 = _mix_out(o_a, o_d, w["w_ba"], w["w_bd"], w["w_out"], proj, x2d, w["g_mix_post"])
    y = _mlp(x1, w["g_mlp_pre"], w["w_up"], w["w_down"], w["g_mlp_post"])
    return y, proj, s_final


def kernel(x_prompt, x_sample, cache_k, cache_v, state_delta, state_conv, page_table, w_in, conv_w, a_log,
           dt_bias, dn_norm_w, w_branch_attn, w_branch_delta, w_out, g_mix_pre, g_mix_post, g_mlp_pre,
           g_mlp_post, w_up, w_down):
    depth = w_in.shape[0]
    assert depth == 1
    b_p, t_p, _ = x_prompt.shape
    b_s, t_s, _ = x_sample.shape
    l = 0
    w = {
        "w_in": _pack_w_in(jnp.transpose(w_in[l])),
        "g_mix_pre": g_mix_pre[l][None, :],
        "g_mix_post": g_mix_post[l][None, :],
        "g_mlp_pre": g_mlp_pre[l][None, :],
        "g_mlp_post": g_mlp_post[l][None, :],
        "w_ba": w_branch_attn[l].astype(BF16),
        "w_bd": w_branch_delta[l].astype(BF16),
        "w_out": w_out[l].astype(BF16),
        "w_up": w_up[l].astype(BF16),
        "w_down": w_down[l].astype(BF16),
    }
    cw = _head_major(conv_w[l])
    al_row = _lane_row(a_log[l], DN_HEADS)
    dt_row = _lane_row(dt_bias[l], DN_HEADS)
    dnw_row = dn_norm_w[l][None, :]

    kv_prompt = []

    def attend_p(proj):
        kaug, vb, kmean, k5, v5 = _kprep(proj, b_p, t_p)
        kv_prompt.extend([k5[None], v5[None]])
        return _moba_prompt(proj, kaug, vb, kmean, b_p, t_p)

    def delta_p(proj):
        pk, u, ge = _delta_prep(proj, proj, cw, al_row, dt_row, seg=DN_TILE, t_len=t_p)
        return _delta_scan_prompt(pk, u, ge, proj, dnw_row, b_p, t_p)

    y_p, proj_p, d_p = _layer(x_prompt.reshape(b_p * t_p, D_MODEL), attend_p, delta_p, w)

    n_pool = cache_k.shape[1]
    ck2d = cache_k[l].reshape(n_pool * PAGE * N_KV, HD)
    cv2d = cache_v[l].reshape(n_pool * PAGE * N_KV, HD)
    conv_rows = _head_major(state_conv[l])
    halo_s = jnp.pad(conv_rows, ((0, 0), (t_s - (CONV_W - 1), 0), (0, 0))).reshape(b_s * t_s, CONV_DIM)

    def attend_s(proj):
        return _moba_sample(proj, ck2d, cv2d, page_table, b_s, t_s)

    def delta_s(proj):
        pk, u, ge = _delta_prep(proj, halo_s, cw, al_row, dt_row, seg=t_s, t_len=t_s)
        return _delta_scan_sample(pk, u, ge, proj, dnw_row, state_delta[l], b_s, t_s)

    y_s, proj_s, d_s = _layer(x_sample.reshape(b_s * t_s, D_MODEL), attend_s, delta_s, w)

    def kv_out(proj, b, t):
        k = proj[:, C_K:C_K + KV_A].reshape(1, b, t, N_KV, HD)
        v = proj[:, C_V:C_V + KV_A].reshape(1, b, t, N_KV, HD)
        return k, v

    def conv_out(proj, b, t):
        raw = proj.reshape(b, t, P_DIM)[:, t - (CONV_W - 1):, C_QKV:C_QKV + CONV_DIM]
        return _part_major(raw)[None]

    k_p, v_p = kv_prompt
    k_s, v_s = kv_out(proj_s, b_s, t_s)
    return (y_p.reshape(b_p, t_p, D_MODEL), y_s.reshape(b_s, t_s, D_MODEL), k_p, v_p, d_p[None],
            conv_out(proj_p, b_p, t_p), k_s, v_s, d_s[None], conv_out(proj_s, b_s, t_s))
```

```python
import functools
import math

import numpy as np
import jax
import jax.numpy as jnp
from jax import lax
from jax.experimental import pallas as pl
from jax.experimental.pallas import tpu as pltpu

F32 = jnp.float32
BF16 = jnp.bfloat16

D_MODEL = 2048
N_HEADS = 16
N_KV = 4
HD = 128
GROUP = N_HEADS // N_KV
MOBA_BLOCK = 256
MOBA_TOPK = 3
PAGE = 128
DN_HEADS = 16
DK = 128
DV = 128
CONV_W = 4
Q_A = N_HEADS * HD
KV_A = N_KV * HD
DN_QK = DN_HEADS * DK
DN_V = DN_HEADS * DV
CONV_DIM = 2 * DN_QK + DN_V
D_FF = 4 * D_MODEL
EPS = 1e-6

LANES = 128
SUBLANES = 8

C_Q = 0
C_K = C_Q + Q_A
C_V = C_K + KV_A
C_QKV = C_V + KV_A
C_Z = C_QKV + CONV_DIM
C_GA = C_Z + DN_V
C_GD = C_GA + D_MODEL
C_BA = C_GD + D_MODEL
P_DIM = C_BA + LANES
HEAD_COLS = 3 * DK

NEG = -1e30
SCALE = HD ** -0.5
EXP2_C = SCALE * math.log2(math.e)


V7X_VMEM_BUDGET = 56 * 2**20


def _nbytes(shape, dtype):
    return math.prod(shape) * jnp.dtype(dtype).itemsize


def _params(sem, pipelined, resident=(), temps=()):
    need = 2 * sum(_nbytes(*b) for b in pipelined) + sum(_nbytes(*b) for b in (*resident, *temps))
    limit = -(-need // 2**20) * 2**20
    assert limit <= V7X_VMEM_BUDGET, (limit, V7X_VMEM_BUDGET)
    return pltpu.CompilerParams(dimension_semantics=sem, vmem_limit_bytes=limit)


def _rms(x, g):
    return x * lax.rsqrt(jnp.mean(x * x, axis=-1, keepdims=True) + EPS) * g


def _sigmoid(x):
    return 1.0 / (1.0 + jnp.exp(-x))


def _in_proj_kernel(x_ref, g_ref, w_ref, o_ref, h_ref):
    @pl.when(pl.program_id(1) == 0)
    def _():
        h_ref[...] = _rms(x_ref[...], g_ref[...]).astype(BF16)

    o_ref[...] = lax.dot_general(h_ref[...], w_ref[...], (((1,), (1,)), ((), ())), preferred_element_type=F32)


def _in_proj(x2d, g_row, wp):
    n = x2d.shape[0]
    tm = 1024 if n % 1024 == 0 else 512
    tn = 1408
    assert n % tm == 0 and P_DIM % tn == 0
    return pl.pallas_call(
        _in_proj_kernel,
        grid=(n // tm, P_DIM // tn),
        in_specs=[
            pl.BlockSpec((tm, D_MODEL), lambda i, j: (i, 0)),
            pl.BlockSpec((1, D_MODEL), lambda i, j: (0, 0)),
            pl.BlockSpec((tn, D_MODEL), lambda i, j: (j, 0)),
        ],
        out_specs=pl.BlockSpec((tm, tn), lambda i, j: (i, j)),
        out_shape=jax.ShapeDtypeStruct((n, P_DIM), F32),
        scratch_shapes=[pltpu.VMEM((tm, D_MODEL), BF16)],
        compiler_params=_params(
            ("parallel", "arbitrary"),
            pipelined=[((tm, D_MODEL), F32), ((1, D_MODEL), F32), ((tn, D_MODEL), BF16), ((tm, tn), F32)],
            resident=[((tm, D_MODEL), BF16)], temps=[((tm, tn), F32)]),
        name="in_proj",
    )(x2d, g_row, wp)


N_SLOPE_FEATS = 4
F_SEL = 0
F_HI = 16
F_LO = F_HI + N_SLOPE_FEATS
MAX_BLOCKS = 16


def _kprep_kernel(k_ref, v_ref, kaug_ref, vb_ref, kmean_ref, k5_ref, v5_ref, *, blocks_per_step):
    step = pl.program_id(1)
    lane = lax.broadcasted_iota(jnp.int32, (MOBA_BLOCK, LANES), 1)
    row = lax.broadcasted_iota(jnp.int32, (MOBA_BLOCK, LANES), 0).astype(F32)
    is_hi = jnp.where(lane >= F_HI, jnp.where(lane < F_LO, 1.0, 0.0), 0.0)
    is_lo = jnp.where(lane >= F_LO, jnp.where(lane < F_LO + N_SLOPE_FEATS, 1.0, 0.0), 0.0)
    ones_col = jnp.where(lane == 0, 1.0, 0.0).astype(BF16)
    for j in range(blocks_per_step):
        n = step * blocks_per_step + j
        rows = slice(j * MOBA_BLOCK, (j + 1) * MOBA_BLOCK)
        k = k_ref[rows, :]
        v = v_ref[rows, :]
        mean = jnp.mean(k, axis=0, keepdims=True)
        feat = (jnp.where(lane == n, 1.0, 0.0)
                + is_hi * (n * MOBA_BLOCK).astype(F32)
                + is_lo * row).astype(BF16)
        for c in range(N_KV):
            cols = slice(c * HD, (c + 1) * HD)
            kaug_ref[c, rows, 0:HD] = k[:, cols].astype(BF16)
            kaug_ref[c, rows, HD:2 * HD] = feat
            vb_ref[c, rows, 0:HD] = v[:, cols].astype(BF16)
            vb_ref[c, rows, HD:2 * HD] = ones_col
            kmean_ref[c, j:j + 1, :] = mean[:, cols]
            k5_ref[rows, c, :] = k[:, cols]
            v5_ref[rows, c, :] = v[:, cols]


def _kprep(proj, batch, t_len):
    nblk = t_len // MOBA_BLOCK
    bps = min(nblk, 8)
    assert nblk % bps == 0 and nblk <= MAX_BLOCKS
    steps = nblk // bps
    rows = bps * MOBA_BLOCK
    return pl.pallas_call(
        functools.partial(_kprep_kernel, blocks_per_step=bps),
        grid=(batch, steps),
        in_specs=[
            pl.BlockSpec((rows, KV_A), lambda b, s: (b * steps + s, C_K // KV_A)),
            pl.BlockSpec((rows, KV_A), lambda b, s: (b * steps + s, C_V // KV_A)),
        ],
        out_specs=[
            pl.BlockSpec((None, N_KV, rows, 2 * HD), lambda b, s: (b, 0, s, 0)),
            pl.BlockSpec((None, N_KV, rows, 2 * HD), lambda b, s: (b, 0, s, 0)),
            pl.BlockSpec((None, N_KV, bps, HD), lambda b, s: (b, 0, s, 0)),
            pl.BlockSpec((None, rows, N_KV, HD), lambda b, s: (b, s, 0, 0)),
            pl.BlockSpec((None, rows, N_KV, HD), lambda b, s: (b, s, 0, 0)),
        ],
        out_shape=[
            jax.ShapeDtypeStruct((batch, N_KV, t_len, 2 * HD), BF16),
            jax.ShapeDtypeStruct((batch, N_KV, t_len, 2 * HD), BF16),
            jax.ShapeDtypeStruct((batch, N_KV, nblk, HD), F32),
            jax.ShapeDtypeStruct((batch, t_len, N_KV, HD), F32),
            jax.ShapeDtypeStruct((batch, t_len, N_KV, HD), F32),
        ],
        compiler_params=_params(
            ("parallel", "arbitrary"),
            pipelined=[((rows, KV_A), F32)] * 2 + [((N_KV, rows, 2 * HD), BF16)] * 2
            + [((N_KV, bps, HD), F32)] + [((rows, N_KV, HD), F32)] * 2),
        name="kprep",
    )(proj, proj)


def _slope_pieces():
    slopes = np.asarray(2.0 ** (-8.0 * np.arange(1, N_HEADS + 1) / N_HEADS), np.float32).astype(np.float64)
    x = slopes / SCALE
    pieces = []
    for _ in range(N_SLOPE_FEATS):
        p = x.astype(np.float32).astype(jnp.bfloat16).astype(np.float64)
        pieces.append(p)
        x = x - p
    return np.stack(pieces, axis=1)


def _slope_feature_table():
    pieces = _slope_pieces()
    tab = np.zeros((N_KV, 2 * N_SLOPE_FEATS, GROUP * MOBA_BLOCK), np.float32)
    for c in range(N_KV):
        for g in range(GROUP):
            cols = slice(g * MOBA_BLOCK, (g + 1) * MOBA_BLOCK)
            for f in range(N_SLOPE_FEATS):
                tab[c, f, cols] = pieces[c * GROUP + g, f]
                tab[c, N_SLOPE_FEATS + f, cols] = pieces[c * GROUP + g, f]
    return jnp.asarray(tab)


def _select_bias(gate_t, own):
    nblk = gate_t.shape[0]
    blk = lax.broadcasted_iota(jnp.int32, gate_t.shape, 0)
    past = blk < own
    gm = jnp.where(past, gate_t, -jnp.inf)
    rank = jnp.zeros(gate_t.shape, F32)
    for m in range(nblk):
        row = gm[m:m + 1, :]
        tie = jnp.where(blk > m, 1.0, 0.0)
        rank = rank + jnp.where(row > gm, 1.0, jnp.where(row == gm, tie, 0.0))
    keep_past = jnp.where(past, jnp.where(rank < MOBA_TOPK - 0.5, 0.0, NEG), NEG)
    return jnp.where(blk == own, 0.0, keep_past)


Q_TILES = 2


def _moba_prompt_kernel(q_ref, kmean_ref, sf_ref, kaug_ref, vb_ref, o_ref,
                        feat_t, qaug, m_s, acc_s, s_a, s_b, s_c, s_d, *, nblk):
    i0 = pl.program_id(2) * Q_TILES
    tile = GROUP * MOBA_BLOCK
    rows = Q_TILES * tile
    q = q_ref[...]
    qs = jnp.concatenate([q[a * MOBA_BLOCK:(a + 1) * MOBA_BLOCK, g * HD:(g + 1) * HD]
                          for a in range(Q_TILES) for g in range(GROUP)], axis=0).astype(BF16)
    qaug[:, 0:HD] = qs
    gate_t = lax.dot_general(kmean_ref[...].astype(BF16), qs, (((1,), (1,)), ((), ())),
                             preferred_element_type=F32)
    own = i0 + lax.broadcasted_iota(jnp.int32, (1, rows), 1) // tile
    feat_t[...] = jnp.zeros(feat_t.shape, F32)
    feat_t[F_SEL:F_SEL + nblk, :] = _select_bias(gate_t, own)
    for a in range(Q_TILES):
        feat_t[F_HI:F_HI + 2 * N_SLOPE_FEATS, a * tile:(a + 1) * tile] = sf_ref[...]
    qaug[:, HD:2 * HD] = feat_t[...].T.astype(BF16)

    def key_rows(n):
        return pl.ds(pl.multiple_of(n * MOBA_BLOCK, MOBA_BLOCK), MOBA_BLOCK)

    def raw_scores(n, r0=0):
        return lax.dot_general(qaug[r0:, :], kaug_ref[key_rows(n), :], (((1,), (1,)), ((), ())),
                               preferred_element_type=F32)

    def pair_scores(pr, bufs):
        n0 = jnp.maximum(jnp.minimum(2 * pr, i0 - 2), 0)
        bufs[0][...] = raw_scores(n0)
        bufs[1][...] = raw_scores(n0 + 1)

    def probs(s, m):
        return jnp.concatenate([jnp.exp2((s[:, :LANES] - m) * EXP2_C),
                                jnp.exp2((s[:, LANES:] - m) * EXP2_C)], axis=1).astype(BF16)

    def row_max(s):
        cur = jnp.max(jnp.maximum(s[:, :LANES], s[:, LANES:]), axis=-1, keepdims=True)
        return jnp.broadcast_to(cur, (s.shape[0], LANES))

    def update(s, n, r0=0):
        m_old = m_s[r0:, :]
        m_new = jnp.maximum(m_old, row_max(s))
        alpha = jnp.exp2((m_old - m_new) * EXP2_C)
        pv = jnp.dot(probs(s, m_new), vb_ref[key_rows(n), :], preferred_element_type=F32)
        acc_s[r0:, :] = jnp.concatenate([alpha, alpha], axis=1) * acc_s[r0:, :] + pv
        m_s[r0:, :] = m_new

    def causal(s):
        qi = lax.broadcasted_iota(jnp.int32, s.shape, 0) & (MOBA_BLOCK - 1)
        kj = lax.broadcasted_iota(jnp.int32, s.shape, 1)
        return kj <= qi

    m_s[...] = jnp.full(m_s.shape, NEG, F32)
    acc_s[...] = jnp.zeros(acc_s.shape, F32)
    for a in reversed(range(Q_TILES)):
        s = raw_scores(i0 + a, a * tile)
        if a == 0:
            pair_scores(0, (s_a, s_b))
        own_rows = lax.broadcasted_iota(jnp.int32, s.shape, 0) < tile
        s = jnp.where(causal(s), s, jnp.where(own_rows, NEG, s))
        update(s, i0 + a, a * tile)

    n_pairs = i0 // 2

    def pair_update(pr, bufs):
        s0 = bufs[0][...]
        s1 = bufs[1][...]
        m_old = m_s[...]
        m_new = jnp.maximum(jnp.maximum(m_old, row_max(s0)), row_max(s1))
        alpha = jnp.exp2((m_old - m_new) * EXP2_C)
        p = jnp.concatenate([probs(s0, m_new), probs(s1, m_new)], axis=1)
        v2 = vb_ref[pl.ds(pl.multiple_of(pr * 2 * MOBA_BLOCK, 2 * MOBA_BLOCK), 2 * MOBA_BLOCK), :]
        pv = jnp.dot(p, v2, preferred_element_type=F32)
        acc_s[...] = jnp.concatenate([alpha, alpha], axis=1) * acc_s[...] + pv
        m_s[...] = m_new

    def body(j, carry):
        pr = 2 * j
        pair_scores(pr + 1, (s_c, s_d))
        pair_update(pr, (s_a, s_b))
        pair_scores(pr + 2, (s_a, s_b))
        pair_update(pr + 1, (s_c, s_d))
        return carry

    lax.fori_loop(0, n_pairs // 2, body, 0)

    @pl.when(n_pairs % 2 == 1)
    def _():
        pair_update(n_pairs - 1, (s_a, s_b))

    acc = acc_s[...]
    out = acc[:, 0:HD] / acc[:, HD:HD + 1]
    for a in range(Q_TILES):
        for g in range(GROUP):
            r = a * tile + g * MOBA_BLOCK
            o_ref[a * MOBA_BLOCK:(a + 1) * MOBA_BLOCK, g * HD:(g + 1) * HD] = (
                out[r:r + MOBA_BLOCK, :].astype(o_ref.dtype))


def _moba_prompt(proj, kaug, vb, kmean, batch, t_len):
    nblk = t_len // MOBA_BLOCK
    assert nblk % Q_TILES == 0 and Q_TILES % 2 == 0
    nq = nblk // Q_TILES
    tile = GROUP * MOBA_BLOCK
    rows = Q_TILES * tile
    qrows = Q_TILES * MOBA_BLOCK
    qcols = GROUP * HD
    return pl.pallas_call(
        functools.partial(_moba_prompt_kernel, nblk=nblk),
        grid=(batch, N_KV, nq),
        in_specs=[
            pl.BlockSpec((qrows, qcols), lambda b, c, i: (b * nq + i, C_Q // qcols + c)),
            pl.BlockSpec((None, None, nblk, HD), lambda b, c, i: (b, c, 0, 0)),
            pl.BlockSpec((None, 2 * N_SLOPE_FEATS, tile), lambda b, c, i: (c, 0, 0)),
            pl.BlockSpec((None, None, t_len, 2 * HD), lambda b, c, i: (b, c, 0, 0)),
            pl.BlockSpec((None, None, t_len, 2 * HD), lambda b, c, i: (b, c, 0, 0)),
        ],
        out_specs=pl.BlockSpec((qrows, qcols), lambda b, c, i: (b * nq + i, c)),
        out_shape=jax.ShapeDtypeStruct((batch * t_len, Q_A), BF16),
        scratch_shapes=[
            pltpu.VMEM((LANES, rows), F32),
            pltpu.VMEM((rows, 2 * HD), BF16),
            pltpu.VMEM((rows, LANES), F32),
            pltpu.VMEM((rows, 2 * HD), F32),
            pltpu.VMEM((rows, MOBA_BLOCK), F32),
            pltpu.VMEM((rows, MOBA_BLOCK), F32),
            pltpu.VMEM((rows, MOBA_BLOCK), F32),
            pltpu.VMEM((rows, MOBA_BLOCK), F32),
        ],
        compiler_params=_params(
            ("parallel", "parallel", "arbitrary"),
            pipelined=[((qrows, qcols), F32), ((nblk, HD), F32), ((2 * N_SLOPE_FEATS, tile), F32),
                       ((t_len, 2 * HD), BF16), ((t_len, 2 * HD), BF16), ((qrows, qcols), BF16)],
            resident=[((LANES, rows), F32), ((rows, 2 * HD), BF16), ((rows, LANES), F32), ((rows, 2 * HD), F32)]
            + [((rows, MOBA_BLOCK), F32)] * 4,
            temps=[((rows, MOBA_BLOCK), F32)] * 3 + [((rows, 2 * MOBA_BLOCK), BF16)]),
        name="moba_prompt",
    )(proj, kmean, _slope_feature_table(), kaug, vb)


def _moba_sample_kernel(pt_ref, q_ref, kn_ref, vn_ref, slope_ref, *rest, n_pages, t_dec):
    del pt_ref
    k_pages = rest[:n_pages]
    v_pages = rest[n_pages:2 * n_pages]
    o_ref = rest[2 * n_pages]
    s_scr, kb_scr = rest[2 * n_pages + 1:]
    past = n_pages * PAGE
    nblk = past // MOBA_BLOCK
    pages_per_blk = MOBA_BLOCK // PAGE
    rows = N_HEADS * t_dec
    assert rows == LANES

    q = q_ref[...]
    q_rows = jnp.concatenate([q[:, h * HD:(h + 1) * HD] for h in range(N_HEADS)], axis=0)
    q_t = q_rows.T.astype(BF16)
    lane = lax.broadcasted_iota(jnp.int32, (HD, LANES), 1)
    rows_per_kv = GROUP * t_dec
    zero = jnp.zeros((HD, LANES), BF16)
    q_bd = [jnp.where(lane // rows_per_kv == c, q_t, zero) for c in range(N_KV)]

    def scores_t(k2d_rows):
        acc = None
        for c in range(N_KV):
            part = jnp.dot(k2d_rows(c), q_bd[c], preferred_element_type=F32)
            acc = part if acc is None else acc + part
        return acc

    for n in range(nblk):
        sums = [jnp.zeros((1, HD), F32) for _ in range(N_KV)]
        for pp in range(pages_per_blk):
            p = n * pages_per_blk + pp
            kc = []
            for c in range(N_KV):
                kf = k_pages[p][pl.ds(c, PAGE, stride=N_KV), :]
                sums[c] = sums[c] + jnp.sum(kf, axis=0, keepdims=True)
                kc.append(kf.astype(BF16))
            s_scr[p * PAGE:(p + 1) * PAGE, :] = scores_t(lambda c: kc[c])
        for c in range(N_KV):
            kb_scr[c, n:n + 1, :] = sums[c] * (1.0 / MOBA_BLOCK)
    gate_t = scores_t(lambda c: kb_scr[c].astype(BF16))
    sel = _select_bias(gate_t, nblk)

    slope = slope_ref[...]
    t_q = lax.broadcasted_iota(jnp.int32, (1, LANES), 1) % t_dec
    q_pos = (past + t_q).astype(F32)

    def logits(raw, k_pos):
        return raw * SCALE - slope * (q_pos - k_pos)

    kn = kn_ref[...]
    vn = vn_ref[...]
    knc = [kn[:, c * HD:(c + 1) * HD].astype(BF16) for c in range(N_KV)]
    t_k = lax.broadcasted_iota(jnp.int32, (t_dec, LANES), 0)
    s_own = logits(scores_t(lambda c: knc[c]), (past + t_k).astype(F32))
    s_own = jnp.where(t_k <= t_q, s_own, NEG)
    m = jnp.max(s_own, axis=0, keepdims=True)

    sub = lax.broadcasted_iota(jnp.int32, (PAGE, LANES), 0)
    for p in range(n_pages):
        n = p // pages_per_blk
        k_pos = (sub + p * PAGE).astype(F32)
        s = logits(s_scr[p * PAGE:(p + 1) * PAGE, :], k_pos) + sel[n:n + 1, :]
        s_scr[p * PAGE:(p + 1) * PAGE, :] = s
        m = jnp.maximum(m, jnp.max(s, axis=0, keepdims=True))

    p_own = jnp.exp(s_own - m)
    l = jnp.sum(p_own, axis=0, keepdims=True)
    row_kv = lax.broadcasted_iota(jnp.int32, (LANES, HD), 0) // rows_per_kv
    contract0 = (((0,), (0,)), ((), ()))
    p_own_b = p_own.astype(BF16)
    acc = jnp.zeros((LANES, HD), F32)
    for c in range(N_KV):
        part = lax.dot_general(p_own_b, vn[:, c * HD:(c + 1) * HD].astype(BF16), contract0,
                               preferred_element_type=F32)
        acc = acc + jnp.where(row_kv == c, part, 0.0)
    for p in range(n_pages):
        pr = jnp.exp(s_scr[p * PAGE:(p + 1) * PAGE, :] - m)
        l = l + jnp.sum(pr, axis=0, keepdims=True)
        pb = pr.astype(BF16)
        for c in range(N_KV):
            vf = v_pages[p][pl.ds(c, PAGE, stride=N_KV), :].astype(BF16)
            part = lax.dot_general(pb, vf, contract0, preferred_element_type=F32)
            acc = acc + jnp.where(row_kv == c, part, 0.0)
    l_col = jnp.broadcast_to(l, (LANES, LANES)).T
    out = acc / l_col
    for h in range(N_HEADS):
        o_ref[:, h * HD:(h + 1) * HD] = out[h * t_dec:(h + 1) * t_dec, :].astype(o_ref.dtype)


def _moba_sample(proj, cache_k2d, cache_v2d, page_table, n_seq, t_dec):
    n_pages = page_table.shape[1]
    past = n_pages * PAGE
    nblk = past // MOBA_BLOCK
    page_rows = PAGE * N_KV
    slopes = np.asarray(2.0 ** (-8.0 * np.arange(1, N_HEADS + 1) / N_HEADS), np.float32)
    slope_row = jnp.asarray(np.repeat(slopes, t_dec)[None, :])

    def page_spec(p):
        return pl.BlockSpec((page_rows, HD), lambda b, pt, p=p: (pt[b, p], 0))

    grid_spec = pltpu.PrefetchScalarGridSpec(
        num_scalar_prefetch=1,
        grid=(n_seq,),
        in_specs=[
            pl.BlockSpec((t_dec, Q_A), lambda b, pt: (b, C_Q // Q_A)),
            pl.BlockSpec((t_dec, KV_A), lambda b, pt: (b, C_K // KV_A)),
            pl.BlockSpec((t_dec, KV_A), lambda b, pt: (b, C_V // KV_A)),
            pl.BlockSpec((1, LANES), lambda b, pt: (0, 0)),
        ] + [page_spec(p) for p in range(n_pages)] * 2,
        out_specs=pl.BlockSpec((t_dec, Q_A), lambda b, pt: (b, 0)),
        scratch_shapes=[
            pltpu.VMEM((past, LANES), F32),
            pltpu.VMEM((N_KV, nblk, HD), F32),
        ],
    )
    return pl.pallas_call(
        functools.partial(_moba_sample_kernel, n_pages=n_pages, t_dec=t_dec),
        grid_spec=grid_spec,
        out_shape=jax.ShapeDtypeStruct((n_seq * t_dec, Q_A), BF16),
        compiler_params=_params(
            ("arbitrary",),
            pipelined=[((t_dec, Q_A), F32), ((t_dec, KV_A), F32), ((t_dec, KV_A), F32), ((1, LANES), F32),
                       ((t_dec, Q_A), BF16)] + [((page_rows, HD), F32)] * (2 * n_pages),
            resident=[((past, LANES), F32), ((N_KV, nblk, HD), F32)],
            temps=[((PAGE, LANES), F32)] * 16),
        name="moba_sample",
    )(page_table, proj, proj, proj, slope_row, *([cache_k2d] * n_pages), *([cache_v2d] * n_pages))


DN_TILE = 128
PK_W, PK_QD, PK_KD, PK_QK = 0, 1, 2, 3
INV_BASE_LEVELS = 3


def _conv_silu(x, w, shifted):
    y = x * w[CONV_W - 1:CONV_W, :]
    for d in range(1, CONV_W):
        y = y + shifted(d) * w[CONV_W - 1 - d:CONV_W - d, :]
    return y * _sigmoid(y)


def _delta_prep_kernel(x_ref, halo_ref, ba_ref, cw_ref, al_ref, dt_ref, pk_ref, u_ref, ge_ref,
                       gate_scr, xe_scr, *, seg, tiles_per_step, tiles_per_seq):
    step = pl.program_id(0)
    h = pl.program_id(1)
    levels = int(math.log2(seg))
    ri = lax.broadcasted_iota(jnp.int32, (DN_TILE, DN_TILE), 0)
    ci = lax.broadcasted_iota(jnp.int32, (DN_TILE, DN_TILE), 1)
    same = jnp.where((ri >> levels) == (ci >> levels), 1.0, 0.0)
    incl = jnp.where(ci <= ri, same, 0.0)
    strict = jnp.where(ci < ri, same, 0.0)
    eye = jnp.where(ci == ri, 1.0, 0.0)
    base_levels = min(INV_BASE_LEVELS, levels)
    merge_masks = [jnp.where((ri >> lv) == (ci >> lv), 1.0, 0.0) for lv in range(base_levels, levels + 1)]
    idx_b = jnp.full((DN_TILE, LANES), h, jnp.int32)
    idx_g = idx_b + DN_HEADS
    cw = cw_ref[...]
    nt = (((1,), (1,)), ((), ()))

    def mm(x, y):
        return jnp.dot(x, y, preferred_element_type=F32)

    @pl.when(h == 0)
    def _():
        after = jnp.where(ci > ri, same, 0.0)
        prefix_suffix = jnp.concatenate([incl, after], axis=0).astype(BF16)
        pieces = []
        for t in range(tiles_per_step):
            ba = ba_ref[t * DN_TILE:(t + 1) * DN_TILE, :]
            gate_scr[t, 0] = _sigmoid(ba)
            xg = ba + dt_ref[...]
            softplus = jnp.maximum(xg, 0.0) + jnp.log(1.0 + jnp.exp(-jnp.abs(xg)))
            g = -jnp.exp(al_ref[...]) * softplus
            g_hi = g.astype(BF16)
            g_r = g - g_hi.astype(F32)
            g_mid = g_r.astype(BF16)
            pieces.append((g_hi, g_mid, (g_r - g_mid.astype(F32)).astype(BF16)))
        sums = [mm(prefix_suffix, p[0]) for p in pieces]
        for piece in (1, 2):
            sums = [sm + mm(prefix_suffix, p[piece]) for sm, p in zip(sums, pieces)]
        for t in range(tiles_per_step):
            gate_scr[t, 1] = sums[t][:DN_TILE]
            gate_scr[t, 2] = sums[t][DN_TILE:]

    if seg == DN_TILE:
        tile0 = step * tiles_per_step
        first = ((tile0 % tiles_per_seq) == 0).astype(F32)
        xe_scr[0:SUBLANES, :] = halo_ref[...] * (1.0 - first)
        xe_scr[SUBLANES:, :] = x_ref[...]

    def front(t):
        rows = slice(t * DN_TILE, (t + 1) * DN_TILE)
        x = x_ref[rows, :]
        if seg == DN_TILE:
            def shifted(d):
                return xe_scr[pl.ds(SUBLANES + t * DN_TILE - d, DN_TILE), :]
        else:
            bx = halo_ref[rows, :]
            t_in = lax.broadcasted_iota(jnp.int32, x.shape, 0) & (seg - 1)

            def shifted(d):
                xs = pltpu.roll(x, d, 0)
                bs = pltpu.roll(bx, (d - seg) % DN_TILE, 0)
                return jnp.where(t_in >= d, xs, bs)
        y = _conv_silu(x, cw, shifted)
        qc, kc, v = y[:, 0:DK], y[:, DK:2 * DK], y[:, 2 * DK:3 * DK]
        q = qc * lax.rsqrt(jnp.sum(qc * qc, axis=-1, keepdims=True) + EPS) * (DK ** -0.5)
        k = kc * lax.rsqrt(jnp.sum(kc * kc, axis=-1, keepdims=True) + EPS)

        def head_lane(j, idx):
            return jnp.take_along_axis(gate_scr[t, j], idx, axis=1, mode="promise_in_bounds")

        return dict(rows=rows, q=q, k=k, v=v, beta=head_lane(0, idx_b), cum=head_lane(1, idx_g),
                    rem=head_lane(2, idx_g), kb=k.astype(BF16), qb=q.astype(BF16))

    tiles = [front(t) for t in range(tiles_per_step)]

    for ts in tiles:
        ts["kk"] = lax.dot_general(ts["kb"], ts["kb"], nt, preferred_element_type=F32)
    for ts in tiles:
        ts["qk"] = lax.dot_general(ts["qb"], ts["kb"], nt, preferred_element_type=F32)
    for ts in tiles:
        diff = ts["cum"] - ts["cum"].T
        ts["decay"] = jnp.exp(jnp.where(incl > 0.0, diff, NEG))
        ts["a"] = strict * (ts["beta"] * ts["decay"] * ts["kk"])
        a0 = ts["a"] * merge_masks[0]
        ts["t_inv"] = eye - a0
        ts["pw"] = a0
    for _ in range(base_levels - 1):
        for ts in tiles:
            pwb = ts["pw"].astype(BF16)
            ts["pw"] = mm(pwb, pwb)
        for ts in tiles:
            ts["t_inv"] = ts["t_inv"] + mm(ts["t_inv"].astype(BF16), ts["pw"].astype(BF16))
    for lv in range(base_levels, levels):
        lmask = merge_masks[lv - base_levels + 1] - merge_masks[lv - base_levels]
        for ts in tiles:
            ts["tb"] = ts["t_inv"].astype(BF16)
            ts["tl"] = mm(ts["tb"], (ts["a"] * lmask).astype(BF16))
        for ts in tiles:
            ts["t_inv"] = ts["t_inv"] - mm(ts["tl"].astype(BF16), ts["tb"])
    for ts in tiles:
        ts["gam"] = jnp.exp(ts["cum"])
        ts["tb"] = ts["t_inv"].astype(BF16)
        ts["w"] = mm(ts["tb"], (ts["beta"] * ts["gam"] * ts["k"]).astype(BF16))
    for ts in tiles:
        ts["u"] = mm(ts["tb"], (ts["beta"] * ts["v"]).astype(BF16))
    segs = DN_TILE // seg
    for t, ts in enumerate(tiles):
        rows = ts["rows"]
        pk_ref[rows, PK_W * DK:(PK_W + 1) * DK] = ts["w"].astype(BF16)
        pk_ref[rows, PK_QD * DK:(PK_QD + 1) * DK] = (ts["q"] * ts["gam"]).astype(BF16)
        pk_ref[rows, PK_KD * DK:(PK_KD + 1) * DK] = (ts["k"] * jnp.exp(ts["rem"])).astype(BF16)
        pk_ref[rows, PK_QK * DK:(PK_QK + 1) * DK] = (ts["qk"] * ts["decay"]).astype(BF16)
        u_ref[rows, :] = ts["u"]
        g_end = jnp.exp(ts["cum"] + ts["rem"])
        for j in range(segs):
            ge_ref[t * segs + j] = g_end[j * seg:j * seg + 1, :]


def _delta_prep(proj, halo_src, conv_w_perm, al_row, dt_row, *, seg, t_len):
    n = proj.shape[0]
    tiles_per_step = 16 if n % (16 * DN_TILE) == 0 else (8 if n % (8 * DN_TILE) == 0 else 4)
    tr = tiles_per_step * DN_TILE
    assert n % tr == 0
    steps = n // tr
    qkv_blk = C_QKV // HEAD_COLS
    segs = tr // seg
    if seg == DN_TILE:
        assert t_len % tr == 0
        sub_per_step = tr // SUBLANES
        halo_spec = pl.BlockSpec(
            (SUBLANES, HEAD_COLS), lambda s, h: (jnp.maximum(s * sub_per_step - 1, 0), qkv_blk + h))
    else:
        halo_spec = pl.BlockSpec((tr, HEAD_COLS), lambda s, h: (s, h))
    return pl.pallas_call(
        functools.partial(_delta_prep_kernel, seg=seg, tiles_per_step=tiles_per_step,
                          tiles_per_seq=max(t_len // DN_TILE, 1)),
        grid=(steps, DN_HEADS),
        in_specs=[
            pl.BlockSpec((tr, HEAD_COLS), lambda s, h: (s, qkv_blk + h)),
            halo_spec,
            pl.BlockSpec((tr, LANES), lambda s, h: (s, C_BA // LANES)),
            pl.BlockSpec((CONV_W, HEAD_COLS), lambda s, h: (0, h)),
            pl.BlockSpec((1, LANES), lambda s, h: (0, 0)),
            pl.BlockSpec((1, LANES), lambda s, h: (0, 0)),
        ],
        out_specs=[
            pl.BlockSpec((None, tr, 4 * DK), lambda s, h: (h, s, 0)),
            pl.BlockSpec((None, tr, DV), lambda s, h: (h, s, 0)),
            pl.BlockSpec((None, segs, 1, LANES), lambda s, h: (h, s, 0, 0)),
        ],
        out_shape=[
            jax.ShapeDtypeStruct((DN_HEADS, n, 4 * DK), BF16),
            jax.ShapeDtypeStruct((DN_HEADS, n, DV), F32),
            jax.ShapeDtypeStruct((DN_HEADS, n // seg, 1, LANES), F32),
        ],
        scratch_shapes=[pltpu.VMEM((tiles_per_step, 3, DN_TILE, LANES), F32),
                        pltpu.VMEM((SUBLANES + tr, HEAD_COLS), F32)],
        compiler_params=_params(
            ("parallel", "arbitrary"),
            pipelined=[((tr, HEAD_COLS), F32), ((tr if seg != DN_TILE else SUBLANES, HEAD_COLS), F32),
                       ((tr, LANES), F32), ((CONV_W, HEAD_COLS), F32), ((1, LANES), F32), ((1, LANES), F32),
                       ((tr, 4 * DK), BF16), ((tr, DV), F32), ((segs, 1, LANES), F32)],
            resident=[((tiles_per_step, 3, DN_TILE, LANES), F32), ((SUBLANES + tr, HEAD_COLS), F32)],
            temps=[((DN_TILE, DN_TILE), F32)] * (24 * tiles_per_step)),
        name="delta_prep",
    )(proj, halo_src, proj, conv_w_perm, al_row, dt_row)


def _out_gate(o, z, dnw):
    return _rms(o, dnw) * (z * _sigmoid(z))


def _delta_scan_prompt_kernel(pk_ref, u_ref, ge_ref, z0_ref, z1_ref, dnw_ref, o_ref, sout_ref, s_scr, *, nb, hg):
    c = pl.program_id(0)

    @pl.when(c == 0)
    def _():
        s_scr[...] = jnp.zeros(s_scr.shape, F32)

    contract0 = (((0,), (0,)), ((), ()))
    chains = [(b, j) for b in range(nb) for j in range(hg)]
    res = []
    for b, j in chains:
        wq = jnp.concatenate([pk_ref[j, b, :, PK_W * DK:(PK_W + 1) * DK],
                              pk_ref[j, b, :, PK_QD * DK:(PK_QD + 1) * DK]], axis=0)
        res.append(jnp.dot(wq, s_scr[b, j].astype(BF16), preferred_element_type=F32))
    ub = [(u_ref[j, b] - r[:DN_TILE]).astype(BF16) for (b, j), r in zip(chains, res)]
    o = [r[DN_TILE:] + jnp.dot(pk_ref[j, b, :, PK_QK * DK:(PK_QK + 1) * DK], x, preferred_element_type=F32)
         for (b, j), r, x in zip(chains, res, ub)]
    upd = [lax.dot_general(pk_ref[j, b, :, PK_KD * DK:(PK_KD + 1) * DK], x, contract0,
                           preferred_element_type=F32) for (b, j), x in zip(chains, ub)]
    half = hg // 2
    for (b, j), oo, up in zip(chains, o, upd):
        s_scr[b, j] = ge_ref[j, b, 0] * s_scr[b, j] + up
        z_ref = z0_ref if j < half else z1_ref
        zc = (j % half) * DV
        o_ref[b, :, j * DV:(j + 1) * DV] = _out_gate(oo, z_ref[b, :, zc:zc + DV], dnw_ref[...]).astype(o_ref.dtype)

    @pl.when(c == pl.num_programs(0) - 1)
    def _():
        sout_ref[...] = s_scr[...]


def _delta_scan_prompt(pk, u, ge, proj, dnw_row, batch, t_len):
    hg = DN_HEADS
    zw = hg * DV // 2
    assert C_Z % zw == 0
    nc = t_len // DN_TILE
    zblk = C_Z // zw
    proj3 = proj.reshape(batch, t_len, P_DIM)
    o, s_final = pl.pallas_call(
        functools.partial(_delta_scan_prompt_kernel, nb=batch, hg=hg),
        grid=(nc,),
        in_specs=[
            pl.BlockSpec((hg, batch, DN_TILE, 4 * DK), lambda c: (0, 0, c, 0)),
            pl.BlockSpec((hg, batch, DN_TILE, DV), lambda c: (0, 0, c, 0)),
            pl.BlockSpec((hg, batch, 1, 1, LANES), lambda c: (0, 0, c, 0, 0)),
            pl.BlockSpec((batch, DN_TILE, zw), lambda c: (0, c, zblk)),
            pl.BlockSpec((batch, DN_TILE, zw), lambda c: (0, c, zblk + 1)),
            pl.BlockSpec((1, DV), lambda c: (0, 0)),
        ],
        out_specs=[
            pl.BlockSpec((batch, DN_TILE, hg * DV), lambda c: (0, c, 0)),
            pl.BlockSpec((batch, hg, DK, DV), lambda c: (0, 0, 0, 0)),
        ],
        out_shape=[
            jax.ShapeDtypeStruct((batch, t_len, DN_V), BF16),
            jax.ShapeDtypeStruct((batch, DN_HEADS, DK, DV), F32),
        ],
        scratch_shapes=[pltpu.VMEM((batch, hg, DK, DV), F32)],
        compiler_params=_params(
            ("arbitrary",),
            pipelined=[((hg, batch, DN_TILE, 4 * DK), BF16), ((hg, batch, DN_TILE, DV), F32),
                       ((hg, batch, 1, 1, LANES), F32), ((batch, DN_TILE, zw), F32), ((batch, DN_TILE, zw), F32),
                       ((1, DV), F32), ((batch, DN_TILE, hg * DV), BF16), ((batch, hg, DK, DV), F32)],
            resident=[((batch, hg, DK, DV), F32)],
            temps=[((batch * hg, 2 * DN_TILE, DV), F32)] + [((batch * hg, DN_TILE, DV), F32)] * 3),
        name="delta_scan_prompt",
    )(pk.reshape(DN_HEADS, batch, t_len, 4 * DK), u.reshape(DN_HEADS, batch, t_len, DV),
      ge.reshape(DN_HEADS, batch, nc, 1, LANES), proj3, proj3, dnw_row)
    return o.reshape(batch * t_len, DN_V), s_final


def _delta_scan_sample_kernel(pk_ref, u_ref, ge_ref, z_ref, dnw_ref, s0_ref, o_ref, sout_ref,
                              ub_scr, oq_scr, pk32_scr, *, hg, t_dec, seq_unroll):
    n_seq = DN_TILE // t_dec
    contract0 = (((0,), (0,)), ((), ()))
    for j in range(hg):
        pk32_scr[j] = pk_ref[j, :, 0:3 * DK].astype(F32)

    def group(gi, carry):
        pairs = []
        for a in range(seq_unroll):
            i = gi * seq_unroll + a
            r0 = pl.multiple_of(i * t_dec, t_dec)
            pairs += [(i, r0, j) for j in range(hg)]
        res = []
        for i, r0, j in pairs:
            wq = jnp.concatenate([pk32_scr[j, pl.ds(r0, t_dec), PK_W * DK:(PK_W + 1) * DK],
                                  pk32_scr[j, pl.ds(r0, t_dec), PK_QD * DK:(PK_QD + 1) * DK]], axis=0)
            res.append(jnp.dot(wq.astype(BF16), s0_ref[i, j].astype(BF16), preferred_element_type=F32))
        upd = []
        for (i, r0, j), r in zip(pairs, res):
            uu = u_ref[j, pl.ds(r0, t_dec), :] - r[:t_dec]
            ub_scr[j, pl.ds(r0, t_dec), :] = uu
            oq_scr[j, pl.ds(r0, t_dec), :] = r[t_dec:]
            kd = pk32_scr[j, pl.ds(r0, t_dec), PK_KD * DK:(PK_KD + 1) * DK]
            upd.append(lax.dot_general(kd.astype(BF16), uu.astype(BF16), contract0,
                                       preferred_element_type=F32))
        for (i, r0, j), up in zip(pairs, upd):
            sout_ref[i, j] = ge_ref[j, i] * s0_ref[i, j] + up
        return carry

    lax.fori_loop(0, n_seq // seq_unroll, group, 0)
    o = [oq_scr[j] + jnp.dot(pk_ref[j, :, PK_QK * DK:(PK_QK + 1) * DK], ub_scr[j].astype(BF16),
                             preferred_element_type=F32) for j in range(hg)]
    for j in range(hg):
        cols = slice(j * DV, (j + 1) * DV)
        o_ref[:, cols] = _out_gate(o[j], z_ref[:, cols], dnw_ref[...]).astype(o_ref.dtype)


def _delta_scan_sample(pk, u, ge, proj, dnw_row, s0, n_seq, t_dec):
    hg = 4
    seq_per_tile = DN_TILE // t_dec
    tiles = n_seq // seq_per_tile
    zblk = C_Z // (hg * DV)
    return pl.pallas_call(
        functools.partial(_delta_scan_sample_kernel, hg=hg, t_dec=t_dec, seq_unroll=4),
        grid=(tiles, DN_HEADS // hg),
        in_specs=[
            pl.BlockSpec((hg, DN_TILE, 4 * DK), lambda r, g: (g, r, 0)),
            pl.BlockSpec((hg, DN_TILE, DV), lambda r, g: (g, r, 0)),
            pl.BlockSpec((hg, seq_per_tile, 1, LANES), lambda r, g: (g, r, 0, 0)),
            pl.BlockSpec((DN_TILE, hg * DV), lambda r, g: (r, zblk + g)),
            pl.BlockSpec((1, DV), lambda r, g: (0, 0)),
            pl.BlockSpec((seq_per_tile, hg, DK, DV), lambda r, g: (r, g, 0, 0)),
        ],
        out_specs=[
            pl.BlockSpec((DN_TILE, hg * DV), lambda r, g: (r, g)),
            pl.BlockSpec((seq_per_tile, hg, DK, DV), lambda r, g: (r, g, 0, 0)),
        ],
        out_shape=[
            jax.ShapeDtypeStruct((n_seq * t_dec, DN_V), BF16),
            jax.ShapeDtypeStruct((n_seq, DN_HEADS, DK, DV), F32),
        ],
        scratch_shapes=[pltpu.VMEM((hg, DN_TILE, DV), F32), pltpu.VMEM((hg, DN_TILE, DV), F32),
                        pltpu.VMEM((hg, DN_TILE, 3 * DK), F32)],
        compiler_params=_params(
            ("parallel", "parallel"),
            pipelined=[((hg, DN_TILE, 4 * DK), BF16), ((hg, DN_TILE, DV), F32), ((hg, seq_per_tile, 1, LANES), F32),
                       ((DN_TILE, hg * DV), F32), ((1, DV), F32), ((seq_per_tile, hg, DK, DV), F32),
                       ((DN_TILE, hg * DV), BF16), ((seq_per_tile, hg, DK, DV), F32)],
            resident=[((hg, DN_TILE, DV), F32)] * 2 + [((hg, DN_TILE, 3 * DK), F32)],
            temps=[((DK, DV), F32)] * 32),
        name="delta_scan_sample",
    )(pk, u, ge, proj, dnw_row, s0)


def _mix_out_kernel(oa_ref, od_ref, wa_ref, wd_ref, wo_ref, ga0_ref, ga1_ref, gd0_ref, gd1_ref, x_ref, g_ref,
                    o_ref):
    ya = jnp.dot(oa_ref[...], wa_ref[...], preferred_element_type=F32)
    yd = jnp.dot(od_ref[...], wd_ref[...], preferred_element_type=F32)
    half = D_MODEL // 2
    mixed = jnp.concatenate(
        [_sigmoid(ga0_ref[...]) * ya[:, :half] + _sigmoid(gd0_ref[...]) * yd[:, :half],
         _sigmoid(ga1_ref[...]) * ya[:, half:] + _sigmoid(gd1_ref[...]) * yd[:, half:]], axis=1)
    t = jnp.dot(mixed.astype(BF16), wo_ref[...], preferred_element_type=F32)
    o_ref[...] = x_ref[...] + _rms(t, g_ref[...])


def _mix_out(oa, od, wa, wd, wo, proj, x2d, g_row):
    n = oa.shape[0]
    tm = 256
    half = D_MODEL // 2

    def resident(shape):
        return pl.BlockSpec(shape, lambda i: (0, 0), pipeline_mode=pl.Buffered(1))

    def gate(col):
        return pl.BlockSpec((tm, half), lambda i, c=col // half: (i, c))

    return pl.pallas_call(
        _mix_out_kernel,
        grid=(n // tm,),
        in_specs=[
            pl.BlockSpec((tm, Q_A), lambda i: (i, 0)),
            pl.BlockSpec((tm, DN_V), lambda i: (i, 0)),
            resident((Q_A, D_MODEL)),
            resident((DN_V, D_MODEL)),
            resident((D_MODEL, D_MODEL)),
            gate(C_GA), gate(C_GA + half), gate(C_GD), gate(C_GD + half),
            pl.BlockSpec((tm, D_MODEL), lambda i: (i, 0)),
            pl.BlockSpec((1, D_MODEL), lambda i: (0, 0)),
        ],
        out_specs=pl.BlockSpec((tm, D_MODEL), lambda i: (i, 0)),
        out_shape=jax.ShapeDtypeStruct((n, D_MODEL), F32),
        compiler_params=_params(
            ("parallel",),
            pipelined=[((tm, Q_A), BF16), ((tm, DN_V), BF16)] + [((tm, half), F32)] * 4
            + [((tm, D_MODEL), F32), ((1, D_MODEL), F32), ((tm, D_MODEL), F32)],
            resident=[((D_MODEL, D_MODEL), BF16)] * 3,
            temps=[((tm, D_MODEL), F32)] * 3 + [((tm, D_MODEL), BF16)]),
        name="mix_out",
    )(oa, od, wa, wd, wo, proj, proj, proj, proj, x2d, g_row)


def _mlp_kernel(x_ref, gpre_ref, wu_ref, wd_ref, gpost_ref, o_ref, h_ref, acc_ref):
    f = pl.program_id(1)

    @pl.when(f == 0)
    def _():
        h_ref[...] = _rms(x_ref[...], gpre_ref[...]).astype(BF16)
        acc_ref[...] = jnp.zeros(acc_ref.shape, F32)

    up = jnp.dot(h_ref[...], wu_ref[...], preferred_element_type=F32)
    act = jnp.square(jnp.maximum(up, 0.0)).astype(BF16)
    acc_ref[...] += jnp.dot(act, wd_ref[...], preferred_element_type=F32)

    @pl.when(f == pl.num_programs(1) - 1)
    def _():
        o_ref[...] = x_ref[...] + _rms(acc_ref[...], gpost_ref[...])


def _mlp(x2d, gpre, w_up, w_down, gpost):
    n = x2d.shape[0]
    tm, tf = 512, 1024
    return pl.pallas_call(
        _mlp_kernel,
        grid=(n // tm, D_FF // tf),
        in_specs=[
            pl.BlockSpec((tm, D_MODEL), lambda i, f: (i, 0)),
            pl.BlockSpec((1, D_MODEL), lambda i, f: (0, 0)),
            pl.BlockSpec((D_MODEL, tf), lambda i, f: (0, f)),
            pl.BlockSpec((tf, D_MODEL), lambda i, f: (f, 0)),
            pl.BlockSpec((1, D_MODEL), lambda i, f: (0, 0)),
        ],
        out_specs=pl.BlockSpec((tm, D_MODEL), lambda i, f: (i, 0)),
        out_shape=jax.ShapeDtypeStruct((n, D_MODEL), F32),
        scratch_shapes=[pltpu.VMEM((tm, D_MODEL), BF16), pltpu.VMEM((tm, D_MODEL), F32)],
        compiler_params=_params(
            ("parallel", "arbitrary"),
            pipelined=[((tm, D_MODEL), F32), ((1, D_MODEL), F32), ((D_MODEL, tf), BF16), ((tf, D_MODEL), BF16),
                       ((1, D_MODEL), F32), ((tm, D_MODEL), F32)],
            resident=[((tm, D_MODEL), BF16), ((tm, D_MODEL), F32)],
            temps=[((tm, tf), F32), ((tm, tf), BF16), ((tm, D_MODEL), F32)]),
        name="mlp",
    )(x2d, gpre, w_up, w_down, gpost)


def _head_major(x):
    lead = x.shape[:-1]
    return jnp.swapaxes(x.reshape(*lead, 3, DN_HEADS, DK), -3, -2).reshape(*lead, CONV_DIM)


def _part_major(x):
    lead = x.shape[:-1]
    return jnp.swapaxes(x.reshape(*lead, DN_HEADS, 3, DK), -3, -2).reshape(*lead, CONV_DIM)


W_QKV = Q_A + 2 * KV_A
W_Z = W_QKV + CONV_DIM
W_BA = W_Z + DN_V
W_GA = W_BA + 2 * DN_HEADS
W_DIM = W_GA + 2 * D_MODEL


def _pack_w_in_kernel(w_ref, o_ref):
    def cp(dst, src, n):
        o_ref[dst:dst + n, :] = w_ref[src:src + n, :].astype(BF16)

    cp(C_Q, 0, W_QKV)
    for h in range(DN_HEADS):
        for p in range(3):
            cp(C_QKV + h * HEAD_COLS + p * DK, W_QKV + p * DN_QK + h * DK, DK)
    cp(C_Z, W_Z, DN_V)
    cp(C_GA, W_GA, 2 * D_MODEL)
    o_ref[C_BA:P_DIM, :] = jnp.zeros((P_DIM - C_BA, o_ref.shape[1]), BF16)
    cp(C_BA, W_BA, 2 * DN_HEADS)


def _pack_w_in(w_in_t):
    tc = 256
    return pl.pallas_call(
        _pack_w_in_kernel,
        grid=(D_MODEL // tc,),
        in_specs=[pl.BlockSpec((W_DIM, tc), lambda i: (0, i))],
        out_specs=pl.BlockSpec((P_DIM, tc), lambda i: (0, i)),
        out_shape=jax.ShapeDtypeStruct((P_DIM, D_MODEL), BF16),
        compiler_params=_params(("parallel",), pipelined=[((W_DIM, tc), F32), ((P_DIM, tc), BF16)]),
        name="pack_w_in",
    )(w_in_t)


def _lane_row(vals, offset):
    return jnp.zeros((1, LANES), F32).at[0, offset:offset + vals.shape[0]].set(vals.astype(F32))


def _layer(x2d, attend, delta, w):
    proj = _in_proj(x2d, w["g_mix_pre"], w["w_in"])
    o_a = attend(proj)
    o_d, s_final = delta(proj)
    x1 = _mix_out(o_a, o_d, w["w_ba"], w["w_bd"], w["w_out"], proj, x2d, w["g_mix_post"])
    y = _mlp(x1, w["g_mlp_pre"], w["w_up"], w["w_down"], w["g_mlp_post"])
    return y, proj, s_final


def kernel(x_prompt, x_sample, cache_k, cache_v, state_delta, state_conv, page_table, w_in, conv_w, a_log,
           dt_bias, dn_norm_w, w_branch_attn, w_branch_delta, w_out, g_mix_pre, g_mix_post, g_mlp_pre,
           g_mlp_post, w_up, w_down):
    depth = w_in.shape[0]
    assert depth == 1
    b_p, t_p, _ = x_prompt.shape
    b_s, t_s, _ = x_sample.shape
    l = 0
    w = {
        "w_in": _pack_w_in(jnp.transpose(w_in[l])),
        "g_mix_pre": g_mix_pre[l][None, :],
        "g_mix_post": g_mix_post[l][None, :],
        "g_mlp_pre": g_mlp_pre[l][None, :],
        "g_mlp_post": g_mlp_post[l][None, :],
        "w_ba": w_branch_attn[l].astype(BF16),
        "w_bd": w_branch_delta[l].astype(BF16),
        "w_out": w_out[l].astype(BF16),
        "w_up": w_up[l].astype(BF16),
        "w_down": w_down[l].astype(BF16),
    }
    cw = _head_major(conv_w[l])
    al_row = _lane_row(a_log[l], DN_HEADS)
    dt_row = _lane_row(dt_bias[l], DN_HEADS)
    dnw_row = dn_norm_w[l][None, :]

    kv_prompt = []

    def attend_p(proj):
        kaug, vb, kmean, k5, v5 = _kprep(proj, b_p, t_p)
        kv_prompt.extend([k5[None], v5[None]])
        return _moba_prompt(proj, kaug, vb, kmean, b_p, t_p)

    def delta_p(proj):
        pk, u, ge = _delta_prep(proj, proj, cw, al_row, dt_row, seg=DN_TILE, t_len=t_p)
        return _delta_scan_prompt(pk, u, ge, proj, dnw_row, b_p, t_p)

    y_p, proj_p, d_p = _layer(x_prompt.reshape(b_p * t_p, D_MODEL), attend_p, delta_p, w)

    n_pool = cache_k.shape[1]
    ck2d = cache_k[l].reshape(n_pool * PAGE * N_KV, HD)
    cv2d = cache_v[l].reshape(n_pool * PAGE * N_KV, HD)
    conv_rows = _head_major(state_conv[l])
    halo_s = jnp.pad(conv_rows, ((0, 0), (t_s - (CONV_W - 1), 0), (0, 0))).reshape(b_s * t_s, CONV_DIM)

    def attend_s(proj):
        return _moba_sample(proj, ck2d, cv2d, page_table, b_s, t_s)

    def delta_s(proj):
        pk, u, ge = _delta_prep(proj, halo_s, cw, al_row, dt_row, seg=t_s, t_len=t_s)
        return _delta_scan_sample(pk, u, ge, proj, dnw_row, state_delta[l], b_s, t_s)

    y_s, proj_s, d_s = _layer(x_sample.reshape(b_s * t_s, D_MODEL), attend_s, delta_s, w)

    def kv_out(proj, b, t):
        k = proj[:, C_K:C_K + KV_A].reshape(1, b, t, N_KV, HD)
        v = proj[:, C_V:C_V + KV_A].reshape(1, b, t, N_KV, HD)
        return k, v

    def conv_out(proj, b, t):
        raw = proj.reshape(b, t, P_DIM)[:, t - (CONV_W - 1):, C_QKV:C_QKV + CONV_DIM]
        return _part_major(raw)[None]

    k_p, v_p = kv_prompt
    k_s, v_s = kv_out(proj_s, b_s, t_s)
    return (y_p.reshape(b_p, t_p, D_MODEL), y_s.reshape(b_s, t_s, D_MODEL), k_p, v_p, d_p[None],
            conv_out(proj_p, b_p, t_p), k_s, v_s, d_s[None], conv_out(proj_s, b_s, t_s))
```

```python
import functools
import math

import numpy as np
import jax
import jax.numpy as jnp
from jax import lax
from jax.experimental import pallas as pl
from jax.experimental.pallas import tpu as pltpu

F32 = jnp.float32
BF16 = jnp.bfloat16

D_MODEL = 2048
N_HEADS = 16
N_KV = 4
HD = 128
GROUP = N_HEADS // N_KV
MOBA_BLOCK = 256
MOBA_TOPK = 3
PAGE = 128
DN_HEADS = 16
DK = 128
DV = 128
CONV_W = 4
Q_A = N_HEADS * HD
KV_A = N_KV * HD
DN_QK = DN_HEADS * DK
DN_V = DN_HEADS * DV
CONV_DIM = 2 * DN_QK + DN_V
D_FF = 4 * D_MODEL
EPS = 1e-6

LANES = 128
SUBLANES = 8

C_Q = 0
C_K = C_Q + Q_A
C_V = C_K + KV_A
C_QKV = C_V + KV_A
C_Z = C_QKV + CONV_DIM
C_GA = C_Z + DN_V
C_GD = C_GA + D_MODEL
C_BA = C_GD + D_MODEL
P_DIM = C_BA + LANES
HEAD_COLS = 3 * DK

NEG = -1e30
SCALE = HD ** -0.5
EXP2_C = SCALE * math.log2(math.e)


V7X_VMEM_BUDGET = 56 * 2**20


def _nbytes(shape, dtype):
    return math.prod(shape) * jnp.dtype(dtype).itemsize


def _params(sem, pipelined, resident=(), temps=()):
    need = 2 * sum(_nbytes(*b) for b in pipelined) + sum(_nbytes(*b) for b in (*resident, *temps))
    limit = -(-need // 2**20) * 2**20
    assert limit <= V7X_VMEM_BUDGET, (limit, V7X_VMEM_BUDGET)
    return pltpu.CompilerParams(dimension_semantics=sem, vmem_limit_bytes=limit)


def _rms(x, g):
    return x * lax.rsqrt(jnp.mean(x * x, axis=-1, keepdims=True) + EPS) * g


def _sigmoid(x):
    return 1.0 / (1.0 + jnp.exp(-x))


def _in_proj_kernel(x_ref, g_ref, w_ref, o_ref, h_ref):
    @pl.when(pl.program_id(1) == 0)
    def _():
        h_ref[...] = _rms(x_ref[...], g_ref[...]).astype(BF16)

    o_ref[...] = lax.dot_general(h_ref[...], w_ref[...], (((1,), (1,)), ((), ())), preferred_element_type=F32)


def _in_proj(x2d, g_row, wp):
    n = x2d.shape[0]
    tm = 1024 if n % 1024 == 0 else 512
    tn = 1408
    assert n % tm == 0 and P_DIM % tn == 0
    return pl.pallas_call(
        _in_proj_kernel,
        grid=(n // tm, P_DIM // tn),
        in_specs=[
            pl.BlockSpec((tm, D_MODEL), lambda i, j: (i, 0)),
            pl.BlockSpec((1, D_MODEL), lambda i, j: (0, 0)),
            pl.BlockSpec((tn, D_MODEL), lambda i, j: (j, 0)),
        ],
        out_specs=pl.BlockSpec((tm, tn), lambda i, j: (i, j)),
        out_shape=jax.ShapeDtypeStruct((n, P_DIM), F32),
        scratch_shapes=[pltpu.VMEM((tm, D_MODEL), BF16)],
        compiler_params=_params(
            ("parallel", "arbitrary"),
            pipelined=[((tm, D_MODEL), F32), ((1, D_MODEL), F32), ((tn, D_MODEL), BF16), ((tm, tn), F32)],
            resident=[((tm, D_MODEL), BF16)], temps=[((tm, tn), F32)]),
        name="in_proj",
    )(x2d, g_row, wp)


N_SLOPE_FEATS = 4
F_SEL = 0
F_HI = 16
F_LO = F_HI + N_SLOPE_FEATS
MAX_BLOCKS = 16


def _kprep_kernel(k_ref, v_ref, kaug_ref, vb_ref, kmean_ref, k5_ref, v5_ref, *, blocks_per_step):
    step = pl.program_id(1)
    lane = lax.broadcasted_iota(jnp.int32, (MOBA_BLOCK, LANES), 1)
    row = lax.broadcasted_iota(jnp.int32, (MOBA_BLOCK, LANES), 0).astype(F32)
    is_hi = jnp.where(lane >= F_HI, jnp.where(lane < F_LO, 1.0, 0.0), 0.0)
    is_lo = jnp.where(lane >= F_LO, jnp.where(lane < F_LO + N_SLOPE_FEATS, 1.0, 0.0), 0.0)
    ones_col = jnp.where(lane == 0, 1.0, 0.0).astype(BF16)
    for j in range(blocks_per_step):
        n = step * blocks_per_step + j
        rows = slice(j * MOBA_BLOCK, (j + 1) * MOBA_BLOCK)
        k = k_ref[rows, :]
        v = v_ref[rows, :]
        mean = jnp.mean(k, axis=0, keepdims=True)
        feat = (jnp.where(lane == n, 1.0, 0.0)
                + is_hi * (n * MOBA_BLOCK).astype(F32)
                + is_lo * row).astype(BF16)
        for c in range(N_KV):
            cols = slice(c * HD, (c + 1) * HD)
            kaug_ref[c, rows, 0:HD] = k[:, cols].astype(BF16)
            kaug_ref[c, rows, HD:2 * HD] = feat
            vb_ref[c, rows, 0:HD] = v[:, cols].astype(BF16)
            vb_ref[c, rows, HD:2 * HD] = ones_col
            kmean_ref[c, j:j + 1, :] = mean[:, cols]
            k5_ref[rows, c, :] = k[:, cols]
            v5_ref[rows, c, :] = v[:, cols]


def _kprep(proj, batch, t_len):
    nblk = t_len // MOBA_BLOCK
    bps = min(nblk, 8)
    assert nblk % bps == 0 and nblk <= MAX_BLOCKS
    steps = nblk // bps
    rows = bps * MOBA_BLOCK
    return pl.pallas_call(
        functools.partial(_kprep_kernel, blocks_per_step=bps),
        grid=(batch, steps),
        in_specs=[
            pl.BlockSpec((rows, KV_A), lambda b, s: (b * steps + s, C_K // KV_A)),
            pl.BlockSpec((rows, KV_A), lambda b, s: (b * steps + s, C_V // KV_A)),
        ],
        out_specs=[
            pl.BlockSpec((None, N_KV, rows, 2 * HD), lambda b, s: (b, 0, s, 0)),
            pl.BlockSpec((None, N_KV, rows, 2 * HD), lambda b, s: (b, 0, s, 0)),
            pl.BlockSpec((None, N_KV, bps, HD), lambda b, s: (b, 0, s, 0)),
            pl.BlockSpec((None, rows, N_KV, HD), lambda b, s: (b, s, 0, 0)),
            pl.BlockSpec((None, rows, N_KV, HD), lambda b, s: (b, s, 0, 0)),
        ],
        out_shape=[
            jax.ShapeDtypeStruct((batch, N_KV, t_len, 2 * HD), BF16),
            jax.ShapeDtypeStruct((batch, N_KV, t_len, 2 * HD), BF16),
            jax.ShapeDtypeStruct((batch, N_KV, nblk, HD), F32),
            jax.ShapeDtypeStruct((batch, t_len, N_KV, HD), F32),
            jax.ShapeDtypeStruct((batch, t_len, N_KV, HD), F32),
        ],
        compiler_params=_params(
            ("parallel", "arbitrary"),
            pipelined=[((rows, KV_A), F32)] * 2 + [((N_KV, rows, 2 * HD), BF16)] * 2
            + [((N_KV, bps, HD), F32)] + [((rows, N_KV, HD), F32)] * 2),
        name="kprep",
    )(proj, proj)


def _slope_pieces():
    slopes = np.asarray(2.0 ** (-8.0 * np.arange(1, N_HEADS + 1) / N_HEADS), np.float32).astype(np.float64)
    x = slopes / SCALE
    pieces = []
    for _ in range(N_SLOPE_FEATS):
        p = x.astype(np.float32).astype(jnp.bfloat16).astype(np.float64)
        pieces.append(p)
        x = x - p
    return np.stack(pieces, axis=1)


def _slope_feature_table():
    pieces = _slope_pieces()
    tab = np.zeros((N_KV, 2 * N_SLOPE_FEATS, GROUP * MOBA_BLOCK), np.float32)
    for c in range(N_KV):
        for g in range(GROUP):
            cols = slice(g * MOBA_BLOCK, (g + 1) * MOBA_BLOCK)
            for f in range(N_SLOPE_FEATS):
                tab[c, f, cols] = pieces[c * GROUP + g, f]
                tab[c, N_SLOPE_FEATS + f, cols] = pieces[c * GROUP + g, f]
    return jnp.asarray(tab)


def _select_bias(gate_t, own):
    nblk = gate_t.shape[0]
    blk = lax.broadcasted_iota(jnp.int32, gate_t.shape, 0)
    past = blk < own
    gm = jnp.where(past, gate_t, -jnp.inf)
    rank = jnp.zeros(gate_t.shape, F32)
    for m in range(nblk):
        row = gm[m:m + 1, :]
        tie = jnp.where(blk > m, 1.0, 0.0)
        rank = rank + jnp.where(row > gm, 1.0, jnp.where(row == gm, tie, 0.0))
    keep_past = jnp.where(past, jnp.where(rank < MOBA_TOPK - 0.5, 0.0, NEG), NEG)
    return jnp.where(blk == own, 0.0, keep_past)


Q_TILES = 2


def _moba_prompt_kernel(q_ref, kmean_ref, sf_ref, kaug_ref, vb_ref, o_ref,
                        feat_t, qaug, m_s, acc_s, s_a, s_b, s_c, s_d, *, nblk):
    i0 = pl.program_id(2) * Q_TILES
    tile = GROUP * MOBA_BLOCK
    rows = Q_TILES * tile
    q = q_ref[...]
    qs = jnp.concatenate([q[a * MOBA_BLOCK:(a + 1) * MOBA_BLOCK, g * HD:(g + 1) * HD]
                          for a in range(Q_TILES) for g in range(GROUP)], axis=0).astype(BF16)
    qaug[:, 0:HD] = qs
    gate_t = lax.dot_general(kmean_ref[...].astype(BF16), qs, (((1,), (1,)), ((), ())),
                             preferred_element_type=F32)
    own = i0 + lax.broadcasted_iota(jnp.int32, (1, rows), 1) // tile
    feat_t[...] = jnp.zeros(feat_t.shape, F32)
    feat_t[F_SEL:F_SEL + nblk, :] = _select_bias(gate_t, own)
    for a in range(Q_TILES):
        feat_t[F_HI:F_HI + 2 * N_SLOPE_FEATS, a * tile:(a + 1) * tile] = sf_ref[...]
    qaug[:, HD:2 * HD] = feat_t[...].T.astype(BF16)

    def key_rows(n):
        return pl.ds(pl.multiple_of(n * MOBA_BLOCK, MOBA_BLOCK), MOBA_BLOCK)

    def raw_scores(n, r0=0):
        return lax.dot_general(qaug[r0:, :], kaug_ref[key_rows(n), :], (((1,), (1,)), ((), ())),
                               preferred_element_type=F32)

    def pair_scores(pr, bufs):
        n0 = jnp.maximum(jnp.minimum(2 * pr, i0 - 2), 0)
        bufs[0][...] = raw_scores(n0)
        bufs[1][...] = raw_scores(n0 + 1)

    def probs(s, m):
        return jnp.concatenate([jnp.exp2((s[:, :LANES] - m) * EXP2_C),
                                jnp.exp2((s[:, LANES:] - m) * EXP2_C)], axis=1).astype(BF16)

    def row_max(s):
        cur = jnp.max(jnp.maximum(s[:, :LANES], s[:, LANES:]), axis=-1, keepdims=True)
        return jnp.broadcast_to(cur, (s.shape[0], LANES))

    def update(s, n, r0=0):
        m_old = m_s[r0:, :]
        m_new = jnp.maximum(m_old, row_max(s))
        alpha = jnp.exp2((m_old - m_new) * EXP2_C)
        pv = jnp.dot(probs(s, m_new), vb_ref[key_rows(n), :], preferred_element_type=F32)
        acc_s[r0:, :] = jnp.concatenate([alpha, alpha], axis=1) * acc_s[r0:, :] + pv
        m_s[r0:, :] = m_new

    def causal(s):
        qi = lax.broadcasted_iota(jnp.int32, s.shape, 0) & (MOBA_BLOCK - 1)
        kj = lax.broadcasted_iota(jnp.int32, s.shape, 1)
        return kj <= qi

    m_s[...] = jnp.full(m_s.shape, NEG, F32)
    acc_s[...] = jnp.zeros(acc_s.shape, F32)
    for a in reversed(range(Q_TILES)):
        s = raw_scores(i0 + a, a * tile)
        if a == 0:
            pair_scores(0, (s_a, s_b))
        own_rows = lax.broadcasted_iota(jnp.int32, s.shape, 0) < tile
        s = jnp.where(causal(s), s, jnp.where(own_rows, NEG, s))
        update(s, i0 + a, a * tile)

    n_pairs = i0 // 2

    def pair_update(pr, bufs):
        s0 = bufs[0][...]
        s1 = bufs[1][...]
        m_old = m_s[...]
        m_new = jnp.maximum(jnp.maximum(m_old, row_max(s0)), row_max(s1))
        alpha = jnp.exp2((m_old - m_new) * EXP2_C)
        p = jnp.concatenate([probs(s0, m_new), probs(s1, m_new)], axis=1)
        v2 = vb_ref[pl.ds(pl.multiple_of(pr * 2 * MOBA_BLOCK, 2 * MOBA_BLOCK), 2 * MOBA_BLOCK), :]
        pv = jnp.dot(p, v2, preferred_element_type=F32)
        acc_s[...] = jnp.concatenate([alpha, alpha], axis=1) * acc_s[...] + pv
        m_s[...] = m_new

    def body(j, carry):
        pr = 2 * j
        pair_scores(pr + 1, (s_c, s_d))
        pair_update(pr, (s_a, s_b))
        pair_scores(pr + 2, (s_a, s_b))
        pair_update(pr + 1, (s_c, s_d))
        return carry

    lax.fori_loop(0, n_pairs // 2, body, 0)

    @pl.when(n_pairs % 2 == 1)
    def _():
        pair_update(n_pairs - 1, (s_a, s_b))

    acc = acc_s[...]
    out = acc[:, 0:HD] / acc[:, HD:HD + 1]
    for a in range(Q_TILES):
        for g in range(GROUP):
            r = a * tile + g * MOBA_BLOCK
            o_ref[a * MOBA_BLOCK:(a + 1) * MOBA_BLOCK, g * HD:(g + 1) * HD] = (
                out[r:r + MOBA_BLOCK, :].astype(o_ref.dtype))


def _moba_prompt(proj, kaug, vb, kmean, batch, t_len):
    nblk = t_len // MOBA_BLOCK
    assert nblk % Q_TILES == 0 and Q_TILES % 2 == 0
    nq = nblk // Q_TILES
    tile = GROUP * MOBA_BLOCK
    rows = Q_TILES * tile
    qrows = Q_TILES * MOBA_BLOCK
    qcols = GROUP * HD
    return pl.pallas_call(
        functools.partial(_moba_prompt_kernel, nblk=nblk),
        grid=(batch, N_KV, nq),
        in_specs=[
            pl.BlockSpec((qrows, qcols), lambda b, c, i: (b * nq + i, C_Q // qcols + c)),
            pl.BlockSpec((None, None, nblk, HD), lambda b, c, i: (b, c, 0, 0)),
            pl.BlockSpec((None, 2 * N_SLOPE_FEATS, tile), lambda b, c, i: (c, 0, 0)),
            pl.BlockSpec((None, None, t_len, 2 * HD), lambda b, c, i: (b, c, 0, 0)),
            pl.BlockSpec((None, None, t_len, 2 * HD), lambda b, c, i: (b, c, 0, 0)),
        ],
        out_specs=pl.BlockSpec((qrows, qcols), lambda b, c, i: (b * nq + i, c)),
        out_shape=jax.ShapeDtypeStruct((batch * t_len, Q_A), BF16),
        scratch_shapes=[
            pltpu.VMEM((LANES, rows), F32),
            pltpu.VMEM((rows, 2 * HD), BF16),
            pltpu.VMEM((rows, LANES), F32),
            pltpu.VMEM((rows, 2 * HD), F32),
            pltpu.VMEM((rows, MOBA_BLOCK), F32),
            pltpu.VMEM((rows, MOBA_BLOCK), F32),
            pltpu.VMEM((rows, MOBA_BLOCK), F32),
            pltpu.VMEM((rows, MOBA_BLOCK), F32),
        ],
        compiler_params=_params(
            ("parallel", "parallel", "arbitrary"),
            pipelined=[((qrows, qcols), F32), ((nblk, HD), F32), ((2 * N_SLOPE_FEATS, tile), F32),
                       ((t_len, 2 * HD), BF16), ((t_len, 2 * HD), BF16), ((qrows, qcols), BF16)],
            resident=[((LANES, rows), F32), ((rows, 2 * HD), BF16), ((rows, LANES), F32), ((rows, 2 * HD), F32)]
            + [((rows, MOBA_BLOCK), F32)] * 4,
            temps=[((rows, MOBA_BLOCK), F32)] * 3 + [((rows, 2 * MOBA_BLOCK), BF16)]),
        name="moba_prompt",
    )(proj, kmean, _slope_feature_table(), kaug, vb)


def _moba_sample_kernel(pt_ref, q_ref, kn_ref, vn_ref, slope_ref, *rest, n_pages, t_dec):
    del pt_ref
    k_pages = rest[:n_pages]
    v_pages = rest[n_pages:2 * n_pages]
    o_ref = rest[2 * n_pages]
    s_scr, kb_scr = rest[2 * n_pages + 1:]
    past = n_pages * PAGE
    nblk = past // MOBA_BLOCK
    pages_per_blk = MOBA_BLOCK // PAGE
    rows = N_HEADS * t_dec
    assert rows == LANES

    q = q_ref[...]
    q_rows = jnp.concatenate([q[:, h * HD:(h + 1) * HD] for h in range(N_HEADS)], axis=0)
    q_t = q_rows.T.astype(BF16)
    lane = lax.broadcasted_iota(jnp.int32, (HD, LANES), 1)
    rows_per_kv = GROUP * t_dec
    zero = jnp.zeros((HD, LANES), BF16)
    q_bd = [jnp.where(lane // rows_per_kv == c, q_t, zero) for c in range(N_KV)]

    def scores_t(k2d_rows):
        acc = None
        for c in range(N_KV):
            part = jnp.dot(k2d_rows(c), q_bd[c], preferred_element_type=F32)
            acc = part if acc is None else acc + part
        return acc

    for n in range(nblk):
        sums = [jnp.zeros((1, HD), F32) for _ in range(N_KV)]
        for pp in range(pages_per_blk):
            p = n * pages_per_blk + pp
            kc = []
            for c in range(N_KV):
                kf = k_pages[p][pl.ds(c, PAGE, stride=N_KV), :]
                sums[c] = sums[c] + jnp.sum(kf, axis=0, keepdims=True)
                kc.append(kf.astype(BF16))
            s_scr[p * PAGE:(p + 1) * PAGE, :] = scores_t(lambda c: kc[c])
        for c in range(N_KV):
            kb_scr[c, n:n + 1, :] = sums[c] * (1.0 / MOBA_BLOCK)
    gate_t = scores_t(lambda c: kb_scr[c].astype(BF16))
    sel = _select_bias(gate_t, nblk)

    slope = slope_ref[...]
    t_q = lax.broadcasted_iota(jnp.int32, (1, LANES), 1) % t_dec
    q_pos = (past + t_q).astype(F32)

    def logits(raw, k_pos):
        return raw * SCALE - slope * (q_pos - k_pos)

    kn = kn_ref[...]
    vn = vn_ref[...]
    knc = [kn[:, c * HD:(c + 1) * HD].astype(BF16) for c in range(N_KV)]
    t_k = lax.broadcasted_iota(jnp.int32, (t_dec, LANES), 0)
    s_own = logits(scores_t(lambda c: knc[c]), (past + t_k).astype(F32))
    s_own = jnp.where(t_k <= t_q, s_own, NEG)
    m = jnp.max(s_own, axis=0, keepdims=True)

    sub = lax.broadcasted_iota(jnp.int32, (PAGE, LANES), 0)
    for p in range(n_pages):
        n = p // pages_per_blk
        k_pos = (sub + p * PAGE).astype(F32)
        s = logits(s_scr[p * PAGE:(p + 1) * PAGE, :], k_pos) + sel[n:n + 1, :]
        s_scr[p * PAGE:(p + 1) * PAGE, :] = s
        m = jnp.maximum(m, jnp.max(s, axis=0, keepdims=True))

    p_own = jnp.exp(s_own - m)
    l = jnp.sum(p_own, axis=0, keepdims=True)
    row_kv = lax.broadcasted_iota(jnp.int32, (LANES, HD), 0) // rows_per_kv
    contract0 = (((0,), (0,)), ((), ()))
    p_own_b = p_own.astype(BF16)
    acc = jnp.zeros((LANES, HD), F32)
    for c in range(N_KV):
        part = lax.dot_general(p_own_b, vn[:, c * HD:(c + 1) * HD].astype(BF16), contract0,
                               preferred_element_type=F32)
        acc = acc + jnp.where(row_kv == c, part, 0.0)
    for p in range(n_pages):
        pr = jnp.exp(s_scr[p * PAGE:(p + 1) * PAGE, :] - m)
        l = l + jnp.sum(pr, axis=0, keepdims=True)
        pb = pr.astype(BF16)
        for c in range(N_KV):
            vf = v_pages[p][pl.ds(c, PAGE, stride=N_KV), :].astype(BF16)
            part = lax.dot_general(pb, vf, contract0, preferred_element_type=F32)
            acc = acc + jnp.where(row_kv == c, part, 0.0)
    l_col = jnp.broadcast_to(l, (LANES, LANES)).T
    out = acc / l_col
    for h in range(N_HEADS):
        o_ref[:, h * HD:(h + 1) * HD] = out[h * t_dec:(h + 1) * t_dec, :].astype(o_ref.dtype)


def _moba_sample(proj, cache_k2d, cache_v2d, page_table, n_seq, t_dec):
    n_pages = page_table.shape[1]
    past = n_pages * PAGE
    nblk = past // MOBA_BLOCK
    page_rows = PAGE * N_KV
    slopes = np.asarray(2.0 ** (-8.0 * np.arange(1, N_HEADS + 1) / N_HEADS), np.float32)
    slope_row = jnp.asarray(np.repeat(slopes, t_dec)[None, :])

    def page_spec(p):
        return pl.BlockSpec((page_rows, HD), lambda b, pt, p=p: (pt[b, p], 0))

    grid_spec = pltpu.PrefetchScalarGridSpec(
        num_scalar_prefetch=1,
        grid=(n_seq,),
        in_specs=[
            pl.BlockSpec((t_dec, Q_A), lambda b, pt: (b, C_Q // Q_A)),
            pl.BlockSpec((t_dec, KV_A), lambda b, pt: (b, C_K // KV_A)),
            pl.BlockSpec((t_dec, KV_A), lambda b, pt: (b, C_V // KV_A)),
            pl.BlockSpec((1, LANES), lambda b, pt: (0, 0)),
        ] + [page_spec(p) for p in range(n_pages)] * 2,
        out_specs=pl.BlockSpec((t_dec, Q_A), lambda b, pt: (b, 0)),
        scratch_shapes=[
            pltpu.VMEM((past, LANES), F32),
            pltpu.VMEM((N_KV, nblk, HD), F32),
        ],
    )
    return pl.pallas_call(
        functools.partial(_moba_sample_kernel, n_pages=n_pages, t_dec=t_dec),
        grid_spec=grid_spec,
        out_shape=jax.ShapeDtypeStruct((n_seq * t_dec, Q_A), BF16),
        compiler_params=_params(
            ("arbitrary",),
            pipelined=[((t_dec, Q_A), F32), ((t_dec, KV_A), F32), ((t_dec, KV_A), F32), ((1, LANES), F32),
                       ((t_dec, Q_A), BF16)] + [((page_rows, HD), F32)] * (2 * n_pages),
            resident=[((past, LANES), F32), ((N_KV, nblk, HD), F32)],
            temps=[((PAGE, LANES), F32)] * 16),
        name="moba_sample",
    )(page_table, proj, proj, proj, slope_row, *([cache_k2d] * n_pages), *([cache_v2d] * n_pages))


DN_TILE = 128
PK_W, PK_QD, PK_KD, PK_QK = 0, 1, 2, 3
INV_BASE_LEVELS = 3


def _conv_silu(x, w, shifted):
    y = x * w[CONV_W - 1:CONV_W, :]
    for d in range(1, CONV_W):
        y = y + shifted(d) * w[CONV_W - 1 - d:CONV_W - d, :]
    return y * _sigmoid(y)


def _delta_prep_kernel(x_ref, halo_ref, ba_ref, cw_ref, al_ref, dt_ref, pk_ref, u_ref, ge_ref,
                       gate_scr, xe_scr, *, seg, tiles_per_step, tiles_per_seq):
    step = pl.program_id(0)
    h = pl.program_id(1)
    levels = int(math.log2(seg))
    ri = lax.broadcasted_iota(jnp.int32, (DN_TILE, DN_TILE), 0)
    ci = lax.broadcasted_iota(jnp.int32, (DN_TILE, DN_TILE), 1)
    same = jnp.where((ri >> levels) == (ci >> levels), 1.0, 0.0)
    incl = jnp.where(ci <= ri, same, 0.0)
    strict = jnp.where(ci < ri, same, 0.0)
    eye = jnp.where(ci == ri, 1.0, 0.0)
    base_levels = min(INV_BASE_LEVELS, levels)
    merge_masks = [jnp.where((ri >> lv) == (ci >> lv), 1.0, 0.0) for lv in range(base_levels, levels + 1)]
    idx_b = jnp.full((DN_TILE, LANES), h, jnp.int32)
    idx_g = idx_b + DN_HEADS
    cw = cw_ref[...]
    nt = (((1,), (1,)), ((), ()))

    def mm(x, y):
        return jnp.dot(x, y, preferred_element_type=F32)

    @pl.when(h == 0)
    def _():
        after = jnp.where(ci > ri, same, 0.0)
        prefix_suffix = jnp.concatenate([incl, after], axis=0).astype(BF16)
        pieces = []
        for t in range(tiles_per_step):
            ba = ba_ref[t * DN_TILE:(t + 1) * DN_TILE, :]
            gate_scr[t, 0] = _sigmoid(ba)
            xg = ba + dt_ref[...]
            softplus = jnp.maximum(xg, 0.0) + jnp.log(1.0 + jnp.exp(-jnp.abs(xg)))
            g = -jnp.exp(al_ref[...]) * softplus
            g_hi = g.astype(BF16)
            g_r = g - g_hi.astype(F32)
            g_mid = g_r.astype(BF16)
            pieces.append((g_hi, g_mid, (g_r - g_mid.astype(F32)).astype(BF16)))
        sums = [mm(prefix_suffix, p[0]) for p in pieces]
        for piece in (1, 2):
            sums = [sm + mm(prefix_suffix, p[piece]) for sm, p in zip(sums, pieces)]
        for t in range(tiles_per_step):
            gate_scr[t, 1] = sums[t][:DN_TILE]
            gate_scr[t, 2] = sums[t][DN_TILE:]

    if seg == DN_TILE:
        tile0 = step * tiles_per_step
        first = ((tile0 % tiles_per_seq) == 0).astype(F32)
        xe_scr[0:SUBLANES, :] = halo_ref[...] * (1.0 - first)
        xe_scr[SUBLANES:, :] = x_ref[...]

    def front(t):
        rows = slice(t * DN_TILE, (t + 1) * DN_TILE)
        x = x_ref[rows, :]
        if seg == DN_TILE:
            def shifted(d):
                return xe_scr[pl.ds(SUBLANES + t * DN_TILE - d, DN_TILE), :]
        else:
            bx = halo_ref[rows, :]
            t_in = lax.broadcasted_iota(jnp.int32, x.shape, 0) & (seg - 1)

            def shifted(d):
                xs = pltpu.roll(x, d, 0)
                bs = pltpu.roll(bx, (d - seg) % DN_TILE, 0)
                return jnp.where(t_in >= d, xs, bs)
        y = _conv_silu(x, cw, shifted)
        qc, kc, v = y[:, 0:DK], y[:, DK:2 * DK], y[:, 2 * DK:3 * DK]
        q = qc * lax.rsqrt(jnp.sum(qc * qc, axis=-1, keepdims=True) + EPS) * (DK ** -0.5)
        k = kc * lax.rsqrt(jnp.sum(kc * kc, axis=-1, keepdims=True) + EPS)

        def head_lane(j, idx):
            return jnp.take_along_axis(gate_scr[t, j], idx, axis=1, mode="promise_in_bounds")

        return dict(rows=rows, q=q, k=k, v=v, beta=head_lane(0, idx_b), cum=head_lane(1, idx_g),
                    rem=head_lane(2, idx_g), kb=k.astype(BF16), qb=q.astype(BF16))

    tiles = [front(t) for t in range(tiles_per_step)]

    for ts in tiles:
        ts["kk"] = lax.dot_general(ts["kb"], ts["kb"], nt, preferred_element_type=F32)
    for ts in tiles:
        ts["qk"] = lax.dot_general(ts["qb"], ts["kb"], nt, preferred_element_type=F32)
    for ts in tiles:
        diff = ts["cum"] - ts["cum"].T
        ts["decay"] = jnp.exp(jnp.where(incl > 0.0, diff, NEG))
        ts["a"] = strict * (ts["beta"] * ts["decay"] * ts["kk"])
        a0 = ts["a"] * merge_masks[0]
        ts["t_inv"] = eye - a0
        ts["pw"] = a0
    for _ in range(base_levels - 1):
        for ts in tiles:
            pwb = ts["pw"].astype(BF16)
            ts["pw"] = mm(pwb, pwb)
        for ts in tiles:
            ts["t_inv"] = ts["t_inv"] + mm(ts["t_inv"].astype(BF16), ts["pw"].astype(BF16))
    for lv in range(base_levels, levels):
        lmask = merge_masks[lv - base_levels + 1] - merge_masks[lv - base_levels]
        for ts in tiles:
            ts["tb"] = ts["t_inv"].astype(BF16)
            ts["tl"] = mm(ts["tb"], (ts["a"] * lmask).astype(BF16))
        for ts in tiles:
            ts["t_inv"] = ts["t_inv"] - mm(ts["tl"].astype(BF16), ts["tb"])
    for ts in tiles:
        ts["gam"] = jnp.exp(ts["cum"])
        ts["tb"] = ts["t_inv"].astype(BF16)
        ts["w"] = mm(ts["tb"], (ts["beta"] * ts["gam"] * ts["k"]).astype(BF16))
    for ts in tiles:
        ts["u"] = mm(ts["tb"], (ts["beta"] * ts["v"]).astype(BF16))
    segs = DN_TILE // seg
    for t, ts in enumerate(tiles):
        rows = ts["rows"]
        pk_ref[rows, PK_W * DK:(PK_W + 1) * DK] = ts["w"].astype(BF16)
        pk_ref[rows, PK_QD * DK:(PK_QD + 1) * DK] = (ts["q"] * ts["gam"]).astype(BF16)
        pk_ref[rows, PK_KD * DK:(PK_KD + 1) * DK] = (ts["k"] * jnp.exp(ts["rem"])).astype(BF16)
        pk_ref[rows, PK_QK * DK:(PK_QK + 1) * DK] = (ts["qk"] * ts["decay"]).astype(BF16)
        u_ref[rows, :] = ts["u"]
        g_end = jnp.exp(ts["cum"] + ts["rem"])
        for j in range(segs):
            ge_ref[t * segs + j] = g_end[j * seg:j * seg + 1, :]


def _delta_prep(proj, halo_src, conv_w_perm, al_row, dt_row, *, seg, t_len):
    n = proj.shape[0]
    tiles_per_step = 16 if n % (16 * DN_TILE) == 0 else (8 if n % (8 * DN_TILE) == 0 else 4)
    tr = tiles_per_step * DN_TILE
    assert n % tr == 0
    steps = n // tr
    qkv_blk = C_QKV // HEAD_COLS
    segs = tr // seg
    if seg == DN_TILE:
        assert t_len % tr == 0
        sub_per_step = tr // SUBLANES
        halo_spec = pl.BlockSpec(
            (SUBLANES, HEAD_COLS), lambda s, h: (jnp.maximum(s * sub_per_step - 1, 0), qkv_blk + h))
    else:
        halo_spec = pl.BlockSpec((tr, HEAD_COLS), lambda s, h: (s, h))
    return pl.pallas_call(
        functools.partial(_delta_prep_kernel, seg=seg, tiles_per_step=tiles_per_step,
                          tiles_per_seq=max(t_len // DN_TILE, 1)),
        grid=(steps, DN_HEADS),
        in_specs=[
            pl.BlockSpec((tr, HEAD_COLS), lambda s, h: (s, qkv_blk + h)),
            halo_spec,
            pl.BlockSpec((tr, LANES), lambda s, h: (s, C_BA // LANES)),
            pl.BlockSpec((CONV_W, HEAD_COLS), lambda s, h: (0, h)),
            pl.BlockSpec((1, LANES), lambda s, h: (0, 0)),
            pl.BlockSpec((1, LANES), lambda s, h: (0, 0)),
        ],
        out_specs=[
            pl.BlockSpec((None, tr, 4 * DK), lambda s, h: (h, s, 0)),
            pl.BlockSpec((None, tr, DV), lambda s, h: (h, s, 0)),
            pl.BlockSpec((None, segs, 1, LANES), lambda s, h: (h, s, 0, 0)),
        ],
        out_shape=[
            jax.ShapeDtypeStruct((DN_HEADS, n, 4 * DK), BF16),
            jax.ShapeDtypeStruct((DN_HEADS, n, DV), F32),
            jax.ShapeDtypeStruct((DN_HEADS, n // seg, 1, LANES), F32),
        ],
        scratch_shapes=[pltpu.VMEM((tiles_per_step, 3, DN_TILE, LANES), F32),
                        pltpu.VMEM((SUBLANES + tr, HEAD_COLS), F32)],
        compiler_params=_params(
            ("parallel", "arbitrary"),
            pipelined=[((tr, HEAD_COLS), F32), ((tr if seg != DN_TILE else SUBLANES, HEAD_COLS), F32),
                       ((tr, LANES), F32), ((CONV_W, HEAD_COLS), F32), ((1, LANES), F32), ((1, LANES), F32),
                       ((tr, 4 * DK), BF16), ((tr, DV), F32), ((segs, 1, LANES), F32)],
            resident=[((tiles_per_step, 3, DN_TILE, LANES), F32), ((SUBLANES + tr, HEAD_COLS), F32)],
            temps=[((DN_TILE, DN_TILE), F32)] * (24 * tiles_per_step)),
        name="delta_prep",
    )(proj, halo_src, proj, conv_w_perm, al_row, dt_row)


def _out_gate(o, z, dnw):
    return _rms(o, dnw) * (z * _sigmoid(z))


def _delta_scan_prompt_kernel(pk_ref, u_ref, ge_ref, z0_ref, z1_ref, dnw_ref, o_ref, sout_ref, s_scr, *, nb, hg):
    c = pl.program_id(0)

    @pl.when(c == 0)
    def _():
        s_scr[...] = jnp.zeros(s_scr.shape, F32)

    contract0 = (((0,), (0,)), ((), ()))
    chains = [(b, j) for b in range(nb) for j in range(hg)]
    res = []
    for b, j in chains:
        wq = jnp.concatenate([pk_ref[j, b, :, PK_W * DK:(PK_W + 1) * DK],
                              pk_ref[j, b, :, PK_QD * DK:(PK_QD + 1) * DK]], axis=0)
        res.append(jnp.dot(wq, s_scr[b, j].astype(BF16), preferred_element_type=F32))
    ub = [(u_ref[j, b] - r[:DN_TILE]).astype(BF16) for (b, j), r in zip(chains, res)]
    o = [r[DN_TILE:] + jnp.dot(pk_ref[j, b, :, PK_QK * DK:(PK_QK + 1) * DK], x, preferred_element_type=F32)
         for (b, j), r, x in zip(chains, res, ub)]
    upd = [lax.dot_general(pk_ref[j, b, :, PK_KD * DK:(PK_KD + 1) * DK], x, contract0,
                           preferred_element_type=F32) for (b, j), x in zip(chains, ub)]
    half = hg // 2
    for (b, j), oo, up in zip(chains, o, upd):
        s_scr[b, j] = ge_ref[j, b, 0] * s_scr[b, j] + up
        z_ref = z0_ref if j < half else z1_ref
        zc = (j % half) * DV
        o_ref[b, :, j * DV:(j + 1) * DV] = _out_gate(oo, z_ref[b, :, zc:zc + DV], dnw_ref[...]).astype(o_ref.dtype)

    @pl.when(c == pl.num_programs(0) - 1)
    def _():
        sout_ref[...] = s_scr[...]


def _delta_scan_prompt(pk, u, ge, proj, dnw_row, batch, t_len):
    hg = DN_HEADS
    zw = hg * DV // 2
    assert C_Z % zw == 0
    nc = t_len // DN_TILE
    zblk = C_Z // zw
    proj3 = proj.reshape(batch, t_len, P_DIM)
    o, s_final = pl.pallas_call(
        functools.partial(_delta_scan_prompt_kernel, nb=batch, hg=hg),
        grid=(nc,),
        in_specs=[
            pl.BlockSpec((hg, batch, DN_TILE, 4 * DK), lambda c: (0, 0, c, 0)),
            pl.BlockSpec((hg, batch, DN_TILE, DV), lambda c: (0, 0, c, 0)),
            pl.BlockSpec((hg, batch, 1, 1, LANES), lambda c: (0, 0, c, 0, 0)),
            pl.BlockSpec((batch, DN_TILE, zw), lambda c: (0, c, zblk)),
            pl.BlockSpec((batch, DN_TILE, zw), lambda c: (0, c, zblk + 1)),
            pl.BlockSpec((1, DV), lambda c: (0, 0)),
        ],
        out_specs=[
            pl.BlockSpec((batch, DN_TILE, hg * DV), lambda c: (0, c, 0)),
            pl.BlockSpec((batch, hg, DK, DV), lambda c: (0, 0, 0, 0)),
        ],
        out_shape=[
            jax.ShapeDtypeStruct((batch, t_len, DN_V), BF16),
            jax.ShapeDtypeStruct((batch, DN_HEADS, DK, DV), F32),
        ],
        scratch_shapes=[pltpu.VMEM((batch, hg, DK, DV), F32)],
        compiler_params=_params(
            ("arbitrary",),
            pipelined=[((hg, batch, DN_TILE, 4 * DK), BF16), ((hg, batch, DN_TILE, DV), F32),
                       ((hg, batch, 1, 1, LANES), F32), ((batch, DN_TILE, zw), F32), ((batch, DN_TILE, zw), F32),
                       ((1, DV), F32), ((batch, DN_TILE, hg * DV), BF16), ((batch, hg, DK, DV), F32)],
            resident=[((batch, hg, DK, DV), F32)],
            temps=[((batch * hg, 2 * DN_TILE, DV), F32)] + [((batch * hg, DN_TILE, DV), F32)] * 3),
        name="delta_scan_prompt",
    )(pk.reshape(DN_HEADS, batch, t_len, 4 * DK), u.reshape(DN_HEADS, batch, t_len, DV),
      ge.reshape(DN_HEADS, batch, nc, 1, LANES), proj3, proj3, dnw_row)
    return o.reshape(batch * t_len, DN_V), s_final


def _delta_scan_sample_kernel(pk_ref, u_ref, ge_ref, z_ref, dnw_ref, s0_ref, o_ref, sout_ref,
                              ub_scr, oq_scr, pk32_scr, *, hg, t_dec, seq_unroll):
    n_seq = DN_TILE // t_dec
    contract0 = (((0,), (0,)), ((), ()))
    for j in range(hg):
        pk32_scr[j] = pk_ref[j, :, 0:3 * DK].astype(F32)

    def group(gi, carry):
        pairs = []
        for a in range(seq_unroll):
            i = gi * seq_unroll + a
            r0 = pl.multiple_of(i * t_dec, t_dec)
            pairs += [(i, r0, j) for j in range(hg)]
        res = []
        for i, r0, j in pairs:
            wq = jnp.concatenate([pk32_scr[j, pl.ds(r0, t_dec), PK_W * DK:(PK_W + 1) * DK],
                                  pk32_scr[j, pl.ds(r0, t_dec), PK_QD * DK:(PK_QD + 1) * DK]], axis=0)
            res.append(jnp.dot(wq.astype(BF16), s0_ref[i, j].astype(BF16), preferred_element_type=F32))
        upd = []
        for (i, r0, j), r in zip(pairs, res):
            uu = u_ref[j, pl.ds(r0, t_dec), :] - r[:t_dec]
            ub_scr[j, pl.ds(r0, t_dec), :] = uu
            oq_scr[j, pl.ds(r0, t_dec), :] = r[t_dec:]
            kd = pk32_scr[j, pl.ds(r0, t_dec), PK_KD * DK:(PK_KD + 1) * DK]
            upd.append(lax.dot_general(kd.astype(BF16), uu.astype(BF16), contract0,
                                       preferred_element_type=F32))
        for (i, r0, j), up in zip(pairs, upd):
            sout_ref[i, j] = ge_ref[j, i] * s0_ref[i, j] + up
        return carry

    lax.fori_loop(0, n_seq // seq_unroll, group, 0)
    o = [oq_scr[j] + jnp.dot(pk_ref[j, :, PK_QK * DK:(PK_QK + 1) * DK], ub_scr[j].astype(BF16),
                             preferred_element_type=F32) for j in range(hg)]
    for j in range(hg):
        cols = slice(j * DV, (j + 1) * DV)
        o_ref[:, cols] = _out_gate(o[j], z_ref[:, cols], dnw_ref[...]).astype(o_ref.dtype)


def _delta_scan_sample(pk, u, ge, proj, dnw_row, s0, n_seq, t_dec):
    hg = 8
    assert C_Z % (hg * DV) == 0
    seq_per_tile = DN_TILE // t_dec
    tiles = n_seq // seq_per_tile
    zblk = C_Z // (hg * DV)
    return pl.pallas_call(
        functools.partial(_delta_scan_sample_kernel, hg=hg, t_dec=t_dec, seq_unroll=2),
        grid=(tiles, DN_HEADS // hg),
        in_specs=[
            pl.BlockSpec((hg, DN_TILE, 4 * DK), lambda r, g: (g, r, 0)),
            pl.BlockSpec((hg, DN_TILE, DV), lambda r, g: (g, r, 0)),
            pl.BlockSpec((hg, seq_per_tile, 1, LANES), lambda r, g: (g, r, 0, 0)),
            pl.BlockSpec((DN_TILE, hg * DV), lambda r, g: (r, zblk + g)),
            pl.BlockSpec((1, DV), lambda r, g: (0, 0)),
            pl.BlockSpec((seq_per_tile, hg, DK, DV), lambda r, g: (r, g, 0, 0)),
        ],
        out_specs=[
            pl.BlockSpec((DN_TILE, hg * DV), lambda r, g: (r, g)),
            pl.BlockSpec((seq_per_tile, hg, DK, DV), lambda r, g: (r, g, 0, 0)),
        ],
        out_shape=[
            jax.ShapeDtypeStruct((n_seq * t_dec, DN_V), BF16),
            jax.ShapeDtypeStruct((n_seq, DN_HEADS, DK, DV), F32),
        ],
        scratch_shapes=[pltpu.VMEM((hg, DN_TILE, DV), F32), pltpu.VMEM((hg, DN_TILE, DV), F32),
                        pltpu.VMEM((hg, DN_TILE, 3 * DK), F32)],
        compiler_params=_params(
            ("parallel", "parallel"),
            pipelined=[((hg, DN_TILE, 4 * DK), BF16), ((hg, DN_TILE, DV), F32), ((hg, seq_per_tile, 1, LANES), F32),
                       ((DN_TILE, hg * DV), F32), ((1, DV), F32), ((seq_per_tile, hg, DK, DV), F32),
                       ((DN_TILE, hg * DV), BF16), ((seq_per_tile, hg, DK, DV), F32)],
            resident=[((hg, DN_TILE, DV), F32)] * 2 + [((hg, DN_TILE, 3 * DK), F32)],
            temps=[((DK, DV), F32)] * 32),
        name="delta_scan_sample",
    )(pk, u, ge, proj, dnw_row, s0)


def _mix_out_kernel(oa_ref, od_ref, wa_ref, wd_ref, wo_ref, ga0_ref, ga1_ref, gd0_ref, gd1_ref, x_ref, g_ref,
                    o_ref):
    ya = jnp.dot(oa_ref[...], wa_ref[...], preferred_element_type=F32)
    yd = jnp.dot(od_ref[...], wd_ref[...], preferred_element_type=F32)
    half = D_MODEL // 2
    mixed = jnp.concatenate(
        [_sigmoid(ga0_ref[...]) * ya[:, :half] + _sigmoid(gd0_ref[...]) * yd[:, :half],
         _sigmoid(ga1_ref[...]) * ya[:, half:] + _sigmoid(gd1_ref[...]) * yd[:, half:]], axis=1)
    t = jnp.dot(mixed.astype(BF16), wo_ref[...], preferred_element_type=F32)
    o_ref[...] = x_ref[...] + _rms(t, g_ref[...])


def _mix_out(oa, od, wa, wd, wo, proj, x2d, g_row):
    n = oa.shape[0]
    tm = 256
    half = D_MODEL // 2

    def resident(shape):
        return pl.BlockSpec(shape, lambda i: (0, 0), pipeline_mode=pl.Buffered(1))

    def gate(col):
        return pl.BlockSpec((tm, half), lambda i, c=col // half: (i, c))

    return pl.pallas_call(
        _mix_out_kernel,
        grid=(n // tm,),
        in_specs=[
            pl.BlockSpec((tm, Q_A), lambda i: (i, 0)),
            pl.BlockSpec((tm, DN_V), lambda i: (i, 0)),
            resident((Q_A, D_MODEL)),
            resident((DN_V, D_MODEL)),
            resident((D_MODEL, D_MODEL)),
            gate(C_GA), gate(C_GA + half), gate(C_GD), gate(C_GD + half),
            pl.BlockSpec((tm, D_MODEL), lambda i: (i, 0)),
            pl.BlockSpec((1, D_MODEL), lambda i: (0, 0)),
        ],
        out_specs=pl.BlockSpec((tm, D_MODEL), lambda i: (i, 0)),
        out_shape=jax.ShapeDtypeStruct((n, D_MODEL), F32),
        compiler_params=_params(
            ("parallel",),
            pipelined=[((tm, Q_A), BF16), ((tm, DN_V), BF16)] + [((tm, half), F32)] * 4
            + [((tm, D_MODEL), F32), ((1, D_MODEL), F32), ((tm, D_MODEL), F32)],
            resident=[((D_MODEL, D_MODEL), BF16)] * 3,
            temps=[((tm, D_MODEL), F32)] * 3 + [((tm, D_MODEL), BF16)]),
        name="mix_out",
    )(oa, od, wa, wd, wo, proj, proj, proj, proj, x2d, g_row)


def _mlp_kernel(x_ref, gpre_ref, wu_ref, wd_ref, gpost_ref, o_ref, h_ref, acc_ref):
    f = pl.program_id(1)

    @pl.when(f == 0)
    def _():
        h_ref[...] = _rms(x_ref[...], gpre_ref[...]).astype(BF16)
        acc_ref[...] = jnp.zeros(acc_ref.shape, F32)

    up = jnp.dot(h_ref[...], wu_ref[...], preferred_element_type=F32)
    act = jnp.square(jnp.maximum(up, 0.0)).astype(BF16)
    acc_ref[...] += jnp.dot(act, wd_ref[...], preferred_element_type=F32)

    @pl.when(f == pl.num_programs(1) - 1)
    def _():
        o_ref[...] = x_ref[...] + _rms(acc_ref[...], gpost_ref[...])


def _mlp(x2d, gpre, w_up, w_down, gpost):
    n = x2d.shape[0]
    tm, tf = 512, 1024
    return pl.pallas_call(
        _mlp_kernel,
        grid=(n // tm, D_FF // tf),
        in_specs=[
            pl.BlockSpec((tm, D_MODEL), lambda i, f: (i, 0)),
            pl.BlockSpec((1, D_MODEL), lambda i, f: (0, 0)),
            pl.BlockSpec((D_MODEL, tf), lambda i, f: (0, f)),
            pl.BlockSpec((tf, D_MODEL), lambda i, f: (f, 0)),
            pl.BlockSpec((1, D_MODEL), lambda i, f: (0, 0)),
        ],
        out_specs=pl.BlockSpec((tm, D_MODEL), lambda i, f: (i, 0)),
        out_shape=jax.ShapeDtypeStruct((n, D_MODEL), F32),
        scratch_shapes=[pltpu.VMEM((tm, D_MODEL), BF16), pltpu.VMEM((tm, D_MODEL), F32)],
        compiler_params=_params(
            ("parallel", "arbitrary"),
            pipelined=[((tm, D_MODEL), F32), ((1, D_MODEL), F32), ((D_MODEL, tf), BF16), ((tf, D_MODEL), BF16),
                       ((1, D_MODEL), F32), ((tm, D_MODEL), F32)],
            resident=[((tm, D_MODEL), BF16), ((tm, D_MODEL), F32)],
            temps=[((tm, tf), F32), ((tm, tf), BF16), ((tm, D_MODEL), F32)]),
        name="mlp",
    )(x2d, gpre, w_up, w_down, gpost)


def _head_major(x):
    lead = x.shape[:-1]
    return jnp.swapaxes(x.reshape(*lead, 3, DN_HEADS, DK), -3, -2).reshape(*lead, CONV_DIM)


def _part_major(x):
    lead = x.shape[:-1]
    return jnp.swapaxes(x.reshape(*lead, DN_HEADS, 3, DK), -3, -2).reshape(*lead, CONV_DIM)


W_QKV = Q_A + 2 * KV_A
W_Z = W_QKV + CONV_DIM
W_BA = W_Z + DN_V
W_GA = W_BA + 2 * DN_HEADS
W_DIM = W_GA + 2 * D_MODEL


def _pack_w_in_kernel(w_ref, o_ref):
    def cp(dst, src, n):
        o_ref[dst:dst + n, :] = w_ref[src:src + n, :].astype(BF16)

    cp(C_Q, 0, W_QKV)
    for h in range(DN_HEADS):
        for p in range(3):
            cp(C_QKV + h * HEAD_COLS + p * DK, W_QKV + p * DN_QK + h * DK, DK)
    cp(C_Z, W_Z, DN_V)
    cp(C_GA, W_GA, 2 * D_MODEL)
    o_ref[C_BA:P_DIM, :] = jnp.zeros((P_DIM - C_BA, o_ref.shape[1]), BF16)
    cp(C_BA, W_BA, 2 * DN_HEADS)


def _pack_w_in(w_in_t):
    tc = 256
    return pl.pallas_call(
        _pack_w_in_kernel,
        grid=(D_MODEL // tc,),
        in_specs=[pl.BlockSpec((W_DIM, tc), lambda i: (0, i))],
        out_specs=pl.BlockSpec((P_DIM, tc), lambda i: (0, i)),
        out_shape=jax.ShapeDtypeStruct((P_DIM, D_MODEL), BF16),
        compiler_params=_params(("parallel",), pipelined=[((W_DIM, tc), F32), ((P_DIM, tc), BF16)]),
        name="pack_w_in",
    )(w_in_t)


def _lane_row(vals, offset):
    return jnp.zeros((1, LANES), F32).at[0, offset:offset + vals.shape[0]].set(vals.astype(F32))


def _layer(x2d, attend, delta, w):
    proj = _in_proj(x2d, w["g_mix_pre"], w["w_in"])
    o_a = attend(proj)
    o_d, s_final = delta(proj)
    x1 = _mix_out(o_a, o_d, w["w_ba"], w["w_bd"], w["w_out"], proj, x2d, w["g_mix_post"])
    y = _mlp(x1, w["g_mlp_pre"], w["w_up"], w["w_down"], w["g_mlp_post"])
    return y, proj, s_final


def kernel(x_prompt, x_sample, cache_k, cache_v, state_delta, state_conv, page_table, w_in, conv_w, a_log,
           dt_bias, dn_norm_w, w_branch_attn, w_branch_delta, w_out, g_mix_pre, g_mix_post, g_mlp_pre,
           g_mlp_post, w_up, w_down):
    depth = w_in.shape[0]
    assert depth == 1
    b_p, t_p, _ = x_prompt.shape
    b_s, t_s, _ = x_sample.shape
    l = 0
    w = {
        "w_in": _pack_w_in(jnp.transpose(w_in[l])),
        "g_mix_pre": g_mix_pre[l][None, :],
        "g_mix_post": g_mix_post[l][None, :],
        "g_mlp_pre": g_mlp_pre[l][None, :],
        "g_mlp_post": g_mlp_post[l][None, :],
        "w_ba": w_branch_attn[l].astype(BF16),
        "w_bd": w_branch_delta[l].astype(BF16),
        "w_out": w_out[l].astype(BF16),
        "w_up": w_up[l].astype(BF16),
        "w_down": w_down[l].astype(BF16),
    }
    cw = _head_major(conv_w[l])
    al_row = _lane_row(a_log[l], DN_HEADS)
    dt_row = _lane_row(dt_bias[l], DN_HEADS)
    dnw_row = dn_norm_w[l][None, :]

    kv_prompt = []

    def attend_p(proj):
        kaug, vb, kmean, k5, v5 = _kprep(proj, b_p, t_p)
        kv_prompt.extend([k5[None], v5[None]])
        return _moba_prompt(proj, kaug, vb, kmean, b_p, t_p)

    def delta_p(proj):
        pk, u, ge = _delta_prep(proj, proj, cw, al_row, dt_row, seg=DN_TILE, t_len=t_p)
        return _delta_scan_prompt(pk, u, ge, proj, dnw_row, b_p, t_p)

    y_p, proj_p, d_p = _layer(x_prompt.reshape(b_p * t_p, D_MODEL), attend_p, delta_p, w)

    n_pool = cache_k.shape[1]
    ck2d = cache_k[l].reshape(n_pool * PAGE * N_KV, HD)
    cv2d = cache_v[l].reshape(n_pool * PAGE * N_KV, HD)
    conv_rows = _head_major(state_conv[l])
    halo_s = jnp.pad(conv_rows, ((0, 0), (t_s - (CONV_W - 1), 0), (0, 0))).reshape(b_s * t_s, CONV_DIM)

    def attend_s(proj):
        return _moba_sample(proj, ck2d, cv2d, page_table, b_s, t_s)

    def delta_s(proj):
        pk, u, ge = _delta_prep(proj, halo_s, cw, al_row, dt_row, seg=t_s, t_len=t_s)
        return _delta_scan_sample(pk, u, ge, proj, dnw_row, state_delta[l], b_s, t_s)

    y_s, proj_s, d_s = _layer(x_sample.reshape(b_s * t_s, D_MODEL), attend_s, delta_s, w)

    def kv_out(proj, b, t):
        k = proj[:, C_K:C_K + KV_A].reshape(1, b, t, N_KV, HD)
        v = proj[:, C_V:C_V + KV_A].reshape(1, b, t, N_KV, HD)
        return k, v

    def conv_out(proj, b, t):
        raw = proj.reshape(b, t, P_DIM)[:, t - (CONV_W - 1):, C_QKV:C_QKV + CONV_DIM]
        return _part_major(raw)[None]

    k_p, v_p = kv_prompt
    k_s, v_s = kv_out(proj_s, b_s, t_s)
    return (y_p.reshape(b_p, t_p, D_MODEL), y_s.reshape(b_s, t_s, D_MODEL), k_p, v_p, d_p[None],
            conv_out(proj_p, b_p, t_p), k_s, v_s, d_s[None], conv_out(proj_s, b_s, t_s))
```
